```python
import math
import jax, jax.numpy as jnp
from jax import lax
import numpy as np

D_MODEL = 1024
BATCH = 8
SEQ = 2048
DEPTH = 1

POOL_WIDTH = D_MODEL // 4
POOL_GROUPS = 4
POOL_GDIM = POOL_WIDTH // POOL_GROUPS
POOL_WINDOWS = (2, 4, 8, 16)
HEAD_DIM = 128
ATTN_WIDTH = D_MODEL - POOL_WIDTH
N_HEADS = ATTN_WIDTH // HEAD_DIM
N_KV_HEADS = 2
KV_WIDTH = N_KV_HEADS * HEAD_DIM
IDX_HEADS = 8
IDX_DIM = 64
TOPK_MAX = 256
QBLK = 128
ROPE_THETA = 500000.0
ROPE_DIM = HEAD_DIM // 4
IDX_ROPE_DIM = IDX_DIM // 4
N_EXPERTS = 32
TOP_K = 4
D_FF = D_MODEL
SWIGLU_LIMIT = 7.0
SWIGLU_ALPHA = 1.702
ALPHA_RES = (2.0 * DEPTH) ** 0.25
BETA_INIT = (8.0 * DEPTH) ** -0.25
LN_EPS = 1e-5
NEG_INF = -1e30
OFF_POOL = 0
OFF_Q = OFF_POOL + POOL_WIDTH
OFF_K = OFF_Q + ATTN_WIDTH
OFF_V = OFF_K + KV_WIDTH
OFF_IQ = OFF_V + KV_WIDTH
OFF_IK = OFF_IQ + IDX_HEADS * IDX_DIM
OFF_IW = OFF_IK + IDX_DIM
IN_WIDTH = OFF_IW + IDX_HEADS

kernel_name = "hymba_pool_dsa_moe_deepnorm"


def layer_norm(x, g, b):
    xf = x.astype(jnp.float32)
    mu = jnp.mean(xf, axis=-1, keepdims=True)
    var = jnp.mean(jnp.square(xf - mu), axis=-1, keepdims=True)
    y = (xf - mu) * lax.rsqrt(var + LN_EPS) * g.astype(jnp.float32) + b.astype(jnp.float32)
    return y.astype(x.dtype)


def apply_rope(x, positions, rd):
    inv_freq = ROPE_THETA ** (-jnp.arange(0, rd, 2, dtype=jnp.float32) / rd)
    ang = positions.astype(jnp.float32)[..., None] * inv_freq
    cos = jnp.cos(ang)[:, :, None, :]
    sin = jnp.sin(ang)[:, :, None, :]
    xr = x[..., :rd].astype(jnp.float32)
    x1, x2 = xr[..., : rd // 2], xr[..., rd // 2:]
    rot = jnp.concatenate([x1 * cos - x2 * sin, x2 * cos + x1 * sin], axis=-1)
    return jnp.concatenate([rot.astype(x.dtype), x[..., rd:]], axis=-1)


def pool_mixer(u, w_pool, pool_scale):
    B, L, C = u.shape
    uf = u.astype(jnp.float32)
    cs = jnp.concatenate([jnp.zeros((B, 1, C), jnp.float32), jnp.cumsum(uf, axis=1)], axis=1)
    t = jnp.arange(L)
    diffs = []
    for gi, w in enumerate(POOL_WINDOWS):
        sl = slice(gi * POOL_GDIM, (gi + 1) * POOL_GDIM)
        c = cs[..., sl]
        lo = jnp.maximum(t + 1 - w, 0)
        win_sum = c[:, t + 1] - c[:, lo]
        cnt = jnp.minimum(t + 1, w).astype(jnp.float32)[None, :, None]
        diffs.append(win_sum / cnt - uf[..., sl])
    d = jnp.stack(diffs, axis=2)
    y = jnp.einsum('blgc,gcd->blgd', d, w_pool.astype(jnp.float32)).reshape(B, L, C)
    return (y * pool_scale.astype(jnp.float32)).astype(u.dtype)


def sparse_attention(q, k, v, iq, ik, iw):
    B, L = q.shape[:2]
    n_sel = min(TOPK_MAX, L // 4)
    nb = L // QBLK
    key_pos = jnp.arange(L)
    ikf = ik.astype(jnp.float32)
    w_scale = (IDX_HEADS ** -0.5) * (IDX_DIM ** -0.5)

    def to_blocks(a):
        return a.reshape((B, nb, QBLK) + a.shape[2:]).swapaxes(0, 1)

    def block(args):
        qb, iqb, iwb, q_pos = args
        s = jnp.einsum('bqhd,bsd->bqhs', iqb.astype(jnp.float32), ikf)
        idx_score = jnp.einsum('bqhs,bqh->bqs', jax.nn.relu(s), iwb.astype(jnp.float32) * w_scale)
        causal = key_pos[None, :] <= q_pos[:, None]
        idx_score = jnp.where(causal[None], idx_score, NEG_INF)
        _, sel = lax.top_k(idx_score, n_sel)
        k_sel = jax.vmap(lambda kb, ib: kb[ib])(k, sel)
        v_sel = jax.vmap(lambda vb, ib: vb[ib])(v, sel)
        qg = qb.reshape(B, QBLK, N_KV_HEADS, N_HEADS // N_KV_HEADS, HEAD_DIM).astype(jnp.float32)
        logits = jnp.einsum('bqgrd,bqkgd->bqgrk', qg, k_sel.astype(jnp.float32)) * (HEAD_DIM ** -0.5)
        valid = sel <= q_pos[None, :, None]
        logits = jnp.where(valid[:, :, None, None, :], logits, NEG_INF)
        p = jax.nn.softmax(logits, axis=-1)
        o = jnp.einsum('bqgrk,bqkgd->bqgrd', p, v_sel.astype(jnp.float32))
        return o.reshape(B, QBLK, ATTN_WIDTH).astype(q.dtype)

    out = lax.map(block, (to_blocks(q), to_blocks(iq), to_blocks(iw), key_pos.reshape(nb, QBLK)))
    return out.swapaxes(0, 1).reshape(B, L, ATTN_WIDTH)


def moe(h, w_router, b_router, w_gate_up, b_gate_up, w_down, b_down):
    B, L, D = h.shape
    xf = h.reshape(-1, D)
    logits = xf.astype(jnp.float32) @ w_router.astype(jnp.float32) + b_router.astype(jnp.float32)
    top_logit, top_e = lax.top_k(logits, TOP_K)
    gates = jax.nn.softmax(top_logit, axis=-1)
    flat_e = top_e.reshape(-1)
    order = jnp.argsort(flat_e)
    e_sorted = flat_e[order]
    tok = order // TOP_K
    g_sorted = gates.reshape(-1)[order]
    group_sizes = jnp.bincount(flat_e, length=N_EXPERTS).astype(jnp.int32)
    xs = xf[tok]
    hu = lax.ragged_dot(xs, w_gate_up, group_sizes) + b_gate_up[e_sorted]
    gate = jnp.minimum(hu[:, :D_FF], SWIGLU_LIMIT)
    up = jnp.clip(hu[:, D_FF:], -SWIGLU_LIMIT, SWIGLU_LIMIT)
    act = (up + 1.0) * gate * jax.nn.sigmoid(SWIGLU_ALPHA * gate)
    y = lax.ragged_dot(act, w_down, group_sizes) + b_down[e_sorted]
    y = y * g_sorted[:, None].astype(y.dtype)
    out = jnp.zeros_like(xf).at[tok].add(y)
    return out.reshape(B, L, D)


def setup_inputs(seed: int = 0) -> dict:
    key = jax.random.key(seed)
    ks = jax.random.split(key, 20)
    f32 = jnp.float32
    nrm = lambda k, shape: jax.random.normal(k, shape, f32)
    x = nrm(ks[0], (BATCH, SEQ, D_MODEL))
    offs = jax.random.randint(ks[1], (BATCH, 1), 0, 4096, dtype=jnp.int32)
    positions = (offs + jnp.arange(SEQ, dtype=jnp.int32)[None, :]).astype(jnp.int32)
    ln_in_g = 1.0 + 0.02 * nrm(ks[2], (D_MODEL,))
    ln_in_b = 0.02 * nrm(ks[3], (D_MODEL,))
    col_scale = jnp.concatenate([
        jnp.ones((OFF_V,), f32),
        jnp.full((KV_WIDTH,), BETA_INIT, f32),
        jnp.ones((IN_WIDTH - OFF_IQ,), f32)])
    w_in = nrm(ks[4], (DEPTH, D_MODEL, IN_WIDTH)) * (D_MODEL ** -0.5) * col_scale
    w_pool = nrm(ks[5], (DEPTH, POOL_GROUPS, POOL_GDIM, POOL_GDIM)) * (POOL_GDIM ** -0.5)
    pool_scale = 1.0 + 0.02 * nrm(ks[6], (DEPTH, POOL_WIDTH))
    w_out = nrm(ks[7], (DEPTH, D_MODEL, D_MODEL)) * (D_MODEL ** -0.5) * BETA_INIT
    ln1_g = 1.0 + 0.02 * nrm(ks[8], (DEPTH, D_MODEL))
    ln1_b = 0.02 * nrm(ks[9], (DEPTH, D_MODEL))
    w_router = nrm(ks[10], (DEPTH, D_MODEL, N_EXPERTS)) * (D_MODEL ** -0.5)
    b_router = 0.01 * nrm(ks[11], (DEPTH, N_EXPERTS))
    w_gate_up = nrm(ks[12], (DEPTH, N_EXPERTS, D_MODEL, 2 * D_FF)) * (D_MODEL ** -0.5) * BETA_INIT
    b_gate_up = 0.01 * nrm(ks[13], (DEPTH, N_EXPERTS, 2 * D_FF))
    w_down = nrm(ks[14], (DEPTH, N_EXPERTS, D_FF, D_MODEL)) * (D_FF ** -0.5) * BETA_INIT
    b_down = 0.01 * nrm(ks[15], (DEPTH, N_EXPERTS, D_MODEL))
    ln2_g = 1.0 + 0.02 * nrm(ks[16], (DEPTH, D_MODEL))
    ln2_b = 0.02 * nrm(ks[17], (DEPTH, D_MODEL))
    return {"x": x, "positions": positions, "ln_in_g": ln_in_g, "ln_in_b": ln_in_b,
            "w_in": w_in, "w_pool": w_pool, "pool_scale": pool_scale, "w_out": w_out,
            "ln1_g": ln1_g, "ln1_b": ln1_b, "w_router": w_router, "b_router": b_router,
            "w_gate_up": w_gate_up, "b_gate_up": b_gate_up, "w_down": w_down, "b_down": b_down,
            "ln2_g": ln2_g, "ln2_b": ln2_b}


def reference(x, positions, ln_in_g, ln_in_b, w_in, w_pool, pool_scale, w_out, ln1_g, ln1_b,
              w_router, b_router, w_gate_up, b_gate_up, w_down, b_down, ln2_g, ln2_b):
    B, L, _ = x.shape
    h = layer_norm(x, ln_in_g, ln_in_b)
    for l in range(DEPTH):
        proj = jnp.einsum('bld,de->ble', h, w_in[l])
        u = proj[..., OFF_POOL:OFF_Q]
        q = proj[..., OFF_Q:OFF_K].reshape(B, L, N_HEADS, HEAD_DIM)
        k = proj[..., OFF_K:OFF_V].reshape(B, L, N_KV_HEADS, HEAD_DIM)
        v = proj[..., OFF_V:OFF_IQ].reshape(B, L, N_KV_HEADS, HEAD_DIM)
        iq = proj[..., OFF_IQ:OFF_IK].reshape(B, L, IDX_HEADS, IDX_DIM)
        ik = proj[..., OFF_IK:OFF_IW].reshape(B, L, 1, IDX_DIM)
        iw = proj[..., OFF_IW:IN_WIDTH]
        q = apply_rope(q, positions, ROPE_DIM)
        k = apply_rope(k, positions, ROPE_DIM)
        iq = apply_rope(iq, positions, IDX_ROPE_DIM)
        ik = apply_rope(ik, positions, IDX_ROPE_DIM)[:, :, 0, :]
        pool_out = pool_mixer(u, w_pool[l], pool_scale[l])
        attn_out = sparse_attention(q, k, v, iq, ik, iw)
        mix = jnp.einsum('ble,ed->bld', jnp.concatenate([pool_out, attn_out], axis=-1), w_out[l])
        h = layer_norm(ALPHA_RES * h + mix, ln1_g[l], ln1_b[l])
        ffn = moe(h, w_router[l], b_router[l], w_gate_up[l], b_gate_up[l], w_down[l], b_down[l])
        h = layer_norm(ALPHA_RES * h + ffn, ln2_g[l], ln2_b[l])
    return h
```

```python
import functools

import jax
import jax.numpy as jnp
import numpy as np
from jax import lax
from jax.experimental import pallas as pl
from jax.experimental.pallas import tpu as pltpu

F32 = jnp.float32
BF16 = jnp.bfloat16

D_MODEL = 1024
POOL_WIDTH = 256
POOL_GROUPS = 4
POOL_GDIM = 64
POOL_WINDOWS = (2, 4, 8, 16)
POOL_HALO = 16
HEAD_DIM = 128
ATTN_WIDTH = 768
N_HEADS = 6
N_KV_HEADS = 2
KV_WIDTH = 256
IDX_HEADS = 8
IDX_DIM = 64
TOPK_MAX = 256
ROPE_THETA = 500000.0
ROPE_DIM = 32
IDX_ROPE_DIM = 16
N_EXPERTS = 32
TOP_K = 4
D_FF = 1024
SWIGLU_LIMIT = 7.0
SWIGLU_ALPHA = 1.702
DEPTH = 1
ALPHA_RES = (2.0 * DEPTH) ** 0.25
LN_EPS = 1e-5
NEG_INF = -1e30
OFF_Q = 256
OFF_K = 1024
OFF_V = 1280
OFF_IQ = 1536
OFF_IK = 2048
IN_WIDTH = 2120
IN_WIDTH_PAD = 2176

LANES = 128
VMEM_LIMIT = 48 * 1024 * 1024
INT_MIN = -2 ** 31


def _layer_norm(x, g, b):
    mu = jnp.mean(x, axis=-1, keepdims=True)
    xc = x - mu
    var = jnp.mean(xc * xc, axis=-1, keepdims=True)
    return xc * lax.rsqrt(var + LN_EPS) * g + b


def _rope(xh, cos, sin_lo, sin_hi, half):
    return (xh * cos + pltpu.roll(xh, LANES - half, 1) * sin_lo
            + pltpu.roll(xh, half, 1) * sin_hi)


def _inproj_kernel(x_ref, pos_ref, g_ref, b_ref, w_ref, tab_ref,
                   u_ref, q_ref, k_ref, v_ref, iq_ref, ik_ref, iw_ref):
    h = _layer_norm(x_ref[...], g_ref[...], b_ref[...])
    proj = jnp.dot(h.astype(BF16), w_ref[...], preferred_element_type=F32)
    pos = pos_ref[...].astype(F32)

    ang = pos * tab_ref[0:1, :]
    cos_q = jnp.cos(ang)
    sin_q = jnp.sin(ang)
    sin_q_lo = sin_q * tab_ref[1:2, :]
    sin_q_hi = sin_q * tab_ref[2:3, :]
    ang = pos * tab_ref[3:4, :]
    cos_i = jnp.cos(ang)
    sin_i = jnp.sin(ang)
    sin_i_lo = sin_i * tab_ref[4:5, :]
    sin_i_hi = sin_i * tab_ref[5:6, :]

    u_ref[...] = proj[:, 0:OFF_Q]
    scale = HEAD_DIM ** -0.5
    for hh in range(N_HEADS):
        c0 = OFF_Q + hh * HEAD_DIM
        r = _rope(proj[:, c0:c0 + HEAD_DIM], cos_q, sin_q_lo, sin_q_hi, ROPE_DIM // 2)
        q_ref[:, hh * HEAD_DIM:(hh + 1) * HEAD_DIM] = (r * scale).astype(BF16)
    for hh in range(N_KV_HEADS):
        c0 = OFF_K + hh * HEAD_DIM
        r = _rope(proj[:, c0:c0 + HEAD_DIM], cos_q, sin_q_lo, sin_q_hi, ROPE_DIM // 2)
        k_ref[:, hh * HEAD_DIM:(hh + 1) * HEAD_DIM] = r.astype(BF16)
    v_ref[...] = proj[:, OFF_V:OFF_IQ].astype(BF16)
    for t in range(IDX_HEADS * IDX_DIM // LANES):
        c0 = OFF_IQ + t * LANES
        r = _rope(proj[:, c0:c0 + LANES], cos_i, sin_i_lo, sin_i_hi, IDX_ROPE_DIM // 2)
        iq_ref[:, t * LANES:(t + 1) * LANES] = r.astype(BF16)
    tail = proj[:, OFF_IK:OFF_IK + LANES]
    r = _rope(tail, cos_i, sin_i_lo, sin_i_hi, IDX_ROPE_DIM // 2)
    ik_ref[...] = r[:, 0:IDX_DIM].astype(BF16)
    iw_ref[...] = tail[:, IDX_DIM:IDX_DIM + IDX_HEADS]


def _rope_tables():
    lane = np.arange(LANES)
    tab = np.zeros((8, LANES), np.float32)
    f_q = ROPE_THETA ** (-jnp.arange(0, ROPE_DIM, 2, dtype=F32) / ROPE_DIM)
    f_i = ROPE_THETA ** (-jnp.arange(0, IDX_ROPE_DIM, 2, dtype=F32) / IDX_ROPE_DIM)
    hq, hi = ROPE_DIM // 2, IDX_ROPE_DIM // 2
    in_q = lane < ROPE_DIM
    li = lane % IDX_DIM
    in_i = li < IDX_ROPE_DIM
    tab[1] = np.where(lane < hq, -1.0, 0.0)
    tab[2] = np.where(in_q & (lane >= hq), 1.0, 0.0)
    tab[4] = np.where(li < hi, -1.0, 0.0)
    tab[5] = np.where(in_i & (li >= hi), 1.0, 0.0)
    tab = jnp.asarray(tab)
    tab = tab.at[0].set(jnp.where(in_q, f_q[lane % hq], 0.0))
    tab = tab.at[3].set(jnp.where(in_i, f_i[li % hi], 0.0))
    return tab


def _inproj(x2, pos2, g, b, w_pad, tq):
    T = x2.shape[0]
    row = lambda i: (i, 0)
    fixed = lambda i: (0, 0)
    out_shapes = (
        jax.ShapeDtypeStruct((T, POOL_WIDTH), F32),
        jax.ShapeDtypeStruct((T, ATTN_WIDTH), BF16),
        jax.ShapeDtypeStruct((T, KV_WIDTH), BF16),
        jax.ShapeDtypeStruct((T, KV_WIDTH), BF16),
        jax.ShapeDtypeStruct((T, IDX_HEADS * IDX_DIM), BF16),
        jax.ShapeDtypeStruct((T, IDX_DIM), BF16),
        jax.ShapeDtypeStruct((T, IDX_HEADS), F32),
    )
    return pl.pallas_call(
        _inproj_kernel,
        grid=(T // tq,),
        in_specs=[
            pl.BlockSpec((tq, D_MODEL), row),
            pl.BlockSpec((tq, 1), row),
            pl.BlockSpec((1, D_MODEL), fixed),
            pl.BlockSpec((1, D_MODEL), fixed),
            pl.BlockSpec((D_MODEL, IN_WIDTH_PAD), fixed),
            pl.BlockSpec((8, LANES), fixed),
        ],
        out_specs=tuple(pl.BlockSpec((tq, s.shape[1]), row) for s in out_shapes),
        out_shape=out_shapes,
        compiler_params=pltpu.CompilerParams(
            dimension_semantics=("parallel",), vmem_limit_bytes=VMEM_LIMIT),
    )(x2, pos2, g, b, w_pad, _rope_tables())


def _row_count(mask):
    return jnp.sum(jnp.where(mask, 1.0, 0.0), axis=-1, keepdims=True)


def _attn_kernel(q_ref, k_ref, v_ref, iq_ref, ik_ref, iw_ref, o_ref, key_ref, bias_ref,
                 *, tq, S, n_sel):
    i = pl.program_id(1)
    w_scale = (IDX_HEADS ** -0.5) * (IDX_DIM ** -0.5)
    iw = iw_ref[...] * w_scale
    ik = ik_ref[0:S, :]
    nt = (((1,), (1,)), ((), ()))
    sc = jnp.zeros((tq, S), F32)
    for hh in range(IDX_HEADS):
        s = lax.dot_general(iq_ref[:, hh * IDX_DIM:(hh + 1) * IDX_DIM], ik, nt,
                            preferred_element_type=F32)
        sc = sc + jnp.maximum(s, 0.0) * iw[:, hh:hh + 1]

    q_pos = i * tq + lax.broadcasted_iota(jnp.int32, (tq, S), 0)
    k_pos = lax.broadcasted_iota(jnp.int32, (tq, S), 1)
    causal = k_pos <= q_pos
    sc = jnp.where(sc == 0.0, 0.0, sc)
    sc = jnp.where(causal, sc, NEG_INF)
    bits = pltpu.bitcast(sc, jnp.int32)
    key_ref[...] = bits ^ ((bits >> 31) & jnp.int32(0x7FFFFFFF))

    def value_step(b, t_u):
        cand = t_u | lax.shift_left(jnp.int32(1), 31 - b)
        cnt = _row_count(key_ref[...] >= (cand ^ jnp.int32(INT_MIN)))
        return jnp.where(cnt >= n_sel, cand, t_u)

    t_u = lax.fori_loop(0, 32, value_step, jnp.zeros((tq, 1), jnp.int32))
    thr = t_u ^ jnp.int32(INT_MIN)
    key = key_ref[...]
    need = n_sel - _row_count(key > thr)
    key_ref[...] = jnp.where(key == thr, k_pos, jnp.int32(S))
    gt_bias = jnp.where((key > thr) & causal, 0.0, NEG_INF)
    bias_ref[...] = gt_bias

    def index_step(b, j0):
        cand = j0 | lax.shift_left(jnp.int32(1), (S.bit_length() - 2) - b)
        cnt = _row_count(key_ref[...] < cand)
        return jnp.where(cnt < need, cand, j0)

    j0 = lax.fori_loop(0, S.bit_length() - 1, index_step, jnp.zeros((tq, 1), jnp.int32))
    bias_ref[...] = jnp.where((key_ref[...] <= j0) & causal, 0.0, bias_ref[...])

    for hh in range(N_HEADS):
        g = hh // (N_HEADS // N_KV_HEADS)
        kg = k_ref[0:S, g * HEAD_DIM:(g + 1) * HEAD_DIM]
        vg = v_ref[0:S, g * HEAD_DIM:(g + 1) * HEAD_DIM]
        logits = lax.dot_general(q_ref[:, hh * HEAD_DIM:(hh + 1) * HEAD_DIM], kg, nt,
                                 preferred_element_type=F32) + bias_ref[...]
        m = jnp.max(logits, axis=-1, keepdims=True)
        p = jnp.exp(logits - m)
        l = jnp.sum(p, axis=-1, keepdims=True)
        o = jnp.dot(p.astype(BF16), vg, preferred_element_type=F32)
        o_ref[:, hh * HEAD_DIM:(hh + 1) * HEAD_DIM] = (o / l).astype(BF16)


def _attn(q, k, v, iq, ik, iw, B, L, tq):
    T = B * L
    nq = L // tq
    n_sel = min(TOPK_MAX, L // 4)
    qrow = lambda b, i: (b * nq + i, 0)
    seq = lambda b, i: (b, 0)
    return pl.pallas_call(
        functools.partial(_attn_kernel, tq=tq, S=L, n_sel=n_sel),
        grid=(B, nq),
        in_specs=[
            pl.BlockSpec((tq, ATTN_WIDTH), qrow),
            pl.BlockSpec((L, KV_WIDTH), seq),
            pl.BlockSpec((L, KV_WIDTH), seq),
            pl.BlockSpec((tq, IDX_HEADS * IDX_DIM), qrow),
            pl.BlockSpec((L, IDX_DIM), seq),
            pl.BlockSpec((tq, IDX_HEADS), qrow),
        ],
        out_specs=pl.BlockSpec((tq, ATTN_WIDTH), qrow),
        out_shape=jax.ShapeDtypeStruct((T, ATTN_WIDTH), BF16),
        scratch_shapes=[pltpu.VMEM((tq, L), jnp.int32), pltpu.VMEM((tq, L), F32)],
        compiler_params=pltpu.CompilerParams(
            dimension_semantics=("parallel", "arbitrary"), vmem_limit_bytes=VMEM_LIMIT),
    )(q, k, v, iq, ik, iw)


def _outproj_kernel(x_ref, u_ref, halo_ref, a_ref, gin_ref, bin_ref, wpool_ref, pscale_ref,
                    wout_ref, g1_ref, b1_ref, wr_hi_ref, wr_lo_ref, br_ref,
                    h1_ref, h1b_ref, ids_ref, gates_ref, *, tq, tiles_per_seq):
    i = pl.program_id(0)
    seq_tile = i % tiles_per_seq
    u = u_ref[...]
    halo = jnp.where(seq_tile == 0, 0.0, halo_ref[...])
    ext = jnp.concatenate([halo, u], axis=0)
    lane = lax.broadcasted_iota(jnp.int32, (tq, POOL_WIDTH), 1)
    grp = lane // POOL_GDIM
    win = jnp.zeros((tq, POOL_WIDTH), F32)
    s = ext
    for gi, w in enumerate(POOL_WINDOWS):
        s = s + pltpu.roll(s, w // 2, 0)
        win = jnp.where(grp == gi, s[POOL_HALO:, :], win)
    t_seq = seq_tile * tq + lax.broadcasted_iota(jnp.int32, (tq, POOL_WIDTH), 0)
    width = lax.shift_left(jnp.int32(2), grp)
    cnt = jnp.minimum(t_seq + 1, width).astype(F32)
    d = win / cnt - u
    y_pool = jnp.dot(d.astype(BF16), wpool_ref[...], preferred_element_type=F32) * pscale_ref[...]

    mix = jnp.dot(y_pool.astype(BF16), wout_ref[0:POOL_WIDTH, :], preferred_element_type=F32)
    mix = mix + jnp.dot(a_ref[...], wout_ref[POOL_WIDTH:, :], preferred_element_type=F32)
    h = _layer_norm(x_ref[...], gin_ref[...], bin_ref[...])
    h1 = _layer_norm(ALPHA_RES * h + mix, g1_ref[...], b1_ref[...])
    h1_ref[...] = h1
    h1_hi = h1.astype(BF16)
    h1b_ref[...] = h1_hi
    h1_lo = (h1 - h1_hi.astype(F32)).astype(BF16)
    wr_hi = wr_hi_ref[...]
    logits = (jnp.dot(h1_hi, wr_hi, preferred_element_type=F32)
              + jnp.dot(h1_lo, wr_hi, preferred_element_type=F32)
              + jnp.dot(h1_hi, wr_lo_ref[...], preferred_element_type=F32)) + br_ref[...]
    lane_e = lax.broadcasted_iota(jnp.int32, (tq, LANES), 1)
    lg = jnp.where(lane_e < N_EXPERTS, logits, -jnp.inf)
    ids = jnp.zeros((tq, LANES), jnp.int32)
    ex = jnp.zeros((tq, LANES), F32)
    top0 = None
    for kk in range(TOP_K):
        m = jnp.max(lg, axis=-1, keepdims=True)
        idx = jnp.min(jnp.where(lg == m, lane_e, LANES), axis=-1, keepdims=True)
        if top0 is None:
            top0 = m
        ids = jnp.where(lane_e == kk, idx, ids)
        ex = jnp.where(lane_e == kk, jnp.exp(m - top0), ex)
        lg = jnp.where(lane_e == idx, -jnp.inf, lg)
    ids_ref[...] = ids
    gates_ref[...] = ex / jnp.sum(ex, axis=-1, keepdims=True)


def _outproj(x2, u, attn, gin, bin_, wpool_bd, pscale, wout, g1, b1, wr_hi, wr_lo, br, L, tq):
    T = x2.shape[0]
    row = lambda i: (i, 0)
    fixed = lambda i: (0, 0)
    halo = lambda i: (jnp.maximum(i * (tq // POOL_HALO) - 1, 0), 0)
    out_shapes = (
        jax.ShapeDtypeStruct((T, D_MODEL), F32),
        jax.ShapeDtypeStruct((T, D_MODEL), BF16),
        jax.ShapeDtypeStruct((T, LANES), jnp.int32),
        jax.ShapeDtypeStruct((T, LANES), F32),
    )
    return pl.pallas_call(
        functools.partial(_outproj_kernel, tq=tq, tiles_per_seq=L // tq),
        grid=(T // tq,),
        in_specs=[
            pl.BlockSpec((tq, D_MODEL), row),
            pl.BlockSpec((tq, POOL_WIDTH), row),
            pl.BlockSpec((POOL_HALO, POOL_WIDTH), halo),
            pl.BlockSpec((tq, ATTN_WIDTH), row),
            pl.BlockSpec((1, D_MODEL), fixed),
            pl.BlockSpec((1, D_MODEL), fixed),
            pl.BlockSpec((POOL_WIDTH, POOL_WIDTH), fixed),
            pl.BlockSpec((1, POOL_WIDTH), fixed),
            pl.BlockSpec((D_MODEL, D_MODEL), fixed),
            pl.BlockSpec((1, D_MODEL), fixed),
            pl.BlockSpec((1, D_MODEL), fixed),
            pl.BlockSpec((D_MODEL, LANES), fixed),
            pl.BlockSpec((D_MODEL, LANES), fixed),
            pl.BlockSpec((1, LANES), fixed),
        ],
        out_specs=tuple(pl.BlockSpec((tq, s.shape[1]), row) for s in out_shapes),
        out_shape=out_shapes,
        compiler_params=pltpu.CompilerParams(
            dimension_semantics=("parallel",), vmem_limit_bytes=VMEM_LIMIT),
    )(x2, u, u, attn, gin, bin_, wpool_bd, pscale, wout, g1, b1, wr_hi, wr_lo, br)


def _ffn_kernel(tile_e_ref, n_tiles_ref, xs_ref, gs_ref, wgu_ref, bgu_ref, wd_ref, bd_ref, y_ref,
                *, n_chunk):
    j = pl.program_id(0)

    @pl.when(j < n_tiles_ref[0])
    def _():
        xs = xs_ref[...]
        cw = D_FF // n_chunk
        acc = jnp.zeros(y_ref.shape, F32)
        for c in range(n_chunk):
            gate = jnp.dot(xs, wgu_ref[0, :, c * cw:(c + 1) * cw], preferred_element_type=F32)
            gate = gate + bgu_ref[0, :, c * cw:(c + 1) * cw]
            up = jnp.dot(xs, wgu_ref[0, :, D_FF + c * cw:D_FF + (c + 1) * cw],
                         preferred_element_type=F32)
            up = up + bgu_ref[0, :, D_FF + c * cw:D_FF + (c + 1) * cw]
            gate = jnp.minimum(gate, SWIGLU_LIMIT)
            up = jnp.clip(up, -SWIGLU_LIMIT, SWIGLU_LIMIT)
            act = (up + 1.0) * gate * jax.nn.sigmoid(SWIGLU_ALPHA * gate)
            acc = acc + jnp.dot(act.astype(BF16), wd_ref[0, c * cw:(c + 1) * cw, :],
                                preferred_element_type=F32)
        y_ref[...] = (acc + bd_ref[0]) * gs_ref[...]

    @pl.when(j >= n_tiles_ref[0])
    def _():
        y_ref[...] = jnp.zeros(y_ref.shape, F32)


def _ffn(tile_e, n_tiles, xs, gs, wgu, bgu, wd, bd, tm):
    n_pad = xs.shape[0]
    row = lambda j, te, nt: (j, 0)
    exp3 = lambda j, te, nt: (te[j], 0, 0)
    grid_spec = pltpu.PrefetchScalarGridSpec(
        num_scalar_prefetch=2,
        grid=(n_pad // tm,),
        in_specs=[
            pl.BlockSpec((tm, D_MODEL), row),
            pl.BlockSpec((tm, 1), row),
            pl.BlockSpec((1, D_MODEL, 2 * D_FF), exp3),
            pl.BlockSpec((1, 1, 2 * D_FF), exp3),
            pl.BlockSpec((1, D_FF, D_MODEL), exp3),
            pl.BlockSpec((1, 1, D_MODEL), exp3),
        ],
        out_specs=pl.BlockSpec((tm, D_MODEL), row),
    )
    return pl.pallas_call(
        functools.partial(_ffn_kernel, n_chunk=2),
        grid_spec=grid_spec,
        out_shape=jax.ShapeDtypeStruct((n_pad, D_MODEL), F32),
        compiler_params=pltpu.CompilerParams(
            dimension_semantics=("arbitrary",), vmem_limit_bytes=VMEM_LIMIT),
    )(tile_e, n_tiles, xs, gs, wgu, bgu, wd, bd)


def _final_kernel(h1_ref, f_ref, g_ref, b_ref, o_ref):
    o_ref[...] = _layer_norm(ALPHA_RES * h1_ref[...] + f_ref[...], g_ref[...], b_ref[...])


def _final(h1, ffn, g, b, tq):
    T = h1.shape[0]
    row = lambda i: (i, 0)
    fixed = lambda i: (0, 0)
    return pl.pallas_call(
        _final_kernel,
        grid=(T // tq,),
        in_specs=[pl.BlockSpec((tq, D_MODEL), row), pl.BlockSpec((tq, D_MODEL), row),
                  pl.BlockSpec((1, D_MODEL), fixed), pl.BlockSpec((1, D_MODEL), fixed)],
        out_specs=pl.BlockSpec((tq, D_MODEL), row),
        out_shape=jax.ShapeDtypeStruct((T, D_MODEL), F32),
        compiler_params=pltpu.CompilerParams(
            dimension_semantics=("parallel",), vmem_limit_bytes=VMEM_LIMIT),
    )(h1, ffn, g, b)


def kernel(x, positions, ln_in_g, ln_in_b, w_in, w_pool, pool_scale, w_out, ln1_g, ln1_b,
           w_router, b_router, w_gate_up, b_gate_up, w_down, b_down, ln2_g, ln2_b):
    B, L, D = x.shape
    T = B * L
    tq = min(256, L)
    tm = 256
    x2 = x.reshape(T, D)
    pos2 = positions.reshape(T, 1)
    gin = ln_in_g.reshape(1, D)
    bin_ = ln_in_b.reshape(1, D)

    w_pad = jnp.pad(w_in[0], ((0, 0), (0, IN_WIDTH_PAD - IN_WIDTH))).astype(BF16)
    u, q, k, v, iq, ik, iw = _inproj(x2, pos2, gin, bin_, w_pad, tq)

    attn = _attn(q, k, v, iq, ik, iw, B, L, min(128, L))

    wpool_bd = jnp.zeros((POOL_WIDTH, POOL_WIDTH), F32)
    for gi in range(POOL_GROUPS):
        sl = slice(gi * POOL_GDIM, (gi + 1) * POOL_GDIM)
        wpool_bd = wpool_bd.at[sl, sl].set(w_pool[0, gi])
    wr = jnp.pad(w_router[0], ((0, 0), (0, LANES - N_EXPERTS)))
    wr_hi = wr.astype(BF16)
    wr_lo = (wr - wr_hi.astype(F32)).astype(BF16)
    br = jnp.pad(b_router[0], (0, LANES - N_EXPERTS)).reshape(1, LANES)
    h1, h1b, ids, gates = _outproj(
        x2, u, attn, gin, bin_, wpool_bd.astype(BF16), pool_scale[0].reshape(1, POOL_WIDTH),
        w_out[0].astype(BF16), ln1_g[0].reshape(1, D), ln1_b[0].reshape(1, D),
        wr_hi, wr_lo, br, L, tq)

    n_assign = T * TOP_K
    n_pad = n_assign + N_EXPERTS * tm
    flat_e = ids[:, :TOP_K].reshape(-1)
    order = jnp.argsort(flat_e, stable=True)
    e_sorted = flat_e[order]
    tok = order // TOP_K
    g_sorted = gates[:, :TOP_K].reshape(-1)[order]
    counts = jnp.bincount(flat_e, length=N_EXPERTS).astype(jnp.int32)
    padded = ((counts + tm - 1) // tm) * tm
    pend = jnp.cumsum(padded)
    pstart = pend - padded
    start = jnp.cumsum(counts) - counts
    dest = pstart[e_sorted] + (jnp.arange(n_assign, dtype=jnp.int32) - start[e_sorted])
    xs = jnp.zeros((n_pad, D), BF16).at[dest].set(h1b[tok])
    gs = jnp.zeros((n_pad, 1), F32).at[dest].set(g_sorted[:, None])
    tile_start = jnp.arange(n_pad // tm, dtype=jnp.int32) * tm
    tile_e = jnp.minimum(jnp.searchsorted(pend, tile_start, side="right"),
                         N_EXPERTS - 1).astype(jnp.int32)
    n_tiles = (pend[-1:] // tm).astype(jnp.int32)

    ys = _ffn(tile_e, n_tiles, xs, gs, w_gate_up[0].astype(BF16),
              b_gate_up[0].reshape(N_EXPERTS, 1, 2 * D_FF), w_down[0].astype(BF16),
              b_down[0].reshape(N_EXPERTS, 1, D), tm)
    ffn = jnp.zeros((T, D), F32).at[tok].add(ys[dest])

    out = _final(h1, ffn, ln2_g[0].reshape(1, D), ln2_b[0].reshape(1, D), tq)
    return out.reshape(B, L, D)
```

```python
import functools

import jax
import jax.numpy as jnp
import numpy as np
from jax import lax
from jax.experimental import pallas as pl
from jax.experimental.pallas import tpu as pltpu

F32 = jnp.float32
BF16 = jnp.bfloat16

D_MODEL = 1024
POOL_WIDTH = 256
POOL_GROUPS = 4
POOL_GDIM = 64
POOL_WINDOWS = (2, 4, 8, 16)
POOL_HALO = 16
HEAD_DIM = 128
ATTN_WIDTH = 768
N_HEADS = 6
N_KV_HEADS = 2
KV_WIDTH = 256
IDX_HEADS = 8
IDX_DIM = 64
TOPK_MAX = 256
ROPE_THETA = 500000.0
ROPE_DIM = 32
IDX_ROPE_DIM = 16
N_EXPERTS = 32
TOP_K = 4
D_FF = 1024
SWIGLU_LIMIT = 7.0
SWIGLU_ALPHA = 1.702
DEPTH = 1
ALPHA_RES = (2.0 * DEPTH) ** 0.25
LN_EPS = 1e-5
NEG_INF = -1e30
OFF_Q = 256
OFF_K = 1024
OFF_V = 1280
OFF_IQ = 1536
OFF_IK = 2048
IN_WIDTH = 2120
IN_WIDTH_PAD = 2176

LANES = 128
VMEM_LIMIT = 48 * 1024 * 1024
INT_MIN = -2 ** 31
NOT_TIED = 2 ** 30
CAUSAL_BANDS = 4


def _layer_norm(x, g, b):
    mu = jnp.mean(x, axis=-1, keepdims=True)
    xc = x - mu
    var = jnp.mean(xc * xc, axis=-1, keepdims=True)
    return xc * lax.rsqrt(var + LN_EPS) * g + b


def _rope(xh, cos, sin_lo, sin_hi, half):
    return (xh * cos + pltpu.roll(xh, LANES - half, 1) * sin_lo
            + pltpu.roll(xh, half, 1) * sin_hi)


def _inproj_kernel(x_ref, pos_ref, g_ref, b_ref, w_ref, tab_ref,
                   u_ref, q_ref, k_ref, v_ref, iq_ref, ik_ref, iw_ref):
    h = _layer_norm(x_ref[...], g_ref[...], b_ref[...])
    proj = jnp.dot(h.astype(BF16), w_ref[...], preferred_element_type=F32)
    pos = pos_ref[...].astype(F32)

    ang = pos * tab_ref[0:1, :]
    cos_q = jnp.cos(ang)
    sin_q = jnp.sin(ang)
    sin_q_lo = sin_q * tab_ref[1:2, :]
    sin_q_hi = sin_q * tab_ref[2:3, :]
    ang = pos * tab_ref[3:4, :]
    cos_i = jnp.cos(ang)
    sin_i = jnp.sin(ang)
    sin_i_lo = sin_i * tab_ref[4:5, :]
    sin_i_hi = sin_i * tab_ref[5:6, :]

    u_ref[...] = proj[:, 0:OFF_Q]
    scale = HEAD_DIM ** -0.5
    for hh in range(N_HEADS):
        c0 = OFF_Q + hh * HEAD_DIM
        r = _rope(proj[:, c0:c0 + HEAD_DIM], cos_q, sin_q_lo, sin_q_hi, ROPE_DIM // 2)
        q_ref[:, hh * HEAD_DIM:(hh + 1) * HEAD_DIM] = (r * scale).astype(BF16)
    for hh in range(N_KV_HEADS):
        c0 = OFF_K + hh * HEAD_DIM
        r = _rope(proj[:, c0:c0 + HEAD_DIM], cos_q, sin_q_lo, sin_q_hi, ROPE_DIM // 2)
        k_ref[:, hh * HEAD_DIM:(hh + 1) * HEAD_DIM] = r.astype(BF16)
    v_ref[...] = proj[:, OFF_V:OFF_IQ].astype(BF16)
    for t in range(IDX_HEADS * IDX_DIM // LANES):
        c0 = OFF_IQ + t * LANES
        r = _rope(proj[:, c0:c0 + LANES], cos_i, sin_i_lo, sin_i_hi, IDX_ROPE_DIM // 2)
        iq_ref[:, t * LANES:(t + 1) * LANES] = r.astype(BF16)
    tail = proj[:, OFF_IK:OFF_IK + LANES]
    r = _rope(tail, cos_i, sin_i_lo, sin_i_hi, IDX_ROPE_DIM // 2)
    ik_ref[...] = r[:, 0:IDX_DIM].astype(BF16)
    iw_ref[...] = tail[:, IDX_DIM:IDX_DIM + IDX_HEADS]


def _rope_tables():
    lane = np.arange(LANES)
    tab = np.zeros((8, LANES), np.float32)
    f_q = ROPE_THETA ** (-jnp.arange(0, ROPE_DIM, 2, dtype=F32) / ROPE_DIM)
    f_i = ROPE_THETA ** (-jnp.arange(0, IDX_ROPE_DIM, 2, dtype=F32) / IDX_ROPE_DIM)
    hq, hi = ROPE_DIM // 2, IDX_ROPE_DIM // 2
    in_q = lane < ROPE_DIM
    li = lane % IDX_DIM
    in_i = li < IDX_ROPE_DIM
    tab[1] = np.where(lane < hq, -1.0, 0.0)
    tab[2] = np.where(in_q & (lane >= hq), 1.0, 0.0)
    tab[4] = np.where(li < hi, -1.0, 0.0)
    tab[5] = np.where(in_i & (li >= hi), 1.0, 0.0)
    tab = jnp.asarray(tab)
    tab = tab.at[0].set(jnp.where(in_q, f_q[lane % hq], 0.0))
    tab = tab.at[3].set(jnp.where(in_i, f_i[li % hi], 0.0))
    return tab


def _inproj(x2, pos2, g, b, w_pad, tq):
    T = x2.shape[0]
    row = lambda i: (i, 0)
    fixed = lambda i: (0, 0)
    out_shapes = (
        jax.ShapeDtypeStruct((T, POOL_WIDTH), F32),
        jax.ShapeDtypeStruct((T, ATTN_WIDTH), BF16),
        jax.ShapeDtypeStruct((T, KV_WIDTH), BF16),
        jax.ShapeDtypeStruct((T, KV_WIDTH), BF16),
        jax.ShapeDtypeStruct((T, IDX_HEADS * IDX_DIM), BF16),
        jax.ShapeDtypeStruct((T, IDX_DIM), BF16),
        jax.ShapeDtypeStruct((T, IDX_HEADS), F32),
    )
    return pl.pallas_call(
        _inproj_kernel,
        grid=(T // tq,),
        in_specs=[
            pl.BlockSpec((tq, D_MODEL), row),
            pl.BlockSpec((tq, 1), row),
            pl.BlockSpec((1, D_MODEL), fixed),
            pl.BlockSpec((1, D_MODEL), fixed),
            pl.BlockSpec((D_MODEL, IN_WIDTH_PAD), fixed),
            pl.BlockSpec((8, LANES), fixed),
        ],
        out_specs=tuple(pl.BlockSpec((tq, s.shape[1]), row) for s in out_shapes),
        out_shape=out_shapes,
        compiler_params=pltpu.CompilerParams(
            dimension_semantics=("parallel",), vmem_limit_bytes=VMEM_LIMIT),
    )(x2, pos2, g, b, w_pad, _rope_tables())


def _row_count(mask):
    return jnp.sum(jnp.where(mask, 1.0, 0.0), axis=-1, keepdims=True)


def _attn_tile(q_ref, k_ref, v_ref, iq_ref, ik_ref, iw_ref, o_ref, key_ref, bias_ref,
               *, i, tq, S, n_sel):
    w_scale = (IDX_HEADS ** -0.5) * (IDX_DIM ** -0.5)
    iw = iw_ref[...] * w_scale
    ik = ik_ref[0:S, :]
    nt = (((1,), (1,)), ((), ()))
    sc = jnp.zeros((tq, S), F32)
    for hh in range(IDX_HEADS):
        s = lax.dot_general(iq_ref[:, hh * IDX_DIM:(hh + 1) * IDX_DIM], ik, nt,
                            preferred_element_type=F32)
        sc = sc + jnp.maximum(s, 0.0) * iw[:, hh:hh + 1]

    q_pos = i * tq + lax.broadcasted_iota(jnp.int32, (tq, S), 0)
    k_pos = lax.broadcasted_iota(jnp.int32, (tq, S), 1)
    causal = k_pos <= q_pos
    sc = jnp.where(sc == 0.0, 0.0, sc)
    sc = jnp.where(causal, sc, NEG_INF)
    bits = pltpu.bitcast(sc, jnp.int32)
    key_ref[:, 0:S] = bits ^ ((bits >> 31) & jnp.int32(0x7FFFFFFF))

    def value_step(b, t_u):
        cand = t_u | lax.shift_left(jnp.int32(1), 31 - b)
        cnt = _row_count(key_ref[:, 0:S] >= (cand ^ jnp.int32(INT_MIN)))
        return jnp.where(cnt >= n_sel, cand, t_u)

    t_u = lax.fori_loop(0, 32, value_step, jnp.zeros((tq, 1), jnp.int32))
    thr = t_u ^ jnp.int32(INT_MIN)
    key = key_ref[:, 0:S]
    need = n_sel - _row_count(key > thr)
    key_ref[:, 0:S] = jnp.where(key == thr, k_pos, jnp.int32(NOT_TIED))
    bias_ref[:, 0:S] = jnp.where((key > thr) & causal, 0.0, NEG_INF)
    idx_bits = (S - 1).bit_length()

    def index_step(b, j0):
        cand = j0 | lax.shift_left(jnp.int32(1), (idx_bits - 1) - b)
        cnt = _row_count(key_ref[:, 0:S] < cand)
        return jnp.where(cnt < need, cand, j0)

    j0 = lax.fori_loop(0, idx_bits, index_step, jnp.zeros((tq, 1), jnp.int32))
    bias_ref[:, 0:S] = jnp.where((key_ref[:, 0:S] <= j0) & causal, 0.0, bias_ref[:, 0:S])

    for hh in range(N_HEADS):
        g = hh // (N_HEADS // N_KV_HEADS)
        kg = k_ref[0:S, g * HEAD_DIM:(g + 1) * HEAD_DIM]
        vg = v_ref[0:S, g * HEAD_DIM:(g + 1) * HEAD_DIM]
        logits = lax.dot_general(q_ref[:, hh * HEAD_DIM:(hh + 1) * HEAD_DIM], kg, nt,
                                 preferred_element_type=F32) + bias_ref[:, 0:S]
        m = jnp.max(logits, axis=-1, keepdims=True)
        p = jnp.exp(logits - m)
        l = jnp.sum(p, axis=-1, keepdims=True)
        o = jnp.dot(p.astype(BF16), vg, preferred_element_type=F32)
        o_ref[:, hh * HEAD_DIM:(hh + 1) * HEAD_DIM] = (o / l).astype(BF16)


def _attn_kernel(*refs, tq, L, n_sel):
    i = pl.program_id(1)
    band = L // CAUSAL_BANDS
    for v in range(CAUSAL_BANDS):
        @pl.when((i * tq) // band == v)
        def _(v=v):
            _attn_tile(*refs, i=i, tq=tq, S=(v + 1) * band, n_sel=n_sel)


def _attn(q, k, v, iq, ik, iw, B, L, tq):
    T = B * L
    nq = L // tq
    n_sel = min(TOPK_MAX, L // 4)
    assert L % (CAUSAL_BANDS * tq) == 0 and L // CAUSAL_BANDS >= n_sel
    qrow = lambda b, i: (b * nq + i, 0)
    seq = lambda b, i: (b, 0)
    return pl.pallas_call(
        functools.partial(_attn_kernel, tq=tq, L=L, n_sel=n_sel),
        grid=(B, nq),
        in_specs=[
            pl.BlockSpec((tq, ATTN_WIDTH), qrow),
            pl.BlockSpec((L, KV_WIDTH), seq),
            pl.BlockSpec((L, KV_WIDTH), seq),
            pl.BlockSpec((tq, IDX_HEADS * IDX_DIM), qrow),
            pl.BlockSpec((L, IDX_DIM), seq),
            pl.BlockSpec((tq, IDX_HEADS), qrow),
        ],
        out_specs=pl.BlockSpec((tq, ATTN_WIDTH), qrow),
        out_shape=jax.ShapeDtypeStruct((T, ATTN_WIDTH), BF16),
        scratch_shapes=[pltpu.VMEM((tq, L), jnp.int32), pltpu.VMEM((tq, L), F32)],
        compiler_params=pltpu.CompilerParams(
            dimension_semantics=("parallel", "arbitrary"), vmem_limit_bytes=VMEM_LIMIT),
    )(q, k, v, iq, ik, iw)


def _outproj_kernel(x_ref, u_ref, halo_ref, a_ref, gin_ref, bin_ref, wpool_ref, pscale_ref,
                    wout_ref, g1_ref, b1_ref, wr_hi_ref, wr_lo_ref, br_ref,
                    h1_ref, h1b_ref, ids_ref, gates_ref, *, tq, tiles_per_seq):
    i = pl.program_id(0)
    seq_tile = i % tiles_per_seq
    u = u_ref[...]
    halo = jnp.where(seq_tile == 0, 0.0, halo_ref[...])
    ext = jnp.concatenate([halo, u], axis=0)
    lane = lax.broadcasted_iota(jnp.int32, (tq, POOL_WIDTH), 1)
    grp = lane // POOL_GDIM
    win = jnp.zeros((tq, POOL_WIDTH), F32)
    s = ext
    for gi, w in enumerate(POOL_WINDOWS):
        s = s + pltpu.roll(s, w // 2, 0)
        win = jnp.where(grp == gi, s[POOL_HALO:, :], win)
    t_seq = seq_tile * tq + lax.broadcasted_iota(jnp.int32, (tq, POOL_WIDTH), 0)
    width = lax.shift_left(jnp.int32(2), grp)
    cnt = jnp.minimum(t_seq + 1, width).astype(F32)
    d = win / cnt - u
    y_pool = jnp.dot(d.astype(BF16), wpool_ref[...], preferred_element_type=F32) * pscale_ref[...]

    mix = jnp.dot(y_pool.astype(BF16), wout_ref[0:POOL_WIDTH, :], preferred_element_type=F32)
    mix = mix + jnp.dot(a_ref[...], wout_ref[POOL_WIDTH:, :], preferred_element_type=F32)
    h = _layer_norm(x_ref[...], gin_ref[...], bin_ref[...])
    h1 = _layer_norm(ALPHA_RES * h + mix, g1_ref[...], b1_ref[...])
    h1_ref[...] = h1
    h1_hi = h1.astype(BF16)
    h1b_ref[...] = h1_hi
    h1_lo = (h1 - h1_hi.astype(F32)).astype(BF16)
    wr_hi = wr_hi_ref[...]
    logits = (jnp.dot(h1_hi, wr_hi, preferred_element_type=F32)
              + jnp.dot(h1_lo, wr_hi, preferred_element_type=F32)
              + jnp.dot(h1_hi, wr_lo_ref[...], preferred_element_type=F32)) + br_ref[...]
    lane_e = lax.broadcasted_iota(jnp.int32, (tq, LANES), 1)
    lg = jnp.where(lane_e < N_EXPERTS, logits, -jnp.inf)
    ids = jnp.zeros((tq, LANES), jnp.int32)
    ex = jnp.zeros((tq, LANES), F32)
    top0 = None
    for kk in range(TOP_K):
        m = jnp.max(lg, axis=-1, keepdims=True)
        idx = jnp.min(jnp.where(lg == m, lane_e, LANES), axis=-1, keepdims=True)
        if top0 is None:
            top0 = m
        ids = jnp.where(lane_e == kk, idx, ids)
        ex = jnp.where(lane_e == kk, jnp.exp(m - top0), ex)
        lg = jnp.where(lane_e == idx, -jnp.inf, lg)
    ids_ref[...] = ids
    gates_ref[...] = ex / jnp.sum(ex, axis=-1, keepdims=True)


def _outproj(x2, u, attn, gin, bin_, wpool_bd, pscale, wout, g1, b1, wr_hi, wr_lo, br, L, tq):
    T = x2.shape[0]
    row = lambda i: (i, 0)
    fixed = lambda i: (0, 0)
    halo = lambda i: (jnp.maximum(i * (tq // POOL_HALO) - 1, 0), 0)
    out_shapes = (
        jax.ShapeDtypeStruct((T, D_MODEL), F32),
        jax.ShapeDtypeStruct((T, D_MODEL), BF16),
        jax.ShapeDtypeStruct((T, LANES), jnp.int32),
        jax.ShapeDtypeStruct((T, LANES), F32),
    )
    return pl.pallas_call(
        functools.partial(_outproj_kernel, tq=tq, tiles_per_seq=L // tq),
        grid=(T // tq,),
        in_specs=[
            pl.BlockSpec((tq, D_MODEL), row),
            pl.BlockSpec((tq, POOL_WIDTH), row),
            pl.BlockSpec((POOL_HALO, POOL_WIDTH), halo),
            pl.BlockSpec((tq, ATTN_WIDTH), row),
            pl.BlockSpec((1, D_MODEL), fixed),
            pl.BlockSpec((1, D_MODEL), fixed),
            pl.BlockSpec((POOL_WIDTH, POOL_WIDTH), fixed),
            pl.BlockSpec((1, POOL_WIDTH), fixed),
            pl.BlockSpec((D_MODEL, D_MODEL), fixed),
            pl.BlockSpec((1, D_MODEL), fixed),
            pl.BlockSpec((1, D_MODEL), fixed),
            pl.BlockSpec((D_MODEL, LANES), fixed),
            pl.BlockSpec((D_MODEL, LANES), fixed),
            pl.BlockSpec((1, LANES), fixed),
        ],
        out_specs=tuple(pl.BlockSpec((tq, s.shape[1]), row) for s in out_shapes),
        out_shape=out_shapes,
        compiler_params=pltpu.CompilerParams(
            dimension_semantics=("parallel",), vmem_limit_bytes=VMEM_LIMIT),
    )(x2, u, u, attn, gin, bin_, wpool_bd, pscale, wout, g1, b1, wr_hi, wr_lo, br)


ROUTE_TB = 512
ROW_ALIGN = 16
RUN_BITS = 6
STAGE_ROWS = 2560
FFN_TM = 256


def _route_kernel(ids_ref, slot_ref, slot_t_ref, meta_ref, carry_ref):
    c = pl.program_id(0)

    @pl.when(c == 0)
    def _():
        carry_ref[...] = jnp.zeros(carry_ref.shape, F32)

    tb = ids_ref.shape[0]
    ids = ids_ref[...]
    lane = lax.broadcasted_iota(jnp.int32, (tb, LANES), 1)
    onehot = [jnp.where(lane == ids[:, k:k + 1], 1.0, 0.0) for k in range(TOP_K)]
    member = onehot[0] + onehot[1] + onehot[2] + onehot[3]
    r = lax.broadcasted_iota(jnp.int32, (tb, tb), 0)
    cc = lax.broadcasted_iota(jnp.int32, (tb, tb), 1)
    before = jnp.where(cc < r, 1.0, 0.0).astype(BF16)
    lrank = jnp.dot(before, member.astype(BF16), preferred_element_type=F32)
    n = jnp.sum(member, axis=0, keepdims=True)
    units = jnp.ceil(n * (1.0 / ROW_ALIGN))
    er = lax.broadcasted_iota(jnp.int32, (LANES, LANES), 0)
    ec = lax.broadcasted_iota(jnp.int32, (LANES, LANES), 1)
    lower = jnp.where(er < ec, 1.0, 0.0).astype(BF16)
    off_units = jnp.dot(jnp.broadcast_to(units, (8, LANES)).astype(BF16), lower,
                        preferred_element_type=F32)[0:1, :]
    base = off_units * ROW_ALIGN + lrank
    slot = jnp.zeros((tb, LANES), jnp.int32)
    for k in range(TOP_K):
        sk = jnp.sum(onehot[k] * base, axis=-1, keepdims=True).astype(jnp.int32)
        slot = jnp.where(lane == k, sk, slot)
    slot_ref[...] = slot
    slot_t_ref[...] = slot.T[0:8, :]
    row = lax.broadcasted_iota(jnp.int32, (8, LANES), 0)
    meta = jnp.where(row == 0, units, jnp.where(row == 1, off_units, carry_ref[...]))
    meta_ref[...] = meta.astype(jnp.int32)
    carry_ref[...] = carry_ref[...] + units


def _route(ids):
    T = ids.shape[0]
    nblk = T // ROUTE_TB
    return pl.pallas_call(
        _route_kernel,
        grid=(nblk,),
        in_specs=[pl.BlockSpec((ROUTE_TB, LANES), lambda c: (c, 0))],
        out_specs=(pl.BlockSpec((ROUTE_TB, LANES), lambda c: (c, 0)),
                   pl.BlockSpec((8, ROUTE_TB), lambda c: (0, c)),
                   pl.BlockSpec((8, LANES), lambda c: (c, 0))),
        out_shape=(jax.ShapeDtypeStruct((T, LANES), jnp.int32),
                   jax.ShapeDtypeStruct((8, T), jnp.int32),
                   jax.ShapeDtypeStruct((nblk * 8, LANES), jnp.int32)),
        scratch_shapes=[pltpu.VMEM((1, LANES), F32)],
        compiler_params=pltpu.CompilerParams(
            dimension_semantics=("arbitrary",), vmem_limit_bytes=VMEM_LIMIT),
    )(ids)


def _run_pieces(units_ref, off_ref, dst_ref, c, visit):
    for e in range(N_EXPERTS):
        idx = c * N_EXPERTS + e
        m = units_ref[idx]
        so = off_ref[idx]
        do = dst_ref[idx]
        for b in range(RUN_BITS):
            done = m & ((1 << b) - 1)

            @pl.when(((m >> b) & 1) == 1)
            def _(e=e, b=b, done=done, so=so, do=do):
                visit(e, b, pl.multiple_of((so + done) * ROW_ALIGN, ROW_ALIGN),
                      pl.multiple_of((do + done) * ROW_ALIGN, ROW_ALIGN), ROW_ALIGN << b)


def _dispatch_kernel(units_ref, off_ref, dst_ref, slot_t_ref, h_ref, xs_init_ref, xs_ref,
                     stage_ref, sems):
    del xs_init_ref
    c = pl.program_id(0)
    tb = h_ref.shape[0]
    srow = lax.broadcasted_iota(jnp.int32, (STAGE_ROWS, tb), 0)
    hit = srow == slot_t_ref[0:1, :]
    for k in range(1, TOP_K):
        hit = hit | (srow == slot_t_ref[k:k + 1, :])
    perm = jnp.where(hit, 1.0, 0.0).astype(BF16)
    stage_ref[...] = jnp.dot(perm, h_ref[...], preferred_element_type=F32).astype(BF16)

    def piece(e, b, s_row, g_row, rows):
        return pltpu.make_async_copy(stage_ref.at[pl.ds(s_row, rows), :],
                                     xs_ref.at[pl.ds(g_row, rows), :], sems.at[e, b])

    _run_pieces(units_ref, off_ref, dst_ref, c, lambda *a: piece(*a).start())
    _run_pieces(units_ref, off_ref, dst_ref, c, lambda *a: piece(*a).wait())


def _dispatch(units, off, dst, slot_t, h1b, n_rows):
    T = h1b.shape[0]
    nblk = T // ROUTE_TB
    grid_spec = pltpu.PrefetchScalarGridSpec(
        num_scalar_prefetch=3,
        grid=(nblk,),
        in_specs=[pl.BlockSpec((8, ROUTE_TB), lambda c, *_: (0, c)),
                  pl.BlockSpec((ROUTE_TB, D_MODEL), lambda c, *_: (c, 0)),
                  pl.BlockSpec(memory_space=pl.ANY)],
        out_specs=pl.BlockSpec(memory_space=pl.ANY),
        scratch_shapes=[pltpu.VMEM((STAGE_ROWS, D_MODEL), BF16),
                        pltpu.SemaphoreType.DMA((N_EXPERTS, RUN_BITS))],
    )
    return pl.pallas_call(
        _dispatch_kernel,
        grid_spec=grid_spec,
        out_shape=jax.ShapeDtypeStruct((n_rows, D_MODEL), BF16),
        input_output_aliases={5: 0},
        compiler_params=pltpu.CompilerParams(
            dimension_semantics=("arbitrary",), vmem_limit_bytes=VMEM_LIMIT),
    )(units, off, dst, slot_t, h1b, jnp.zeros((n_rows, D_MODEL), BF16))


def _ffn_kernel(tile_e_ref, n_tiles_ref, xs_ref, wgu_ref, bgu_ref, wd_ref, bd_ref, y_ref,
                *, n_chunk):
    j = pl.program_id(0)

    @pl.when(j < n_tiles_ref[0])
    def _():
        xs = xs_ref[...]
        cw = D_FF // n_chunk
        acc = jnp.zeros(y_ref.shape, F32)
        for c in range(n_chunk):
            gate = jnp.dot(xs, wgu_ref[0, :, c * cw:(c + 1) * cw], preferred_element_type=F32)
            gate = gate + bgu_ref[0, :, c * cw:(c + 1) * cw]
            up = jnp.dot(xs, wgu_ref[0, :, D_FF + c * cw:D_FF + (c + 1) * cw],
                         preferred_element_type=F32)
            up = up + bgu_ref[0, :, D_FF + c * cw:D_FF + (c + 1) * cw]
            gate = jnp.minimum(gate, SWIGLU_LIMIT)
            up = jnp.clip(up, -SWIGLU_LIMIT, SWIGLU_LIMIT)
            act = (up + 1.0) * gate * jax.nn.sigmoid(SWIGLU_ALPHA * gate)
            acc = acc + jnp.dot(act.astype(BF16), wd_ref[0, c * cw:(c + 1) * cw, :],
                                preferred_element_type=F32)
        y_ref[...] = (acc + bd_ref[0]).astype(BF16)

    @pl.when(j >= n_tiles_ref[0])
    def _():
        y_ref[...] = jnp.zeros(y_ref.shape, BF16)


def _ffn(tile_e, n_tiles, xs, wgu, bgu, wd, bd):
    n_rows = xs.shape[0]
    row = lambda j, te, nt: (j, 0)
    exp3 = lambda j, te, nt: (te[j], 0, 0)
    grid_spec = pltpu.PrefetchScalarGridSpec(
        num_scalar_prefetch=2,
        grid=(n_rows // FFN_TM,),
        in_specs=[
            pl.BlockSpec((FFN_TM, D_MODEL), row),
            pl.BlockSpec((1, D_MODEL, 2 * D_FF), exp3),
            pl.BlockSpec((1, 1, 2 * D_FF), exp3),
            pl.BlockSpec((1, D_FF, D_MODEL), exp3),
            pl.BlockSpec((1, 1, D_MODEL), exp3),
        ],
        out_specs=pl.BlockSpec((FFN_TM, D_MODEL), row),
    )
    return pl.pallas_call(
        functools.partial(_ffn_kernel, n_chunk=2),
        grid_spec=grid_spec,
        out_shape=jax.ShapeDtypeStruct((n_rows, D_MODEL), BF16),
        compiler_params=pltpu.CompilerParams(
            dimension_semantics=("arbitrary",), vmem_limit_bytes=VMEM_LIMIT),
    )(tile_e, n_tiles, xs, wgu, bgu, wd, bd)


def _combine_kernel(units_ref, off_ref, dst_ref, slot_ref, gates_ref, h1_ref, ys_ref,
                    g_ref, b_ref, o_ref, stage_ref, sems):
    c = pl.program_id(0)
    tb = h1_ref.shape[0]

    @pl.when(c == 0)
    def _():
        stage_ref[...] = jnp.zeros(stage_ref.shape, BF16)

    def piece(e, b, s_row, g_row, rows):
        return pltpu.make_async_copy(ys_ref.at[pl.ds(g_row, rows), :],
                                     stage_ref.at[pl.ds(s_row, rows), :], sems.at[e, b])

    _run_pieces(units_ref, off_ref, dst_ref, c, lambda *a: piece(*a).start())

    slot = slot_ref[...]
    gates = gates_ref[...]
    scol = lax.broadcasted_iota(jnp.int32, (tb, STAGE_ROWS), 1)
    w = jnp.zeros((tb, STAGE_ROWS), F32)
    for k in range(TOP_K):
        w = w + jnp.where(scol == slot[:, k:k + 1], gates[:, k:k + 1], 0.0)
    w_hi = w.astype(BF16)
    w_lo = (w - w_hi.astype(F32)).astype(BF16)

    _run_pieces(units_ref, off_ref, dst_ref, c, lambda *a: piece(*a).wait())

    y = stage_ref[...]
    ffn = (jnp.dot(w_hi, y, preferred_element_type=F32)
           + jnp.dot(w_lo, y, preferred_element_type=F32))
    o_ref[...] = _layer_norm(ALPHA_RES * h1_ref[...] + ffn, g_ref[...], b_ref[...])


def _combine(units, off, dst, slot, gates, h1, ys, g, b):
    T = h1.shape[0]
    nblk = T // ROUTE_TB
    blk = lambda c, *_: (c, 0)
    fixed = lambda c, *_: (0, 0)
    grid_spec = pltpu.PrefetchScalarGridSpec(
        num_scalar_prefetch=3,
        grid=(nblk,),
        in_specs=[pl.BlockSpec((ROUTE_TB, LANES), blk),
                  pl.BlockSpec((ROUTE_TB, LANES), blk),
                  pl.BlockSpec((ROUTE_TB, D_MODEL), blk),
                  pl.BlockSpec(memory_space=pl.ANY),
                  pl.BlockSpec((1, D_MODEL), fixed),
                  pl.BlockSpec((1, D_MODEL), fixed)],
        out_specs=pl.BlockSpec((ROUTE_TB, D_MODEL), blk),
        scratch_shapes=[pltpu.VMEM((STAGE_ROWS, D_MODEL), BF16),
                        pltpu.SemaphoreType.DMA((N_EXPERTS, RUN_BITS))],
    )
    return pl.pallas_call(
        _combine_kernel,
        grid_spec=grid_spec,
        out_shape=jax.ShapeDtypeStruct((T, D_MODEL), F32),
        compiler_params=pltpu.CompilerParams(
            dimension_semantics=("arbitrary",), vmem_limit_bytes=VMEM_LIMIT),
    )(units, off, dst, slot, gates, h1, ys, g, b)


def kernel(x, positions, ln_in_g, ln_in_b, w_in, w_pool, pool_scale, w_out, ln1_g, ln1_b,
           w_router, b_router, w_gate_up, b_gate_up, w_down, b_down, ln2_g, ln2_b):
    B, L, D = x.shape
    T = B * L
    assert T % ROUTE_TB == 0 and D == D_MODEL
    tq = min(256, L)
    x2 = x.reshape(T, D)
    pos2 = positions.reshape(T, 1)
    gin = ln_in_g.reshape(1, D)
    bin_ = ln_in_b.reshape(1, D)

    w_pad = jnp.pad(w_in[0], ((0, 0), (0, IN_WIDTH_PAD - IN_WIDTH))).astype(BF16)
    u, q, k, v, iq, ik, iw = _inproj(x2, pos2, gin, bin_, w_pad, tq)

    attn = _attn(q, k, v, iq, ik, iw, B, L, min(128, L))

    wpool_bd = jnp.zeros((POOL_WIDTH, POOL_WIDTH), F32)
    for gi in range(POOL_GROUPS):
        sl = slice(gi * POOL_GDIM, (gi + 1) * POOL_GDIM)
        wpool_bd = wpool_bd.at[sl, sl].set(w_pool[0, gi])
    wr = jnp.pad(w_router[0], ((0, 0), (0, LANES - N_EXPERTS)))
    wr_hi = wr.astype(BF16)
    wr_lo = (wr - wr_hi.astype(F32)).astype(BF16)
    br = jnp.pad(b_router[0], (0, LANES - N_EXPERTS)).reshape(1, LANES)
    h1, h1b, ids, gates = _outproj(
        x2, u, attn, gin, bin_, wpool_bd.astype(BF16), pool_scale[0].reshape(1, POOL_WIDTH),
        w_out[0].astype(BF16), ln1_g[0].reshape(1, D), ln1_b[0].reshape(1, D),
        wr_hi, wr_lo, br, L, tq)

    nblk = T // ROUTE_TB
    slot, slot_t, meta = _route(ids)
    meta = meta.reshape(nblk, 8, LANES)[:, :, :N_EXPERTS]
    units, off_units, base_units = meta[:, 0], meta[:, 1], meta[:, 2]
    tile_units = FFN_TM // ROW_ALIGN
    total_units = base_units[-1] + units[-1]
    region_units = ((total_units + tile_units - 1) // tile_units) * tile_units
    region_end = jnp.cumsum(region_units)
    dst_units = (region_end - region_units)[None, :] + base_units
    max_rows = T * TOP_K + nblk * N_EXPERTS * (ROW_ALIGN - 1) + N_EXPERTS * (FFN_TM - 1)
    n_rows = -(-max_rows // FFN_TM) * FFN_TM
    tile_start = jnp.arange(n_rows // FFN_TM, dtype=jnp.int32) * tile_units
    tile_e = jnp.minimum(jnp.searchsorted(region_end, tile_start, side="right"),
                         N_EXPERTS - 1).astype(jnp.int32)
    n_tiles = (region_end[-1:] // tile_units).astype(jnp.int32)
    units_f = units.reshape(-1)
    off_f = off_units.reshape(-1)
    dst_f = dst_units.reshape(-1).astype(jnp.int32)

    xs = _dispatch(units_f, off_f, dst_f, slot_t, h1b, n_rows)
    ys = _ffn(tile_e, n_tiles, xs, w_gate_up[0].astype(BF16),
              b_gate_up[0].reshape(N_EXPERTS, 1, 2 * D_FF), w_down[0].astype(BF16),
              b_down[0].reshape(N_EXPERTS, 1, D))
    out = _combine(units_f, off_f, dst_f, slot, gates, h1, ys,
                   ln2_g[0].reshape(1, D), ln2_b[0].reshape(1, D))
    return out.reshape(B, L, D)
```

```python
import functools

import jax
import jax.numpy as jnp
import numpy as np
from jax import lax
from jax.experimental import pallas as pl
from jax.experimental.pallas import tpu as pltpu

F32 = jnp.float32
BF16 = jnp.bfloat16

D_MODEL = 1024
POOL_WIDTH = 256
POOL_GROUPS = 4
POOL_GDIM = 64
POOL_WINDOWS = (2, 4, 8, 16)
POOL_HALO = 16
HEAD_DIM = 128
ATTN_WIDTH = 768
N_HEADS = 6
N_KV_HEADS = 2
KV_WIDTH = 256
IDX_HEADS = 8
IDX_DIM = 64
TOPK_MAX = 256
ROPE_THETA = 500000.0
ROPE_DIM = 32
IDX_ROPE_DIM = 16
N_EXPERTS = 32
TOP_K = 4
D_FF = 1024
SWIGLU_LIMIT = 7.0
SWIGLU_ALPHA = 1.702
DEPTH = 1
ALPHA_RES = (2.0 * DEPTH) ** 0.25
LN_EPS = 1e-5
NEG_INF = -1e30
OFF_Q = 256
OFF_K = 1024
OFF_V = 1280
OFF_IQ = 1536
OFF_IK = 2048
IN_WIDTH = 2120
IN_WIDTH_PAD = 2176

LANES = 128
VMEM_LIMIT = 48 * 1024 * 1024
FFN_VMEM_LIMIT = 56 * 1024 * 1024
INT_MIN = -2 ** 31
NOT_TIED = 2 ** 30
CAUSAL_BANDS = 4


def _layer_norm(x, g, b):
    mu = jnp.mean(x, axis=-1, keepdims=True)
    xc = x - mu
    var = jnp.mean(xc * xc, axis=-1, keepdims=True)
    return xc * lax.rsqrt(var + LN_EPS) * g + b


def _rope(xh, cos, sin_lo, sin_hi, half):
    return (xh * cos + pltpu.roll(xh, LANES - half, 1) * sin_lo
            + pltpu.roll(xh, half, 1) * sin_hi)


def _inproj_kernel(x_ref, pos_ref, g_ref, b_ref, w_ref, tab_ref,
                   u_ref, q_ref, k_ref, v_ref, iq_ref, ik_ref, iw_ref):
    h = _layer_norm(x_ref[...], g_ref[...], b_ref[...])
    proj = jnp.dot(h.astype(BF16), w_ref[...], preferred_element_type=F32)
    pos = pos_ref[...].astype(F32)

    ang = pos * tab_ref[0:1, :]
    cos_q = jnp.cos(ang)
    sin_q = jnp.sin(ang)
    sin_q_lo = sin_q * tab_ref[1:2, :]
    sin_q_hi = sin_q * tab_ref[2:3, :]
    ang = pos * tab_ref[3:4, :]
    cos_i = jnp.cos(ang)
    sin_i = jnp.sin(ang)
    sin_i_lo = sin_i * tab_ref[4:5, :]
    sin_i_hi = sin_i * tab_ref[5:6, :]

    u_ref[...] = proj[:, 0:OFF_Q]
    scale = HEAD_DIM ** -0.5
    for hh in range(N_HEADS):
        c0 = OFF_Q + hh * HEAD_DIM
        r = _rope(proj[:, c0:c0 + HEAD_DIM], cos_q, sin_q_lo, sin_q_hi, ROPE_DIM // 2)
        q_ref[:, hh * HEAD_DIM:(hh + 1) * HEAD_DIM] = (r * scale).astype(BF16)
    for hh in range(N_KV_HEADS):
        c0 = OFF_K + hh * HEAD_DIM
        r = _rope(proj[:, c0:c0 + HEAD_DIM], cos_q, sin_q_lo, sin_q_hi, ROPE_DIM // 2)
        k_ref[:, hh * HEAD_DIM:(hh + 1) * HEAD_DIM] = r.astype(BF16)
    v_ref[...] = proj[:, OFF_V:OFF_IQ].astype(BF16)
    for t in range(IDX_HEADS * IDX_DIM // LANES):
        c0 = OFF_IQ + t * LANES
        r = _rope(proj[:, c0:c0 + LANES], cos_i, sin_i_lo, sin_i_hi, IDX_ROPE_DIM // 2)
        iq_ref[:, t * LANES:(t + 1) * LANES] = r.astype(BF16)
    tail = proj[:, OFF_IK:OFF_IK + LANES]
    r = _rope(tail, cos_i, sin_i_lo, sin_i_hi, IDX_ROPE_DIM // 2)
    ik_ref[...] = r[:, 0:IDX_DIM].astype(BF16)
    iw_ref[...] = tail[:, IDX_DIM:IDX_DIM + IDX_HEADS]


def _rope_tables():
    lane = np.arange(LANES)
    tab = np.zeros((8, LANES), np.float32)
    f_q = ROPE_THETA ** (-jnp.arange(0, ROPE_DIM, 2, dtype=F32) / ROPE_DIM)
    f_i = ROPE_THETA ** (-jnp.arange(0, IDX_ROPE_DIM, 2, dtype=F32) / IDX_ROPE_DIM)
    hq, hi = ROPE_DIM // 2, IDX_ROPE_DIM // 2
    in_q = lane < ROPE_DIM
    li = lane % IDX_DIM
    in_i = li < IDX_ROPE_DIM
    tab[1] = np.where(lane < hq, -1.0, 0.0)
    tab[2] = np.where(in_q & (lane >= hq), 1.0, 0.0)
    tab[4] = np.where(li < hi, -1.0, 0.0)
    tab[5] = np.where(in_i & (li >= hi), 1.0, 0.0)
    tab = jnp.asarray(tab)
    tab = tab.at[0].set(jnp.where(in_q, f_q[lane % hq], 0.0))
    tab = tab.at[3].set(jnp.where(in_i, f_i[li % hi], 0.0))
    return tab


def _inproj(x2, pos2, g, b, w_pad, tq):
    T = x2.shape[0]
    row = lambda i: (i, 0)
    fixed = lambda i: (0, 0)
    out_shapes = (
        jax.ShapeDtypeStruct((T, POOL_WIDTH), F32),
        jax.ShapeDtypeStruct((T, ATTN_WIDTH), BF16),
        jax.ShapeDtypeStruct((T, KV_WIDTH), BF16),
        jax.ShapeDtypeStruct((T, KV_WIDTH), BF16),
        jax.ShapeDtypeStruct((T, IDX_HEADS * IDX_DIM), BF16),
        jax.ShapeDtypeStruct((T, IDX_DIM), BF16),
        jax.ShapeDtypeStruct((T, IDX_HEADS), F32),
    )
    return pl.pallas_call(
        _inproj_kernel,
        grid=(T // tq,),
        in_specs=[
            pl.BlockSpec((tq, D_MODEL), row),
            pl.BlockSpec((tq, 1), row),
            pl.BlockSpec((1, D_MODEL), fixed),
            pl.BlockSpec((1, D_MODEL), fixed),
            pl.BlockSpec((D_MODEL, IN_WIDTH_PAD), fixed),
            pl.BlockSpec((8, LANES), fixed),
        ],
        out_specs=tuple(pl.BlockSpec((tq, s.shape[1]), row) for s in out_shapes),
        out_shape=out_shapes,
        compiler_params=pltpu.CompilerParams(
            dimension_semantics=("parallel",), vmem_limit_bytes=VMEM_LIMIT),
    )(x2, pos2, g, b, w_pad, _rope_tables())


def _row_count(mask):
    return jnp.sum(jnp.where(mask, 1.0, 0.0), axis=-1, keepdims=True)


def _attn_tile(q_ref, k_ref, v_ref, iq_ref, ik_ref, iw_ref, o_ref, key_ref, bias_ref,
               *, i, tq, S, n_sel):
    w_scale = (IDX_HEADS ** -0.5) * (IDX_DIM ** -0.5)
    iw = iw_ref[...] * w_scale
    ik = ik_ref[0:S, :]
    nt = (((1,), (1,)), ((), ()))
    sc = jnp.zeros((tq, S), F32)
    for hh in range(IDX_HEADS):
        s = lax.dot_general(iq_ref[:, hh * IDX_DIM:(hh + 1) * IDX_DIM], ik, nt,
                            preferred_element_type=F32)
        sc = sc + jnp.maximum(s, 0.0) * iw[:, hh:hh + 1]

    q_pos = i * tq + lax.broadcasted_iota(jnp.int32, (tq, S), 0)
    k_pos = lax.broadcasted_iota(jnp.int32, (tq, S), 1)
    causal = k_pos <= q_pos
    sc = jnp.where(sc == 0.0, 0.0, sc)
    sc = jnp.where(causal, sc, NEG_INF)
    bits = pltpu.bitcast(sc, jnp.int32)
    key_ref[:, 0:S] = bits ^ ((bits >> 31) & jnp.int32(0x7FFFFFFF))

    def value_step(b, t_u):
        cand = t_u | lax.shift_left(jnp.int32(1), 31 - b)
        cnt = _row_count(key_ref[:, 0:S] >= (cand ^ jnp.int32(INT_MIN)))
        return jnp.where(cnt >= n_sel, cand, t_u)

    t_u = lax.fori_loop(0, 32, value_step, jnp.zeros((tq, 1), jnp.int32))
    thr = t_u ^ jnp.int32(INT_MIN)
    key = key_ref[:, 0:S]
    need = n_sel - _row_count(key > thr)
    tied = key == thr
    bias_ref[:, 0:S] = jnp.where((key >= thr) & causal, 0.0, NEG_INF)
    excess = _row_count(tied & causal) > need
    any_excess = jnp.max(jnp.where(excess, 1.0, 0.0)) > 0.0

    @pl.when(any_excess)
    def _():
        kp = lax.broadcasted_iota(jnp.int32, (tq, S), 1)
        qp = i * tq + lax.broadcasted_iota(jnp.int32, (tq, S), 0)
        bias_ref[:, 0:S] = jnp.where(key_ref[:, 0:S] == thr, kp.astype(F32), NOT_TIED)
        idx_bits = (S - 1).bit_length()

        def index_step(b, j0):
            cand = j0 | lax.shift_left(jnp.int32(1), (idx_bits - 1) - b)
            cnt = _row_count(bias_ref[:, 0:S] < cand.astype(F32))
            return jnp.where(cnt < need, cand, j0)

        j0 = lax.fori_loop(0, idx_bits, index_step, jnp.zeros((tq, 1), jnp.int32))
        admit = ((key_ref[:, 0:S] > thr) | (bias_ref[:, 0:S] <= j0.astype(F32))) & (kp <= qp)
        bias_ref[:, 0:S] = jnp.where(admit, 0.0, NEG_INF)

    for hh in range(N_HEADS):
        g = hh // (N_HEADS // N_KV_HEADS)
        kg = k_ref[0:S, g * HEAD_DIM:(g + 1) * HEAD_DIM]
        vg = v_ref[0:S, g * HEAD_DIM:(g + 1) * HEAD_DIM]
        logits = lax.dot_general(q_ref[:, hh * HEAD_DIM:(hh + 1) * HEAD_DIM], kg, nt,
                                 preferred_element_type=F32) + bias_ref[:, 0:S]
        m = jnp.max(logits, axis=-1, keepdims=True)
        p = jnp.exp(logits - m)
        l = jnp.sum(p, axis=-1, keepdims=True)
        o = jnp.dot(p.astype(BF16), vg, preferred_element_type=F32)
        o_ref[:, hh * HEAD_DIM:(hh + 1) * HEAD_DIM] = (o / l).astype(BF16)


def _attn_kernel(*refs, tq, L, n_sel):
    i = pl.program_id(1)
    band = L // CAUSAL_BANDS
    for v in range(CAUSAL_BANDS):
        @pl.when((i * tq) // band == v)
        def _(v=v):
            _attn_tile(*refs, i=i, tq=tq, S=(v + 1) * band, n_sel=n_sel)


def _attn(q, k, v, iq, ik, iw, B, L, tq):
    T = B * L
    nq = L // tq
    n_sel = min(TOPK_MAX, L // 4)
    assert L % (CAUSAL_BANDS * tq) == 0 and L // CAUSAL_BANDS >= n_sel
    qrow = lambda b, i: (b * nq + i, 0)
    seq = lambda b, i: (b, 0)
    return pl.pallas_call(
        functools.partial(_attn_kernel, tq=tq, L=L, n_sel=n_sel),
        grid=(B, nq),
        in_specs=[
            pl.BlockSpec((tq, ATTN_WIDTH), qrow),
            pl.BlockSpec((L, KV_WIDTH), seq),
            pl.BlockSpec((L, KV_WIDTH), seq),
            pl.BlockSpec((tq, IDX_HEADS * IDX_DIM), qrow),
            pl.BlockSpec((L, IDX_DIM), seq),
            pl.BlockSpec((tq, IDX_HEADS), qrow),
        ],
        out_specs=pl.BlockSpec((tq, ATTN_WIDTH), qrow),
        out_shape=jax.ShapeDtypeStruct((T, ATTN_WIDTH), BF16),
        scratch_shapes=[pltpu.VMEM((tq, L), jnp.int32), pltpu.VMEM((tq, L), F32)],
        compiler_params=pltpu.CompilerParams(
            dimension_semantics=("parallel", "arbitrary"), vmem_limit_bytes=VMEM_LIMIT),
    )(q, k, v, iq, ik, iw)


def _outproj_kernel(x_ref, u_ref, halo_ref, a_ref, gin_ref, bin_ref, wpool_ref, pscale_ref,
                    wout_ref, g1_ref, b1_ref, wr_hi_ref, wr_lo_ref, br_ref,
                    h1_ref, h1b_ref, ids_ref, gates_ref, *, tq, tiles_per_seq):
    i = pl.program_id(0)
    seq_tile = i % tiles_per_seq
    u = u_ref[...]
    halo = jnp.where(seq_tile == 0, 0.0, halo_ref[...])
    ext = jnp.concatenate([halo, u], axis=0)
    lane = lax.broadcasted_iota(jnp.int32, (tq, POOL_WIDTH), 1)
    grp = lane // POOL_GDIM
    win = jnp.zeros((tq, POOL_WIDTH), F32)
    s = ext
    for gi, w in enumerate(POOL_WINDOWS):
        s = s + pltpu.roll(s, w // 2, 0)
        win = jnp.where(grp == gi, s[POOL_HALO:, :], win)
    t_seq = seq_tile * tq + lax.broadcasted_iota(jnp.int32, (tq, POOL_WIDTH), 0)
    width = lax.shift_left(jnp.int32(2), grp)
    cnt = jnp.minimum(t_seq + 1, width).astype(F32)
    d = win / cnt - u
    y_pool = jnp.dot(d.astype(BF16), wpool_ref[...], preferred_element_type=F32) * pscale_ref[...]

    mix = jnp.dot(y_pool.astype(BF16), wout_ref[0:POOL_WIDTH, :], preferred_element_type=F32)
    mix = mix + jnp.dot(a_ref[...], wout_ref[POOL_WIDTH:, :], preferred_element_type=F32)
    h = _layer_norm(x_ref[...], gin_ref[...], bin_ref[...])
    h1 = _layer_norm(ALPHA_RES * h + mix, g1_ref[...], b1_ref[...])
    h1_ref[...] = h1
    h1_hi = h1.astype(BF16)
    h1b_ref[...] = h1_hi
    h1_lo = (h1 - h1_hi.astype(F32)).astype(BF16)
    wr_hi = wr_hi_ref[...]
    logits = (jnp.dot(h1_hi, wr_hi, preferred_element_type=F32)
              + jnp.dot(h1_lo, wr_hi, preferred_element_type=F32)
              + jnp.dot(h1_hi, wr_lo_ref[...], preferred_element_type=F32)) + br_ref[...]
    lane_e = lax.broadcasted_iota(jnp.int32, (tq, LANES), 1)
    lg = jnp.where(lane_e < N_EXPERTS, logits, -jnp.inf)
    ids = jnp.zeros((tq, LANES), jnp.int32)
    ex = jnp.zeros((tq, LANES), F32)
    top0 = None
    for kk in range(TOP_K):
        m = jnp.max(lg, axis=-1, keepdims=True)
        idx = jnp.min(jnp.where(lg == m, lane_e, LANES), axis=-1, keepdims=True)
        if top0 is None:
            top0 = m
        ids = jnp.where(lane_e == kk, idx, ids)
        ex = jnp.where(lane_e == kk, jnp.exp(m - top0), ex)
        lg = jnp.where(lane_e == idx, -jnp.inf, lg)
    ids_ref[...] = ids
    gates_ref[...] = ex / jnp.sum(ex, axis=-1, keepdims=True)


def _outproj(x2, u, attn, gin, bin_, wpool_bd, pscale, wout, g1, b1, wr_hi, wr_lo, br, L, tq):
    T = x2.shape[0]
    row = lambda i: (i, 0)
    fixed = lambda i: (0, 0)
    halo = lambda i: (jnp.maximum(i * (tq // POOL_HALO) - 1, 0), 0)
    out_shapes = (
        jax.ShapeDtypeStruct((T, D_MODEL), F32),
        jax.ShapeDtypeStruct((T, D_MODEL), BF16),
        jax.ShapeDtypeStruct((T, LANES), jnp.int32),
        jax.ShapeDtypeStruct((T, LANES), F32),
    )
    return pl.pallas_call(
        functools.partial(_outproj_kernel, tq=tq, tiles_per_seq=L // tq),
        grid=(T // tq,),
        in_specs=[
            pl.BlockSpec((tq, D_MODEL), row),
            pl.BlockSpec((tq, POOL_WIDTH), row),
            pl.BlockSpec((POOL_HALO, POOL_WIDTH), halo),
            pl.BlockSpec((tq, ATTN_WIDTH), row),
            pl.BlockSpec((1, D_MODEL), fixed),
            pl.BlockSpec((1, D_MODEL), fixed),
            pl.BlockSpec((POOL_WIDTH, POOL_WIDTH), fixed),
            pl.BlockSpec((1, POOL_WIDTH), fixed),
            pl.BlockSpec((D_MODEL, D_MODEL), fixed),
            pl.BlockSpec((1, D_MODEL), fixed),
            pl.BlockSpec((1, D_MODEL), fixed),
            pl.BlockSpec((D_MODEL, LANES), fixed),
            pl.BlockSpec((D_MODEL, LANES), fixed),
            pl.BlockSpec((1, LANES), fixed),
        ],
        out_specs=tuple(pl.BlockSpec((tq, s.shape[1]), row) for s in out_shapes),
        out_shape=out_shapes,
        compiler_params=pltpu.CompilerParams(
            dimension_semantics=("parallel",), vmem_limit_bytes=VMEM_LIMIT),
    )(x2, u, u, attn, gin, bin_, wpool_bd, pscale, wout, g1, b1, wr_hi, wr_lo, br)


ROUTE_TB = 512
ROW_ALIGN = 16
RUN_BITS = 6
STAGE_ROWS = 2560
FFN_TM = 256


def _route_kernel(ids_ref, slot_ref, slot_t_ref, meta_ref, carry_ref):
    c = pl.program_id(0)

    @pl.when(c == 0)
    def _():
        carry_ref[...] = jnp.zeros(carry_ref.shape, F32)

    tb = ids_ref.shape[0]
    ids = ids_ref[...]
    lane = lax.broadcasted_iota(jnp.int32, (tb, LANES), 1)
    onehot = [jnp.where(lane == ids[:, k:k + 1], 1.0, 0.0) for k in range(TOP_K)]
    member = onehot[0] + onehot[1] + onehot[2] + onehot[3]
    r = lax.broadcasted_iota(jnp.int32, (tb, tb), 0)
    cc = lax.broadcasted_iota(jnp.int32, (tb, tb), 1)
    before = jnp.where(cc < r, 1.0, 0.0).astype(BF16)
    lrank = jnp.dot(before, member.astype(BF16), preferred_element_type=F32)
    n = jnp.sum(member, axis=0, keepdims=True)
    units = jnp.ceil(n * (1.0 / ROW_ALIGN))
    er = lax.broadcasted_iota(jnp.int32, (LANES, LANES), 0)
    ec = lax.broadcasted_iota(jnp.int32, (LANES, LANES), 1)
    lower = jnp.where(er < ec, 1.0, 0.0).astype(BF16)
    off_units = jnp.dot(jnp.broadcast_to(units, (8, LANES)).astype(BF16), lower,
                        preferred_element_type=F32)[0:1, :]
    base = off_units * ROW_ALIGN + lrank
    slot = jnp.zeros((tb, LANES), jnp.int32)
    for k in range(TOP_K):
        sk = jnp.sum(onehot[k] * base, axis=-1, keepdims=True).astype(jnp.int32)
        slot = jnp.where(lane == k, sk, slot)
    slot_ref[...] = slot
    slot_t_ref[...] = slot.T[0:8, :]
    row = lax.broadcasted_iota(jnp.int32, (8, LANES), 0)
    meta = jnp.where(row == 0, units, jnp.where(row == 1, off_units, carry_ref[...]))
    meta_ref[...] = meta.astype(jnp.int32)
    carry_ref[...] = carry_ref[...] + units


def _route(ids):
    T = ids.shape[0]
    nblk = T // ROUTE_TB
    return pl.pallas_call(
        _route_kernel,
        grid=(nblk,),
        in_specs=[pl.BlockSpec((ROUTE_TB, LANES), lambda c: (c, 0))],
        out_specs=(pl.BlockSpec((ROUTE_TB, LANES), lambda c: (c, 0)),
                   pl.BlockSpec((8, ROUTE_TB), lambda c: (0, c)),
                   pl.BlockSpec((8, LANES), lambda c: (c, 0))),
        out_shape=(jax.ShapeDtypeStruct((T, LANES), jnp.int32),
                   jax.ShapeDtypeStruct((8, T), jnp.int32),
                   jax.ShapeDtypeStruct((nblk * 8, LANES), jnp.int32)),
        scratch_shapes=[pltpu.VMEM((1, LANES), F32)],
        compiler_params=pltpu.CompilerParams(
            dimension_semantics=("arbitrary",), vmem_limit_bytes=VMEM_LIMIT),
    )(ids)


def _run_pieces(units_ref, off_ref, dst_ref, c, visit):
    for e in range(N_EXPERTS):
        idx = c * N_EXPERTS + e
        m = units_ref[idx]
        so = off_ref[idx]
        do = dst_ref[idx]
        for b in range(RUN_BITS):
            done = m & ((1 << b) - 1)

            @pl.when(((m >> b) & 1) == 1)
            def _(e=e, b=b, done=done, so=so, do=do):
                visit(e, b, pl.multiple_of((so + done) * ROW_ALIGN, ROW_ALIGN),
                      pl.multiple_of((do + done) * ROW_ALIGN, ROW_ALIGN), ROW_ALIGN << b)


def _dispatch_kernel(units_ref, off_ref, dst_ref, slot_t_ref, h_ref, xs_init_ref, xs_ref,
                     stage_ref, sems):
    del xs_init_ref
    c = pl.program_id(0)
    tb = h_ref.shape[0]
    srow = lax.broadcasted_iota(jnp.int32, (STAGE_ROWS, tb), 0)
    hit = srow == slot_t_ref[0:1, :]
    for k in range(1, TOP_K):
        hit = hit | (srow == slot_t_ref[k:k + 1, :])
    perm = jnp.where(hit, 1.0, 0.0).astype(BF16)
    stage_ref[...] = jnp.dot(perm, h_ref[...], preferred_element_type=F32).astype(BF16)

    def piece(e, b, s_row, g_row, rows):
        return pltpu.make_async_copy(stage_ref.at[pl.ds(s_row, rows), :],
                                     xs_ref.at[pl.ds(g_row, rows), :], sems.at[e, b])

    _run_pieces(units_ref, off_ref, dst_ref, c, lambda *a: piece(*a).start())
    _run_pieces(units_ref, off_ref, dst_ref, c, lambda *a: piece(*a).wait())


def _dispatch(units, off, dst, slot_t, h1b, n_rows):
    T = h1b.shape[0]
    nblk = T // ROUTE_TB
    grid_spec = pltpu.PrefetchScalarGridSpec(
        num_scalar_prefetch=3,
        grid=(nblk,),
        in_specs=[pl.BlockSpec((8, ROUTE_TB), lambda c, *_: (0, c)),
                  pl.BlockSpec((ROUTE_TB, D_MODEL), lambda c, *_: (c, 0)),
                  pl.BlockSpec(memory_space=pl.ANY)],
        out_specs=pl.BlockSpec(memory_space=pl.ANY),
        scratch_shapes=[pltpu.VMEM((STAGE_ROWS, D_MODEL), BF16),
                        pltpu.SemaphoreType.DMA((N_EXPERTS, RUN_BITS))],
    )
    return pl.pallas_call(
        _dispatch_kernel,
        grid_spec=grid_spec,
        out_shape=jax.ShapeDtypeStruct((n_rows, D_MODEL), BF16),
        input_output_aliases={5: 0},
        compiler_params=pltpu.CompilerParams(
            dimension_semantics=("arbitrary",), vmem_limit_bytes=VMEM_LIMIT),
    )(units, off, dst, slot_t, h1b, jnp.zeros((n_rows, D_MODEL), BF16))


def _ffn_kernel(tile_e_ref, n_tiles_ref, xs_ref, wgu_ref, bgu_ref, wd_ref, bd_ref, y_ref,
                wgu_b, wd_b, *, n_chunk):
    j = pl.program_id(0)
    active = j < n_tiles_ref[0]
    new_expert = (j == 0) | (tile_e_ref[j] != tile_e_ref[jnp.maximum(j - 1, 0)])

    @pl.when(active & new_expert)
    def _():
        wgu_b[...] = wgu_ref[0].astype(BF16)
        wd_b[...] = wd_ref[0].astype(BF16)

    @pl.when(active)
    def _():
        xs = xs_ref[...]
        cw = D_FF // n_chunk
        acc = jnp.zeros(y_ref.shape, F32)
        for c in range(n_chunk):
            gate = jnp.dot(xs, wgu_b[:, c * cw:(c + 1) * cw], preferred_element_type=F32)
            gate = gate + bgu_ref[0, :, c * cw:(c + 1) * cw]
            up = jnp.dot(xs, wgu_b[:, D_FF + c * cw:D_FF + (c + 1) * cw],
                         preferred_element_type=F32)
            up = up + bgu_ref[0, :, D_FF + c * cw:D_FF + (c + 1) * cw]
            gate = jnp.minimum(gate, SWIGLU_LIMIT)
            up = jnp.clip(up, -SWIGLU_LIMIT, SWIGLU_LIMIT)
            act = (up + 1.0) * gate * jax.nn.sigmoid(SWIGLU_ALPHA * gate)
            acc = acc + jnp.dot(act.astype(BF16), wd_b[c * cw:(c + 1) * cw, :],
                                preferred_element_type=F32)
        y_ref[...] = (acc + bd_ref[0]).astype(BF16)

    @pl.when(j >= n_tiles_ref[0])
    def _():
        y_ref[...] = jnp.zeros(y_ref.shape, BF16)


def _ffn(tile_e, n_tiles, xs, wgu, bgu, wd, bd):
    n_rows = xs.shape[0]
    row = lambda j, te, nt: (j, 0)
    exp3 = lambda j, te, nt: (te[j], 0, 0)
    grid_spec = pltpu.PrefetchScalarGridSpec(
        num_scalar_prefetch=2,
        grid=(n_rows // FFN_TM,),
        in_specs=[
            pl.BlockSpec((FFN_TM, D_MODEL), row),
            pl.BlockSpec((1, D_MODEL, 2 * D_FF), exp3),
            pl.BlockSpec((1, 1, 2 * D_FF), exp3),
            pl.BlockSpec((1, D_FF, D_MODEL), exp3),
            pl.BlockSpec((1, 1, D_MODEL), exp3),
        ],
        out_specs=pl.BlockSpec((FFN_TM, D_MODEL), row),
        scratch_shapes=[pltpu.VMEM((D_MODEL, 2 * D_FF), BF16), pltpu.VMEM((D_FF, D_MODEL), BF16)],
    )
    return pl.pallas_call(
        functools.partial(_ffn_kernel, n_chunk=2),
        grid_spec=grid_spec,
        out_shape=jax.ShapeDtypeStruct((n_rows, D_MODEL), BF16),
        compiler_params=pltpu.CompilerParams(
            dimension_semantics=("arbitrary",), vmem_limit_bytes=FFN_VMEM_LIMIT),
    )(tile_e, n_tiles, xs, wgu, bgu, wd, bd)


def _combine_kernel(units_ref, off_ref, dst_ref, slot_ref, gates_ref, h1_ref, ys_ref,
                    g_ref, b_ref, o_ref, stage_ref, sems):
    c = pl.program_id(0)
    tb = h1_ref.shape[0]

    @pl.when(c == 0)
    def _():
        stage_ref[...] = jnp.zeros(stage_ref.shape, BF16)

    def piece(e, b, s_row, g_row, rows):
        return pltpu.make_async_copy(ys_ref.at[pl.ds(g_row, rows), :],
                                     stage_ref.at[pl.ds(s_row, rows), :], sems.at[e, b])

    _run_pieces(units_ref, off_ref, dst_ref, c, lambda *a: piece(*a).start())

    slot = slot_ref[...]
    gates = gates_ref[...]
    scol = lax.broadcasted_iota(jnp.int32, (tb, STAGE_ROWS), 1)
    w = jnp.zeros((tb, STAGE_ROWS), F32)
    for k in range(TOP_K):
        w = w + jnp.where(scol == slot[:, k:k + 1], gates[:, k:k + 1], 0.0)
    w_hi = w.astype(BF16)
    w_lo = (w - w_hi.astype(F32)).astype(BF16)

    _run_pieces(units_ref, off_ref, dst_ref, c, lambda *a: piece(*a).wait())

    y = stage_ref[...]
    ffn = (jnp.dot(w_hi, y, preferred_element_type=F32)
           + jnp.dot(w_lo, y, preferred_element_type=F32))
    o_ref[...] = _layer_norm(ALPHA_RES * h1_ref[...] + ffn, g_ref[...], b_ref[...])


def _combine(units, off, dst, slot, gates, h1, ys, g, b):
    T = h1.shape[0]
    nblk = T // ROUTE_TB
    blk = lambda c, *_: (c, 0)
    fixed = lambda c, *_: (0, 0)
    grid_spec = pltpu.PrefetchScalarGridSpec(
        num_scalar_prefetch=3,
        grid=(nblk,),
        in_specs=[pl.BlockSpec((ROUTE_TB, LANES), blk),
                  pl.BlockSpec((ROUTE_TB, LANES), blk),
                  pl.BlockSpec((ROUTE_TB, D_MODEL), blk),
                  pl.BlockSpec(memory_space=pl.ANY),
                  pl.BlockSpec((1, D_MODEL), fixed),
                  pl.BlockSpec((1, D_MODEL), fixed)],
        out_specs=pl.BlockSpec((ROUTE_TB, D_MODEL), blk),
        scratch_shapes=[pltpu.VMEM((STAGE_ROWS, D_MODEL), BF16),
                        pltpu.SemaphoreType.DMA((N_EXPERTS, RUN_BITS))],
    )
    return pl.pallas_call(
        _combine_kernel,
        grid_spec=grid_spec,
        out_shape=jax.ShapeDtypeStruct((T, D_MODEL), F32),
        compiler_params=pltpu.CompilerParams(
            dimension_semantics=("arbitrary",), vmem_limit_bytes=VMEM_LIMIT),
    )(units, off, dst, slot, gates, h1, ys, g, b)


def kernel(x, positions, ln_in_g, ln_in_b, w_in, w_pool, pool_scale, w_out, ln1_g, ln1_b,
           w_router, b_router, w_gate_up, b_gate_up, w_down, b_down, ln2_g, ln2_b):
    B, L, D = x.shape
    T = B * L
    assert T % ROUTE_TB == 0 and D == D_MODEL
    tq = min(256, L)
    x2 = x.reshape(T, D)
    pos2 = positions.reshape(T, 1)
    gin = ln_in_g.reshape(1, D)
    bin_ = ln_in_b.reshape(1, D)

    w_pad = jnp.pad(w_in[0], ((0, 0), (0, IN_WIDTH_PAD - IN_WIDTH))).astype(BF16)
    u, q, k, v, iq, ik, iw = _inproj(x2, pos2, gin, bin_, w_pad, tq)

    attn = _attn(q, k, v, iq, ik, iw, B, L, min(256, L // CAUSAL_BANDS))

    wpool_bd = jnp.zeros((POOL_WIDTH, POOL_WIDTH), F32)
    for gi in range(POOL_GROUPS):
        sl = slice(gi * POOL_GDIM, (gi + 1) * POOL_GDIM)
        wpool_bd = wpool_bd.at[sl, sl].set(w_pool[0, gi])
    wr = jnp.pad(w_router[0], ((0, 0), (0, LANES - N_EXPERTS)))
    wr_hi = wr.astype(BF16)
    wr_lo = (wr - wr_hi.astype(F32)).astype(BF16)
    br = jnp.pad(b_router[0], (0, LANES - N_EXPERTS)).reshape(1, LANES)
    h1, h1b, ids, gates = _outproj(
        x2, u, attn, gin, bin_, wpool_bd.astype(BF16), pool_scale[0].reshape(1, POOL_WIDTH),
        w_out[0].astype(BF16), ln1_g[0].reshape(1, D), ln1_b[0].reshape(1, D),
        wr_hi, wr_lo, br, L, tq)

    nblk = T // ROUTE_TB
    slot, slot_t, meta = _route(ids)
    meta = meta.reshape(nblk, 8, LANES)[:, :, :N_EXPERTS]
    units, off_units, base_units = meta[:, 0], meta[:, 1], meta[:, 2]
    tile_units = FFN_TM // ROW_ALIGN
    total_units = base_units[-1] + units[-1]
    region_units = ((total_units + tile_units - 1) // tile_units) * tile_units
    region_end = jnp.cumsum(region_units)
    dst_units = (region_end - region_units)[None, :] + base_units
    max_rows = T * TOP_K + nblk * N_EXPERTS * (ROW_ALIGN - 1) + N_EXPERTS * (FFN_TM - 1)
    n_rows = -(-max_rows // FFN_TM) * FFN_TM
    tile_start = jnp.arange(n_rows // FFN_TM, dtype=jnp.int32) * tile_units
    tile_e = jnp.minimum(jnp.sum(tile_start[:, None] >= region_end[None, :], axis=1),
                         N_EXPERTS - 1).astype(jnp.int32)
    n_tiles = (region_end[-1:] // tile_units).astype(jnp.int32)
    units_f = units.reshape(-1)
    off_f = off_units.reshape(-1)
    dst_f = dst_units.reshape(-1).astype(jnp.int32)

    xs = _dispatch(units_f, off_f, dst_f, slot_t, h1b, n_rows)
    ys = _ffn(tile_e, n_tiles, xs, w_gate_up[0], b_gate_up[0].reshape(N_EXPERTS, 1, 2 * D_FF),
              w_down[0], b_down[0].reshape(N_EXPERTS, 1, D))
    out = _combine(units_f, off_f, dst_f, slot, gates, h1, ys,
                   ln2_g[0].reshape(1, D), ln2_b[0].reshape(1, D))
    return out.reshape(B, L, D)
```

```python
import functools

import jax
import jax.numpy as jnp
import numpy as np
from jax import lax
from jax.experimental import pallas as pl
from jax.experimental.pallas import tpu as pltpu

F32 = jnp.float32
BF16 = jnp.bfloat16

D_MODEL = 1024
POOL_WIDTH = 256
POOL_GROUPS = 4
POOL_GDIM = 64
POOL_WINDOWS = (2, 4, 8, 16)
POOL_HALO = 16
HEAD_DIM = 128
ATTN_WIDTH = 768
N_HEADS = 6
N_KV_HEADS = 2
KV_WIDTH = 256
IDX_HEADS = 8
IDX_DIM = 64
TOPK_MAX = 256
ROPE_THETA = 500000.0
ROPE_DIM = 32
IDX_ROPE_DIM = 16
N_EXPERTS = 32
TOP_K = 4
D_FF = 1024
SWIGLU_LIMIT = 7.0
SWIGLU_ALPHA = 1.702
DEPTH = 1
ALPHA_RES = (2.0 * DEPTH) ** 0.25
LN_EPS = 1e-5
NEG_INF = -1e30
OFF_Q = 256
OFF_K = 1024
OFF_V = 1280
OFF_IQ = 1536
OFF_IK = 2048
IN_WIDTH = 2120
IN_WIDTH_PAD = 2176

LANES = 128
VMEM_LIMIT = 48 * 1024 * 1024
FFN_VMEM_LIMIT = 56 * 1024 * 1024
INT_MIN = -2 ** 31
NOT_TIED = 2 ** 30
CAUSAL_BANDS = 4
SEARCH_UNROLL = 4


def _layer_norm(x, g, b):
    mu = jnp.mean(x, axis=-1, keepdims=True)
    xc = x - mu
    var = jnp.mean(xc * xc, axis=-1, keepdims=True)
    return xc * lax.rsqrt(var + LN_EPS) * g + b


def _rope(xh, cos, sin_lo, sin_hi, half):
    return (xh * cos + pltpu.roll(xh, LANES - half, 1) * sin_lo
            + pltpu.roll(xh, half, 1) * sin_hi)


def _inproj_kernel(x_ref, pos_ref, g_ref, b_ref, w_ref, tab_ref,
                   u_ref, q_ref, k_ref, v_ref, iq_ref, ik_ref, iw_ref):
    h = _layer_norm(x_ref[...], g_ref[...], b_ref[...])
    proj = jnp.dot(h.astype(BF16), w_ref[...], preferred_element_type=F32)
    pos = pos_ref[...].astype(F32)

    lane = lax.broadcasted_iota(jnp.int32, (x_ref.shape[0], LANES), 1)
    ang = pos * tab_ref[0:1, :]
    cos_a = jnp.cos(ang)
    sin_a = jnp.sin(ang)
    cos_q = jnp.where(lane < ROPE_DIM, cos_a, 1.0)
    sin_q_lo = sin_a * tab_ref[1:2, :]
    sin_q_hi = sin_a * tab_ref[2:3, :]
    in_first = lane < IDX_ROPE_DIM
    in_second = (lane >= IDX_DIM) & (lane < IDX_DIM + IDX_ROPE_DIM)
    cos_i = jnp.where(in_first, pltpu.roll(cos_a, LANES - ROPE_DIM, 1),
                      jnp.where(in_second, pltpu.roll(cos_a, ROPE_DIM, 1), 1.0))
    sin_i = jnp.where(lane < IDX_DIM, pltpu.roll(sin_a, LANES - ROPE_DIM, 1),
                      pltpu.roll(sin_a, ROPE_DIM, 1))
    sin_i_lo = sin_i * tab_ref[4:5, :]
    sin_i_hi = sin_i * tab_ref[5:6, :]

    u_ref[...] = proj[:, 0:OFF_Q]
    scale = HEAD_DIM ** -0.5
    for hh in range(N_HEADS):
        c0 = OFF_Q + hh * HEAD_DIM
        r = _rope(proj[:, c0:c0 + HEAD_DIM], cos_q, sin_q_lo, sin_q_hi, ROPE_DIM // 2)
        q_ref[:, hh * HEAD_DIM:(hh + 1) * HEAD_DIM] = (r * scale).astype(BF16)
    for hh in range(N_KV_HEADS):
        c0 = OFF_K + hh * HEAD_DIM
        r = _rope(proj[:, c0:c0 + HEAD_DIM], cos_q, sin_q_lo, sin_q_hi, ROPE_DIM // 2)
        k_ref[:, hh * HEAD_DIM:(hh + 1) * HEAD_DIM] = r.astype(BF16)
    ones = jnp.ones((x_ref.shape[0], HEAD_DIM), BF16)
    for hh in range(N_KV_HEADS):
        c0 = OFF_V + hh * HEAD_DIM
        v_ref[:, 2 * hh * HEAD_DIM:(2 * hh + 1) * HEAD_DIM] = proj[:, c0:c0 + HEAD_DIM].astype(BF16)
        v_ref[:, (2 * hh + 1) * HEAD_DIM:(2 * hh + 2) * HEAD_DIM] = ones
    for t in range(IDX_HEADS * IDX_DIM // LANES):
        c0 = OFF_IQ + t * LANES
        r = _rope(proj[:, c0:c0 + LANES], cos_i, sin_i_lo, sin_i_hi, IDX_ROPE_DIM // 2)
        iq_ref[:, t * LANES:(t + 1) * LANES] = r.astype(BF16)
    tail = proj[:, OFF_IK:OFF_IK + LANES]
    r = _rope(tail, cos_i, sin_i_lo, sin_i_hi, IDX_ROPE_DIM // 2)
    ik_ref[...] = r[:, 0:IDX_DIM].astype(BF16)
    iw_ref[...] = tail[:, IDX_DIM:IDX_DIM + IDX_HEADS]


def _rope_tables():
    lane = np.arange(LANES)
    tab = np.zeros((8, LANES), np.float32)
    f_q = ROPE_THETA ** (-jnp.arange(0, ROPE_DIM, 2, dtype=F32) / ROPE_DIM)
    f_i = ROPE_THETA ** (-jnp.arange(0, IDX_ROPE_DIM, 2, dtype=F32) / IDX_ROPE_DIM)
    hq, hi = ROPE_DIM // 2, IDX_ROPE_DIM // 2
    in_q = lane < ROPE_DIM
    li = lane % IDX_DIM
    in_i = li < IDX_ROPE_DIM
    tab[1] = np.where(lane < hq, -1.0, 0.0)
    tab[2] = np.where(in_q & (lane >= hq), 1.0, 0.0)
    tab[4] = np.where(li < hi, -1.0, 0.0)
    tab[5] = np.where(in_i & (li >= hi), 1.0, 0.0)
    tab = jnp.asarray(tab)
    in_i_slot = (lane >= ROPE_DIM) & (lane < ROPE_DIM + IDX_ROPE_DIM)
    freq = jnp.where(in_q, f_q[lane % hq], jnp.where(in_i_slot, f_i[lane % hi], 0.0))
    return tab.at[0].set(freq)


def _inproj(x2, pos2, g, b, w_pad, tq):
    T = x2.shape[0]
    row = lambda i: (i, 0)
    fixed = lambda i: (0, 0)
    out_shapes = (
        jax.ShapeDtypeStruct((T, POOL_WIDTH), F32),
        jax.ShapeDtypeStruct((T, ATTN_WIDTH), BF16),
        jax.ShapeDtypeStruct((T, KV_WIDTH), BF16),
        jax.ShapeDtypeStruct((T, 2 * KV_WIDTH), BF16),
        jax.ShapeDtypeStruct((T, IDX_HEADS * IDX_DIM), BF16),
        jax.ShapeDtypeStruct((T, IDX_DIM), BF16),
        jax.ShapeDtypeStruct((T, IDX_HEADS), F32),
    )
    return pl.pallas_call(
        _inproj_kernel,
        grid=(T // tq,),
        in_specs=[
            pl.BlockSpec((tq, D_MODEL), row),
            pl.BlockSpec((tq, 1), row),
            pl.BlockSpec((1, D_MODEL), fixed),
            pl.BlockSpec((1, D_MODEL), fixed),
            pl.BlockSpec((D_MODEL, IN_WIDTH_PAD), fixed),
            pl.BlockSpec((8, LANES), fixed),
        ],
        out_specs=tuple(pl.BlockSpec((tq, s.shape[1]), row) for s in out_shapes),
        out_shape=out_shapes,
        compiler_params=pltpu.CompilerParams(
            dimension_semantics=("parallel",), vmem_limit_bytes=VMEM_LIMIT),
    )(x2, pos2, g, b, w_pad, _rope_tables())


def _row_count(mask):
    return jnp.sum(jnp.where(mask, 1.0, 0.0), axis=-1, keepdims=True)


def _attn_tile(q_ref, k_ref, v_ref, iq_ref, ik_ref, iw_ref, o_ref, key_ref, bias_ref,
               *, i, tq, S, n_sel):
    w_scale = (IDX_HEADS ** -0.5) * (IDX_DIM ** -0.5)
    iw = iw_ref[...] * w_scale
    ik = ik_ref[0:S, :]
    nt = (((1,), (1,)), ((), ()))
    sc = jnp.zeros((tq, S), F32)
    for hh in range(IDX_HEADS):
        s = lax.dot_general(iq_ref[:, hh * IDX_DIM:(hh + 1) * IDX_DIM], ik, nt,
                            preferred_element_type=F32)
        sc = sc + jnp.maximum(s, 0.0) * iw[:, hh:hh + 1]

    q_pos = i * tq + lax.broadcasted_iota(jnp.int32, (tq, S), 0)
    k_pos = lax.broadcasted_iota(jnp.int32, (tq, S), 1)
    causal = k_pos <= q_pos
    sc = jnp.where(sc == 0.0, 0.0, sc)
    sc = jnp.where(causal, sc, NEG_INF)
    bits = pltpu.bitcast(sc, jnp.int32)
    key_ref[:, 0:S] = bits ^ ((bits >> 31) & jnp.int32(0x7FFFFFFF))

    def value_step(b, t_u):
        cand = t_u | lax.shift_left(jnp.int32(1), 31 - b)
        cnt = _row_count(key_ref[:, 0:S] >= (cand ^ jnp.int32(INT_MIN)))
        return jnp.where(cnt >= n_sel, cand, t_u)

    def value_steps(bb, t_u):
        for r in range(SEARCH_UNROLL):
            t_u = value_step(bb * SEARCH_UNROLL + r, t_u)
        return t_u

    t_u = lax.fori_loop(0, 32 // SEARCH_UNROLL, value_steps, jnp.zeros((tq, 1), jnp.int32))
    thr = t_u ^ jnp.int32(INT_MIN)
    key = key_ref[:, 0:S]
    need = n_sel - _row_count(key > thr)
    tied = key == thr
    bias_ref[:, 0:S] = jnp.where((key >= thr) & causal, 0.0, NEG_INF)
    excess = _row_count(tied & causal) > need
    any_excess = jnp.max(jnp.where(excess, 1.0, 0.0)) > 0.0

    @pl.when(any_excess)
    def _():
        kp = lax.broadcasted_iota(jnp.int32, (tq, S), 1)
        qp = i * tq + lax.broadcasted_iota(jnp.int32, (tq, S), 0)
        bias_ref[:, 0:S] = jnp.where(key_ref[:, 0:S] == thr, kp.astype(F32), NOT_TIED)
        idx_bits = (S - 1).bit_length()

        def index_step(b, j0):
            cand = j0 | lax.shift_left(jnp.int32(1), (idx_bits - 1) - b)
            cnt = _row_count(bias_ref[:, 0:S] < cand.astype(F32))
            return jnp.where(cnt < need, cand, j0)

        j0 = lax.fori_loop(0, idx_bits, index_step, jnp.zeros((tq, 1), jnp.int32))
        admit = ((key_ref[:, 0:S] > thr) | (bias_ref[:, 0:S] <= j0.astype(F32))) & (kp <= qp)
        bias_ref[:, 0:S] = jnp.where(admit, 0.0, NEG_INF)

    for hh in range(N_HEADS):
        g = hh // (N_HEADS // N_KV_HEADS)
        kg = k_ref[0:S, g * HEAD_DIM:(g + 1) * HEAD_DIM]
        vg = v_ref[0:S, 2 * g * HEAD_DIM:2 * (g + 1) * HEAD_DIM]
        logits = lax.dot_general(q_ref[:, hh * HEAD_DIM:(hh + 1) * HEAD_DIM], kg, nt,
                                 preferred_element_type=F32) + bias_ref[:, 0:S]
        m = jnp.max(logits, axis=-1, keepdims=True)
        p = jnp.exp((logits - m).astype(BF16))
        ol = jnp.dot(p, vg, preferred_element_type=F32)
        o = ol[:, 0:HEAD_DIM] / ol[:, HEAD_DIM:HEAD_DIM + 1]
        o_ref[:, hh * HEAD_DIM:(hh + 1) * HEAD_DIM] = o.astype(BF16)


def _attn_kernel(*refs, tq, L, n_sel):
    i = pl.program_id(1)
    band = L // CAUSAL_BANDS
    for v in range(CAUSAL_BANDS):
        @pl.when((i * tq) // band == v)
        def _(v=v):
            _attn_tile(*refs, i=i, tq=tq, S=(v + 1) * band, n_sel=n_sel)


def _attn(q, k, v, iq, ik, iw, B, L, tq):
    T = B * L
    nq = L // tq
    n_sel = min(TOPK_MAX, L // 4)
    assert L % (CAUSAL_BANDS * tq) == 0 and L // CAUSAL_BANDS >= n_sel
    qrow = lambda b, i: (b * nq + i, 0)
    seq = lambda b, i: (b, 0)
    return pl.pallas_call(
        functools.partial(_attn_kernel, tq=tq, L=L, n_sel=n_sel),
        grid=(B, nq),
        in_specs=[
            pl.BlockSpec((tq, ATTN_WIDTH), qrow),
            pl.BlockSpec((L, KV_WIDTH), seq),
            pl.BlockSpec((L, 2 * KV_WIDTH), seq),
            pl.BlockSpec((tq, IDX_HEADS * IDX_DIM), qrow),
            pl.BlockSpec((L, IDX_DIM), seq),
            pl.BlockSpec((tq, IDX_HEADS), qrow),
        ],
        out_specs=pl.BlockSpec((tq, ATTN_WIDTH), qrow),
        out_shape=jax.ShapeDtypeStruct((T, ATTN_WIDTH), BF16),
        scratch_shapes=[pltpu.VMEM((tq, L), jnp.int32), pltpu.VMEM((tq, L), F32)],
        compiler_params=pltpu.CompilerParams(
            dimension_semantics=("parallel", "arbitrary"), vmem_limit_bytes=VMEM_LIMIT),
    )(q, k, v, iq, ik, iw)


def _outproj_kernel(x_ref, u_ref, halo_ref, a_ref, gin_ref, bin_ref, wpool_ref, pscale_ref,
                    wout_ref, g1_ref, b1_ref, wr_hi_ref, wr_lo_ref, br_ref,
                    h1_ref, h1b_ref, ids_ref, gates_ref, *, tq, tiles_per_seq):
    i = pl.program_id(0)
    seq_tile = i % tiles_per_seq
    u = u_ref[...]
    halo = jnp.where(seq_tile == 0, 0.0, halo_ref[...])
    ext = jnp.concatenate([halo, u], axis=0)
    lane = lax.broadcasted_iota(jnp.int32, (tq, POOL_WIDTH), 1)
    grp = lane // POOL_GDIM
    win = jnp.zeros((tq, POOL_WIDTH), F32)
    s = ext
    for gi, w in enumerate(POOL_WINDOWS):
        s = s + pltpu.roll(s, w // 2, 0)
        win = jnp.where(grp == gi, s[POOL_HALO:, :], win)
    t_seq = seq_tile * tq + lax.broadcasted_iota(jnp.int32, (tq, POOL_WIDTH), 0)
    width = lax.shift_left(jnp.int32(2), grp)
    cnt = jnp.minimum(t_seq + 1, width).astype(F32)
    d = win / cnt - u
    y_pool = jnp.dot(d.astype(BF16), wpool_ref[...], preferred_element_type=F32) * pscale_ref[...]

    mix = jnp.dot(y_pool.astype(BF16), wout_ref[0:POOL_WIDTH, :], preferred_element_type=F32)
    mix = mix + jnp.dot(a_ref[...], wout_ref[POOL_WIDTH:, :], preferred_element_type=F32)
    h = _layer_norm(x_ref[...], gin_ref[...], bin_ref[...])
    h1 = _layer_norm(ALPHA_RES * h + mix, g1_ref[...], b1_ref[...])
    h1_ref[...] = h1
    h1_hi = h1.astype(BF16)
    h1b_ref[...] = h1_hi
    h1_lo = (h1 - h1_hi.astype(F32)).astype(BF16)
    wr_hi = wr_hi_ref[...]
    logits = (jnp.dot(h1_hi, wr_hi, preferred_element_type=F32)
              + jnp.dot(h1_lo, wr_hi, preferred_element_type=F32)
              + jnp.dot(h1_hi, wr_lo_ref[...], preferred_element_type=F32)) + br_ref[...]
    lane_e = lax.broadcasted_iota(jnp.int32, (tq, LANES), 1)
    lg = jnp.where(lane_e < N_EXPERTS, logits, -jnp.inf)
    ids = jnp.zeros((tq, LANES), jnp.int32)
    ex = jnp.zeros((tq, LANES), F32)
    top0 = None
    for kk in range(TOP_K):
        m = jnp.max(lg, axis=-1, keepdims=True)
        idx = jnp.min(jnp.where(lg == m, lane_e, LANES), axis=-1, keepdims=True)
        if top0 is None:
            top0 = m
        ids = jnp.where(lane_e == kk, idx, ids)
        ex = jnp.where(lane_e == kk, jnp.exp(m - top0), ex)
        lg = jnp.where(lane_e == idx, -jnp.inf, lg)
    ids_ref[...] = ids
    gates_ref[...] = ex / jnp.sum(ex, axis=-1, keepdims=True)


def _outproj(x2, u, attn, gin, bin_, wpool_bd, pscale, wout, g1, b1, wr_hi, wr_lo, br, L, tq):
    T = x2.shape[0]
    row = lambda i: (i, 0)
    fixed = lambda i: (0, 0)
    halo = lambda i: (jnp.maximum(i * (tq // POOL_HALO) - 1, 0), 0)
    out_shapes = (
        jax.ShapeDtypeStruct((T, D_MODEL), F32),
        jax.ShapeDtypeStruct((T, D_MODEL), BF16),
        jax.ShapeDtypeStruct((T, LANES), jnp.int32),
        jax.ShapeDtypeStruct((T, LANES), F32),
    )
    return pl.pallas_call(
        functools.partial(_outproj_kernel, tq=tq, tiles_per_seq=L // tq),
        grid=(T // tq,),
        in_specs=[
            pl.BlockSpec((tq, D_MODEL), row),
            pl.BlockSpec((tq, POOL_WIDTH), row),
            pl.BlockSpec((POOL_HALO, POOL_WIDTH), halo),
            pl.BlockSpec((tq, ATTN_WIDTH), row),
            pl.BlockSpec((1, D_MODEL), fixed),
            pl.BlockSpec((1, D_MODEL), fixed),
            pl.BlockSpec((POOL_WIDTH, POOL_WIDTH), fixed),
            pl.BlockSpec((1, POOL_WIDTH), fixed),
            pl.BlockSpec((D_MODEL, D_MODEL), fixed),
            pl.BlockSpec((1, D_MODEL), fixed),
            pl.BlockSpec((1, D_MODEL), fixed),
            pl.BlockSpec((D_MODEL, LANES), fixed),
            pl.BlockSpec((D_MODEL, LANES), fixed),
            pl.BlockSpec((1, LANES), fixed),
        ],
        out_specs=tuple(pl.BlockSpec((tq, s.shape[1]), row) for s in out_shapes),
        out_shape=out_shapes,
        compiler_params=pltpu.CompilerParams(
            dimension_semantics=("parallel",), vmem_limit_bytes=VMEM_LIMIT),
    )(x2, u, u, attn, gin, bin_, wpool_bd, pscale, wout, g1, b1, wr_hi, wr_lo, br)


ROUTE_TB = 512
ROW_ALIGN = 16
RUN_BITS = 6
COMMON_RUN_BITS = 3
STAGE_ROWS = 2560
FFN_TM = 256


def _route_kernel(ids_ref, slot_ref, slot_t_ref, meta_ref, carry_ref):
    c = pl.program_id(0)

    @pl.when(c == 0)
    def _():
        carry_ref[...] = jnp.zeros(carry_ref.shape, F32)

    tb = ids_ref.shape[0]
    ids = ids_ref[...]
    lane = lax.broadcasted_iota(jnp.int32, (tb, LANES), 1)
    onehot = [jnp.where(lane == ids[:, k:k + 1], 1.0, 0.0) for k in range(TOP_K)]
    member = onehot[0] + onehot[1] + onehot[2] + onehot[3]
    r = lax.broadcasted_iota(jnp.int32, (tb, tb), 0)
    cc = lax.broadcasted_iota(jnp.int32, (tb, tb), 1)
    before = jnp.where(cc < r, 1.0, 0.0).astype(BF16)
    lrank = jnp.dot(before, member.astype(BF16), preferred_element_type=F32)
    n = jnp.sum(member, axis=0, keepdims=True)
    units = jnp.ceil(n * (1.0 / ROW_ALIGN))
    er = lax.broadcasted_iota(jnp.int32, (LANES, LANES), 0)
    ec = lax.broadcasted_iota(jnp.int32, (LANES, LANES), 1)
    lower = jnp.where(er < ec, 1.0, 0.0).astype(BF16)
    off_units = jnp.dot(jnp.broadcast_to(units, (8, LANES)).astype(BF16), lower,
                        preferred_element_type=F32)[0:1, :]
    base = off_units * ROW_ALIGN + lrank
    slot = jnp.zeros((tb, LANES), jnp.int32)
    for k in range(TOP_K):
        sk = jnp.sum(onehot[k] * base, axis=-1, keepdims=True).astype(jnp.int32)
        slot = jnp.where(lane == k, sk, slot)
    slot_ref[...] = slot
    slot_t_ref[...] = slot.T[0:8, :]
    row = lax.broadcasted_iota(jnp.int32, (8, LANES), 0)
    meta = jnp.where(row == 0, units, jnp.where(row == 1, off_units, carry_ref[...]))
    meta_ref[...] = meta.astype(jnp.int32)
    carry_ref[...] = carry_ref[...] + units


def _route(ids):
    T = ids.shape[0]
    nblk = T // ROUTE_TB
    return pl.pallas_call(
        _route_kernel,
        grid=(nblk,),
        in_specs=[pl.BlockSpec((ROUTE_TB, LANES), lambda c: (c, 0))],
        out_specs=(pl.BlockSpec((ROUTE_TB, LANES), lambda c: (c, 0)),
                   pl.BlockSpec((8, ROUTE_TB), lambda c: (0, c)),
                   pl.BlockSpec((8, LANES), lambda c: (c, 0))),
        out_shape=(jax.ShapeDtypeStruct((T, LANES), jnp.int32),
                   jax.ShapeDtypeStruct((8, T), jnp.int32),
                   jax.ShapeDtypeStruct((nblk * 8, LANES), jnp.int32)),
        scratch_shapes=[pltpu.VMEM((1, LANES), F32)],
        compiler_params=pltpu.CompilerParams(
            dimension_semantics=("arbitrary",), vmem_limit_bytes=VMEM_LIMIT),
    )(ids)


def _run_pieces(units_ref, off_ref, dst_ref, c, visit):
    for e in range(N_EXPERTS):
        idx = c * N_EXPERTS + e
        m = units_ref[idx]
        so = off_ref[idx]
        do = dst_ref[idx]

        def pieces(bits, e=e, m=m, so=so, do=do):
            for b in bits:
                done = m & ((1 << b) - 1)

                @pl.when(((m >> b) & 1) == 1)
                def _(b=b, done=done):
                    visit(e, b, pl.multiple_of((so + done) * ROW_ALIGN, ROW_ALIGN),
                          pl.multiple_of((do + done) * ROW_ALIGN, ROW_ALIGN), ROW_ALIGN << b)

        pieces(range(COMMON_RUN_BITS))
        pl.when(m >= (1 << COMMON_RUN_BITS))(lambda: pieces(range(COMMON_RUN_BITS, RUN_BITS)))


def _dispatch_kernel(units_ref, off_ref, dst_ref, slot_t_ref, h_ref, xs_init_ref, xs_ref,
                     stage_ref, sems):
    del xs_init_ref
    c = pl.program_id(0)
    tb = h_ref.shape[0]
    srow = lax.broadcasted_iota(jnp.int32, (STAGE_ROWS, tb), 0)
    hit = srow == slot_t_ref[0:1, :]
    for k in range(1, TOP_K):
        hit = hit | (srow == slot_t_ref[k:k + 1, :])
    perm = jnp.where(hit, 1.0, 0.0).astype(BF16)
    stage_ref[...] = jnp.dot(perm, h_ref[...], preferred_element_type=F32).astype(BF16)

    def piece(e, b, s_row, g_row, rows):
        return pltpu.make_async_copy(stage_ref.at[pl.ds(s_row, rows), :],
                                     xs_ref.at[pl.ds(g_row, rows), :], sems.at[e, b])

    _run_pieces(units_ref, off_ref, dst_ref, c, lambda *a: piece(*a).start())
    _run_pieces(units_ref, off_ref, dst_ref, c, lambda *a: piece(*a).wait())


def _dispatch(units, off, dst, slot_t, h1b, n_rows):
    T = h1b.shape[0]
    nblk = T // ROUTE_TB
    grid_spec = pltpu.PrefetchScalarGridSpec(
        num_scalar_prefetch=3,
        grid=(nblk,),
        in_specs=[pl.BlockSpec((8, ROUTE_TB), lambda c, *_: (0, c)),
                  pl.BlockSpec((ROUTE_TB, D_MODEL), lambda c, *_: (c, 0)),
                  pl.BlockSpec(memory_space=pl.ANY)],
        out_specs=pl.BlockSpec(memory_space=pl.ANY),
        scratch_shapes=[pltpu.VMEM((STAGE_ROWS, D_MODEL), BF16),
                        pltpu.SemaphoreType.DMA((N_EXPERTS, RUN_BITS))],
    )
    return pl.pallas_call(
        _dispatch_kernel,
        grid_spec=grid_spec,
        out_shape=jax.ShapeDtypeStruct((n_rows, D_MODEL), BF16),
        input_output_aliases={5: 0},
        compiler_params=pltpu.CompilerParams(
            dimension_semantics=("arbitrary",), vmem_limit_bytes=VMEM_LIMIT),
    )(units, off, dst, slot_t, h1b, jnp.zeros((n_rows, D_MODEL), BF16))


def _ffn_kernel(tile_e_ref, n_tiles_ref, xs_ref, wgu_ref, bgu_ref, wd_ref, bd_ref, y_ref,
                wgu_b, wd_b, *, n_chunk):
    j = pl.program_id(0)
    active = j < n_tiles_ref[0]
    new_expert = (j == 0) | (tile_e_ref[j] != tile_e_ref[jnp.maximum(j - 1, 0)])

    @pl.when(active & new_expert)
    def _():
        wgu_b[...] = wgu_ref[0].astype(BF16)
        wd_b[...] = wd_ref[0].astype(BF16)

    @pl.when(active)
    def _():
        xs = xs_ref[...]
        cw = D_FF // n_chunk
        acc = jnp.zeros(y_ref.shape, F32)
        for c in range(n_chunk):
            gate = jnp.dot(xs, wgu_b[:, c * cw:(c + 1) * cw], preferred_element_type=F32)
            gate = gate + bgu_ref[0, :, c * cw:(c + 1) * cw]
            up = jnp.dot(xs, wgu_b[:, D_FF + c * cw:D_FF + (c + 1) * cw],
                         preferred_element_type=F32)
            up = up + bgu_ref[0, :, D_FF + c * cw:D_FF + (c + 1) * cw]
            gate = jnp.minimum(gate, SWIGLU_LIMIT)
            up = jnp.clip(up, -SWIGLU_LIMIT, SWIGLU_LIMIT)
            act = (up + 1.0) * gate * jax.nn.sigmoid(SWIGLU_ALPHA * gate)
            acc = acc + jnp.dot(act.astype(BF16), wd_b[c * cw:(c + 1) * cw, :],
                                preferred_element_type=F32)
        y_ref[...] = (acc + bd_ref[0]).astype(BF16)

    @pl.when(j >= n_tiles_ref[0])
    def _():
        y_ref[...] = jnp.zeros(y_ref.shape, BF16)


def _ffn(tile_e, n_tiles, xs, wgu, bgu, wd, bd):
    n_rows = xs.shape[0]
    row = lambda j, te, nt: (j, 0)
    exp3 = lambda j, te, nt: (te[j], 0, 0)
    grid_spec = pltpu.PrefetchScalarGridSpec(
        num_scalar_prefetch=2,
        grid=(n_rows // FFN_TM,),
        in_specs=[
            pl.BlockSpec((FFN_TM, D_MODEL), row),
            pl.BlockSpec((1, D_MODEL, 2 * D_FF), exp3),
            pl.BlockSpec((1, 1, 2 * D_FF), exp3),
            pl.BlockSpec((1, D_FF, D_MODEL), exp3),
            pl.BlockSpec((1, 1, D_MODEL), exp3),
        ],
        out_specs=pl.BlockSpec((FFN_TM, D_MODEL), row),
        scratch_shapes=[pltpu.VMEM((D_MODEL, 2 * D_FF), BF16), pltpu.VMEM((D_FF, D_MODEL), BF16)],
    )
    return pl.pallas_call(
        functools.partial(_ffn_kernel, n_chunk=2),
        grid_spec=grid_spec,
        out_shape=jax.ShapeDtypeStruct((n_rows, D_MODEL), BF16),
        compiler_params=pltpu.CompilerParams(
            dimension_semantics=("arbitrary",), vmem_limit_bytes=FFN_VMEM_LIMIT),
    )(tile_e, n_tiles, xs, wgu, bgu, wd, bd)


def _combine_kernel(units_ref, off_ref, dst_ref, slot_ref, gates_ref, h1_ref, ys_ref,
                    g_ref, b_ref, o_ref, stage_ref, sems):
    c = pl.program_id(0)
    tb = h1_ref.shape[0]

    @pl.when(c == 0)
    def _():
        stage_ref[...] = jnp.zeros(stage_ref.shape, BF16)

    def piece(e, b, s_row, g_row, rows):
        return pltpu.make_async_copy(ys_ref.at[pl.ds(g_row, rows), :],
                                     stage_ref.at[pl.ds(s_row, rows), :], sems.at[e, b])

    _run_pieces(units_ref, off_ref, dst_ref, c, lambda *a: piece(*a).start())

    slot = slot_ref[...]
    gates = gates_ref[...]
    scol = lax.broadcasted_iota(jnp.int32, (tb, STAGE_ROWS), 1)
    w = jnp.zeros((tb, STAGE_ROWS), F32)
    for k in range(TOP_K):
        w = w + jnp.where(scol == slot[:, k:k + 1], gates[:, k:k + 1], 0.0)
    w = w.astype(BF16)

    _run_pieces(units_ref, off_ref, dst_ref, c, lambda *a: piece(*a).wait())

    ffn = jnp.dot(w, stage_ref[...], preferred_element_type=F32)
    o_ref[...] = _layer_norm(ALPHA_RES * h1_ref[...] + ffn, g_ref[...], b_ref[...])


def _combine(units, off, dst, slot, gates, h1, ys, g, b):
    T = h1.shape[0]
    nblk = T // ROUTE_TB
    blk = lambda c, *_: (c, 0)
    fixed = lambda c, *_: (0, 0)
    grid_spec = pltpu.PrefetchScalarGridSpec(
        num_scalar_prefetch=3,
        grid=(nblk,),
        in_specs=[pl.BlockSpec((ROUTE_TB, LANES), blk),
                  pl.BlockSpec((ROUTE_TB, LANES), blk),
                  pl.BlockSpec((ROUTE_TB, D_MODEL), blk),
                  pl.BlockSpec(memory_space=pl.ANY),
                  pl.BlockSpec((1, D_MODEL), fixed),
                  pl.BlockSpec((1, D_MODEL), fixed)],
        out_specs=pl.BlockSpec((ROUTE_TB, D_MODEL), blk),
        scratch_shapes=[pltpu.VMEM((STAGE_ROWS, D_MODEL), BF16),
                        pltpu.SemaphoreType.DMA((N_EXPERTS, RUN_BITS))],
    )
    return pl.pallas_call(
        _combine_kernel,
        grid_spec=grid_spec,
        out_shape=jax.ShapeDtypeStruct((T, D_MODEL), F32),
        compiler_params=pltpu.CompilerParams(
            dimension_semantics=("arbitrary",), vmem_limit_bytes=VMEM_LIMIT),
    )(units, off, dst, slot, gates, h1, ys, g, b)


def kernel(x, positions, ln_in_g, ln_in_b, w_in, w_pool, pool_scale, w_out, ln1_g, ln1_b,
           w_router, b_router, w_gate_up, b_gate_up, w_down, b_down, ln2_g, ln2_b):
    B, L, D = x.shape
    T = B * L
    assert T % ROUTE_TB == 0 and D == D_MODEL
    tq = min(256, L)
    x2 = x.reshape(T, D)
    pos2 = positions.reshape(T, 1)
    gin = ln_in_g.reshape(1, D)
    bin_ = ln_in_b.reshape(1, D)

    w_pad = jnp.pad(w_in[0], ((0, 0), (0, IN_WIDTH_PAD - IN_WIDTH))).astype(BF16)
    u, q, k, v, iq, ik, iw = _inproj(x2, pos2, gin, bin_, w_pad, tq)

    attn = _attn(q, k, v, iq, ik, iw, B, L, min(256, L // CAUSAL_BANDS))

    wpool_bd = jnp.zeros((POOL_WIDTH, POOL_WIDTH), F32)
    for gi in range(POOL_GROUPS):
        sl = slice(gi * POOL_GDIM, (gi + 1) * POOL_GDIM)
        wpool_bd = wpool_bd.at[sl, sl].set(w_pool[0, gi])
    wr = jnp.pad(w_router[0], ((0, 0), (0, LANES - N_EXPERTS)))
    wr_hi = wr.astype(BF16)
    wr_lo = (wr - wr_hi.astype(F32)).astype(BF16)
    br = jnp.pad(b_router[0], (0, LANES - N_EXPERTS)).reshape(1, LANES)
    h1, h1b, ids, gates = _outproj(
        x2, u, attn, gin, bin_, wpool_bd.astype(BF16), pool_scale[0].reshape(1, POOL_WIDTH),
        w_out[0].astype(BF16), ln1_g[0].reshape(1, D), ln1_b[0].reshape(1, D),
        wr_hi, wr_lo, br, L, tq)

    nblk = T // ROUTE_TB
    slot, slot_t, meta = _route(ids)
    meta = meta.reshape(nblk, 8, LANES)[:, :, :N_EXPERTS]
    units, off_units, base_units = meta[:, 0], meta[:, 1], meta[:, 2]
    tile_units = FFN_TM // ROW_ALIGN
    total_units = base_units[-1] + units[-1]
    region_units = ((total_units + tile_units - 1) // tile_units) * tile_units
    region_end = jnp.cumsum(region_units)
    dst_units = (region_end - region_units)[None, :] + base_units
    max_rows = T * TOP_K + nblk * N_EXPERTS * (ROW_ALIGN - 1) + N_EXPERTS * (FFN_TM - 1)
    n_rows = -(-max_rows // FFN_TM) * FFN_TM
    tile_start = jnp.arange(n_rows // FFN_TM, dtype=jnp.int32) * tile_units
    tile_e = jnp.minimum(jnp.sum(tile_start[:, None] >= region_end[None, :], axis=1),
                         N_EXPERTS - 1).astype(jnp.int32)
    n_tiles = (region_end[-1:] // tile_units).astype(jnp.int32)
    units_f = units.reshape(-1)
    off_f = off_units.reshape(-1)
    dst_f = dst_units.reshape(-1).astype(jnp.int32)

    xs = _dispatch(units_f, off_f, dst_f, slot_t, h1b, n_rows)
    ys = _ffn(tile_e, n_tiles, xs, w_gate_up[0], b_gate_up[0].reshape(N_EXPERTS, 1, 2 * D_FF),
              w_down[0], b_down[0].reshape(N_EXPERTS, 1, D))
    out = _combine(units_f, off_f, dst_f, slot, gates, h1, ys,
                   ln2_g[0].reshape(1, D), ln2_b[0].reshape(1, D))
    return out.reshape(B, L, D)
```

```python
import functools

import jax
import jax.numpy as jnp
import numpy as np
from jax import lax
from jax.experimental import pallas as pl
from jax.experimental.pallas import tpu as pltpu

F32 = jnp.float32
BF16 = jnp.bfloat16

D_MODEL = 1024
POOL_WIDTH = 256
POOL_GROUPS = 4
POOL_GDIM = 64
POOL_WINDOWS = (2, 4, 8, 16)
POOL_HALO = 16
HEAD_DIM = 128
ATTN_WIDTH = 768
N_HEADS = 6
N_KV_HEADS = 2
KV_WIDTH = 256
IDX_HEADS = 8
IDX_DIM = 64
TOPK_MAX = 256
ROPE_THETA = 500000.0
ROPE_DIM = 32
IDX_ROPE_DIM = 16
N_EXPERTS = 32
TOP_K = 4
D_FF = 1024
SWIGLU_LIMIT = 7.0
SWIGLU_ALPHA = 1.702
DEPTH = 1
ALPHA_RES = (2.0 * DEPTH) ** 0.25
LN_EPS = 1e-5
NEG_INF = -1e30
OFF_Q = 256
OFF_K = 1024
OFF_V = 1280
OFF_IQ = 1536
OFF_IK = 2048
IN_WIDTH = 2120
IN_WIDTH_PAD = 2176

LANES = 128
VMEM_LIMIT = 48 * 1024 * 1024
FFN_VMEM_LIMIT = 56 * 1024 * 1024
INT_MIN = -2 ** 31
NOT_TIED = 2 ** 30
CAUSAL_BANDS = 4
SEARCH_UNROLL = 4


def _layer_norm(x, g, b):
    mu = jnp.mean(x, axis=-1, keepdims=True)
    xc = x - mu
    var = jnp.mean(xc * xc, axis=-1, keepdims=True)
    return xc * lax.rsqrt(var + LN_EPS) * g + b


def _rope(xh, cos, sin_lo, sin_hi, half):
    return (xh * cos + pltpu.roll(xh, LANES - half, 1) * sin_lo
            + pltpu.roll(xh, half, 1) * sin_hi)


def _inproj_kernel(x_ref, pos_ref, g_ref, b_ref, w_ref, tab_ref,
                   u_ref, q_ref, k_ref, v_ref, iq_ref, ik_ref, iw_ref):
    h = _layer_norm(x_ref[...], g_ref[...], b_ref[...])
    proj = jnp.dot(h.astype(BF16), w_ref[...], preferred_element_type=F32)
    pos = pos_ref[...].astype(F32)

    lane = lax.broadcasted_iota(jnp.int32, (x_ref.shape[0], LANES), 1)
    ang = pos * tab_ref[0:1, :]
    cos_a = jnp.cos(ang)
    sin_a = jnp.sin(ang)
    cos_q = jnp.where(lane < ROPE_DIM, cos_a, 1.0)
    sin_q_lo = sin_a * tab_ref[1:2, :]
    sin_q_hi = sin_a * tab_ref[2:3, :]
    in_first = lane < IDX_ROPE_DIM
    in_second = (lane >= IDX_DIM) & (lane < IDX_DIM + IDX_ROPE_DIM)
    cos_i = jnp.where(in_first, pltpu.roll(cos_a, LANES - ROPE_DIM, 1),
                      jnp.where(in_second, pltpu.roll(cos_a, ROPE_DIM, 1), 1.0))
    sin_i = jnp.where(lane < IDX_DIM, pltpu.roll(sin_a, LANES - ROPE_DIM, 1),
                      pltpu.roll(sin_a, ROPE_DIM, 1))
    sin_i_lo = sin_i * tab_ref[4:5, :]
    sin_i_hi = sin_i * tab_ref[5:6, :]

    u_ref[...] = proj[:, 0:OFF_Q]
    scale = HEAD_DIM ** -0.5
    for hh in range(N_HEADS):
        c0 = OFF_Q + hh * HEAD_DIM
        r = _rope(proj[:, c0:c0 + HEAD_DIM], cos_q, sin_q_lo, sin_q_hi, ROPE_DIM // 2)
        q_ref[:, hh * HEAD_DIM:(hh + 1) * HEAD_DIM] = (r * scale).astype(BF16)
    for hh in range(N_KV_HEADS):
        c0 = OFF_K + hh * HEAD_DIM
        r = _rope(proj[:, c0:c0 + HEAD_DIM], cos_q, sin_q_lo, sin_q_hi, ROPE_DIM // 2)
        k_ref[:, hh * HEAD_DIM:(hh + 1) * HEAD_DIM] = r.astype(BF16)
    ones = jnp.ones((x_ref.shape[0], HEAD_DIM), BF16)
    for hh in range(N_KV_HEADS):
        c0 = OFF_V + hh * HEAD_DIM
        v_ref[:, 2 * hh * HEAD_DIM:(2 * hh + 1) * HEAD_DIM] = proj[:, c0:c0 + HEAD_DIM].astype(BF16)
        v_ref[:, (2 * hh + 1) * HEAD_DIM:(2 * hh + 2) * HEAD_DIM] = ones
    for t in range(IDX_HEADS * IDX_DIM // LANES):
        c0 = OFF_IQ + t * LANES
        r = _rope(proj[:, c0:c0 + LANES], cos_i, sin_i_lo, sin_i_hi, IDX_ROPE_DIM // 2)
        iq_ref[:, t * LANES:(t + 1) * LANES] = r.astype(BF16)
    tail = proj[:, OFF_IK:OFF_IK + LANES]
    r = _rope(tail, cos_i, sin_i_lo, sin_i_hi, IDX_ROPE_DIM // 2)
    ik_ref[...] = r[:, 0:IDX_DIM].astype(BF16)
    iw_ref[...] = tail[:, IDX_DIM:IDX_DIM + IDX_HEADS]


def _rope_tables():
    lane = np.arange(LANES)
    tab = np.zeros((8, LANES), np.float32)
    f_q = ROPE_THETA ** (-jnp.arange(0, ROPE_DIM, 2, dtype=F32) / ROPE_DIM)
    f_i = ROPE_THETA ** (-jnp.arange(0, IDX_ROPE_DIM, 2, dtype=F32) / IDX_ROPE_DIM)
    hq, hi = ROPE_DIM // 2, IDX_ROPE_DIM // 2
    in_q = lane < ROPE_DIM
    li = lane % IDX_DIM
    in_i = li < IDX_ROPE_DIM
    tab[1] = np.where(lane < hq, -1.0, 0.0)
    tab[2] = np.where(in_q & (lane >= hq), 1.0, 0.0)
    tab[4] = np.where(li < hi, -1.0, 0.0)
    tab[5] = np.where(in_i & (li >= hi), 1.0, 0.0)
    tab = jnp.asarray(tab)
    in_i_slot = (lane >= ROPE_DIM) & (lane < ROPE_DIM + IDX_ROPE_DIM)
    freq = jnp.where(in_q, f_q[lane % hq], jnp.where(in_i_slot, f_i[lane % hi], 0.0))
    return tab.at[0].set(freq)


def _inproj(x2, pos2, g, b, w_pad, tq):
    T = x2.shape[0]
    row = lambda i: (i, 0)
    fixed = lambda i: (0, 0)
    out_shapes = (
        jax.ShapeDtypeStruct((T, POOL_WIDTH), F32),
        jax.ShapeDtypeStruct((T, ATTN_WIDTH), BF16),
        jax.ShapeDtypeStruct((T, KV_WIDTH), BF16),
        jax.ShapeDtypeStruct((T, 2 * KV_WIDTH), BF16),
        jax.ShapeDtypeStruct((T, IDX_HEADS * IDX_DIM), BF16),
        jax.ShapeDtypeStruct((T, IDX_DIM), BF16),
        jax.ShapeDtypeStruct((T, IDX_HEADS), F32),
    )
    return pl.pallas_call(
        _inproj_kernel,
        grid=(T // tq,),
        in_specs=[
            pl.BlockSpec((tq, D_MODEL), row),
            pl.BlockSpec((tq, 1), row),
            pl.BlockSpec((1, D_MODEL), fixed),
            pl.BlockSpec((1, D_MODEL), fixed),
            pl.BlockSpec((D_MODEL, IN_WIDTH_PAD), fixed),
            pl.BlockSpec((8, LANES), fixed),
        ],
        out_specs=tuple(pl.BlockSpec((tq, s.shape[1]), row) for s in out_shapes),
        out_shape=out_shapes,
        compiler_params=pltpu.CompilerParams(
            dimension_semantics=("parallel",), vmem_limit_bytes=VMEM_LIMIT),
    )(x2, pos2, g, b, w_pad, _rope_tables())


def _row_count(mask):
    return jnp.sum(jnp.where(mask, 1.0, 0.0), axis=-1, keepdims=True)


def _attn_tile(q_ref, k_ref, v_ref, iq_ref, ik_ref, iw_ref, o_ref, key_ref, bias_ref,
               *, i, tq, S, n_sel):
    w_scale = (IDX_HEADS ** -0.5) * (IDX_DIM ** -0.5)
    iw = iw_ref[...] * w_scale
    ik = ik_ref[0:S, :]
    nt = (((1,), (1,)), ((), ()))
    sc = jnp.zeros((tq, S), F32)
    for hh in range(IDX_HEADS):
        s = lax.dot_general(iq_ref[:, hh * IDX_DIM:(hh + 1) * IDX_DIM], ik, nt,
                            preferred_element_type=F32)
        sc = sc + jnp.maximum(s, 0.0) * iw[:, hh:hh + 1]

    q_pos = i * tq + lax.broadcasted_iota(jnp.int32, (tq, S), 0)
    k_pos = lax.broadcasted_iota(jnp.int32, (tq, S), 1)
    causal = k_pos <= q_pos
    sc = jnp.where(sc == 0.0, 0.0, sc)
    sc = jnp.where(causal, sc, NEG_INF)
    bits = pltpu.bitcast(sc, jnp.int32)
    key_ref[:, 0:S] = bits ^ ((bits >> 31) & jnp.int32(0x7FFFFFFF))

    def value_step(b, t_u):
        cand = t_u | lax.shift_left(jnp.int32(1), 31 - b)
        cnt = _row_count(key_ref[:, 0:S] >= (cand ^ jnp.int32(INT_MIN)))
        return jnp.where(cnt >= n_sel, cand, t_u)

    def value_steps(bb, t_u):
        for r in range(SEARCH_UNROLL):
            t_u = value_step(bb * SEARCH_UNROLL + r, t_u)
        return t_u

    t_u = lax.fori_loop(0, 32 // SEARCH_UNROLL, value_steps, jnp.zeros((tq, 1), jnp.int32))
    thr = t_u ^ jnp.int32(INT_MIN)
    key = key_ref[:, 0:S]
    need = n_sel - _row_count(key > thr)
    tied = key == thr
    bias_ref[:, 0:S] = jnp.where((key >= thr) & causal, 0.0, NEG_INF)
    excess = _row_count(tied & causal) > need
    any_excess = jnp.max(jnp.where(excess, 1.0, 0.0)) > 0.0

    @pl.when(any_excess)
    def _():
        kp = lax.broadcasted_iota(jnp.int32, (tq, S), 1)
        qp = i * tq + lax.broadcasted_iota(jnp.int32, (tq, S), 0)
        bias_ref[:, 0:S] = jnp.where(key_ref[:, 0:S] == thr, kp.astype(F32), NOT_TIED)
        idx_bits = (S - 1).bit_length()

        j0 = jnp.zeros((tq, 1), jnp.int32)
        for b in range(idx_bits - 1, -1, -1):
            cand = j0 | jnp.int32(1 << b)
            cnt = _row_count(bias_ref[:, 0:S] < cand.astype(F32))
            j0 = jnp.where(cnt < need, cand, j0)
        admit = ((key_ref[:, 0:S] > thr) | (bias_ref[:, 0:S] <= j0.astype(F32))) & (kp <= qp)
        bias_ref[:, 0:S] = jnp.where(admit, 0.0, NEG_INF)

    for hh in range(N_HEADS):
        g = hh // (N_HEADS // N_KV_HEADS)
        kg = k_ref[0:S, g * HEAD_DIM:(g + 1) * HEAD_DIM]
        vg = v_ref[0:S, 2 * g * HEAD_DIM:2 * (g + 1) * HEAD_DIM]
        logits = lax.dot_general(q_ref[:, hh * HEAD_DIM:(hh + 1) * HEAD_DIM], kg, nt,
                                 preferred_element_type=F32) + bias_ref[:, 0:S]
        m = jnp.max(logits, axis=-1, keepdims=True)
        p = jnp.exp((logits - m).astype(BF16))
        ol = jnp.dot(p, vg, preferred_element_type=F32)
        o = ol[:, 0:HEAD_DIM] / ol[:, HEAD_DIM:HEAD_DIM + 1]
        o_ref[:, hh * HEAD_DIM:(hh + 1) * HEAD_DIM] = o.astype(BF16)


def _attn_kernel(*refs, tq, L, n_sel):
    i = pl.program_id(1)
    band = L // CAUSAL_BANDS
    for v in range(CAUSAL_BANDS):
        @pl.when((i * tq) // band == v)
        def _(v=v):
            _attn_tile(*refs, i=i, tq=tq, S=(v + 1) * band, n_sel=n_sel)


def _attn(q, k, v, iq, ik, iw, B, L, tq):
    T = B * L
    nq = L // tq
    n_sel = min(TOPK_MAX, L // 4)
    assert L % (CAUSAL_BANDS * tq) == 0 and L // CAUSAL_BANDS >= n_sel
    qrow = lambda b, i: (b * nq + i, 0)
    seq = lambda b, i: (b, 0)
    return pl.pallas_call(
        functools.partial(_attn_kernel, tq=tq, L=L, n_sel=n_sel),
        grid=(B, nq),
        in_specs=[
            pl.BlockSpec((tq, ATTN_WIDTH), qrow),
            pl.BlockSpec((L, KV_WIDTH), seq),
            pl.BlockSpec((L, 2 * KV_WIDTH), seq),
            pl.BlockSpec((tq, IDX_HEADS * IDX_DIM), qrow),
            pl.BlockSpec((L, IDX_DIM), seq),
            pl.BlockSpec((tq, IDX_HEADS), qrow),
        ],
        out_specs=pl.BlockSpec((tq, ATTN_WIDTH), qrow),
        out_shape=jax.ShapeDtypeStruct((T, ATTN_WIDTH), BF16),
        scratch_shapes=[pltpu.VMEM((tq, L), jnp.int32), pltpu.VMEM((tq, L), F32)],
        compiler_params=pltpu.CompilerParams(
            dimension_semantics=("parallel", "arbitrary"), vmem_limit_bytes=VMEM_LIMIT),
    )(q, k, v, iq, ik, iw)


def _outproj_kernel(x_ref, u_ref, halo_ref, a_ref, gin_ref, bin_ref, wpool_ref, pscale_ref,
                    wout_ref, g1_ref, b1_ref, wr_hi_ref, wr_lo_ref, br_ref,
                    h1_ref, h1b_ref, ids_ref, gates_ref, *, tq, tiles_per_seq):
    i = pl.program_id(0)
    seq_tile = i % tiles_per_seq
    u = u_ref[...]
    halo = jnp.where(seq_tile == 0, 0.0, halo_ref[...])
    ext = jnp.concatenate([halo, u], axis=0)
    lane = lax.broadcasted_iota(jnp.int32, (tq, POOL_WIDTH), 1)
    grp = lane // POOL_GDIM
    win = jnp.zeros((tq, POOL_WIDTH), F32)
    s = ext
    for gi, w in enumerate(POOL_WINDOWS):
        s = s + pltpu.roll(s, w // 2, 0)
        win = jnp.where(grp == gi, s[POOL_HALO:, :], win)
    t_seq = seq_tile * tq + lax.broadcasted_iota(jnp.int32, (tq, POOL_WIDTH), 0)
    width = lax.shift_left(jnp.int32(2), grp)
    cnt = jnp.minimum(t_seq + 1, width).astype(F32)
    d = win / cnt - u
    y_pool = jnp.dot(d.astype(BF16), wpool_ref[...], preferred_element_type=F32) * pscale_ref[...]

    mix = jnp.dot(y_pool.astype(BF16), wout_ref[0:POOL_WIDTH, :], preferred_element_type=F32)
    mix = mix + jnp.dot(a_ref[...], wout_ref[POOL_WIDTH:, :], preferred_element_type=F32)
    h = _layer_norm(x_ref[...], gin_ref[...], bin_ref[...])
    h1 = _layer_norm(ALPHA_RES * h + mix, g1_ref[...], b1_ref[...])
    h1_ref[...] = h1
    h1_hi = h1.astype(BF16)
    h1b_ref[...] = h1_hi
    h1_lo = (h1 - h1_hi.astype(F32)).astype(BF16)
    wr_hi = wr_hi_ref[...]
    logits = (jnp.dot(h1_hi, wr_hi, preferred_element_type=F32)
              + jnp.dot(h1_lo, wr_hi, preferred_element_type=F32)
              + jnp.dot(h1_hi, wr_lo_ref[...], preferred_element_type=F32)) + br_ref[...]
    lane_e = lax.broadcasted_iota(jnp.int32, (tq, LANES), 1)
    lg = jnp.where(lane_e < N_EXPERTS, logits, -jnp.inf)
    ids = jnp.zeros((tq, LANES), jnp.int32)
    ex = jnp.zeros((tq, LANES), F32)
    top0 = None
    for kk in range(TOP_K):
        m = jnp.max(lg, axis=-1, keepdims=True)
        idx = jnp.min(jnp.where(lg == m, lane_e, LANES), axis=-1, keepdims=True)
        if top0 is None:
            top0 = m
        ids = jnp.where(lane_e == kk, idx, ids)
        ex = jnp.where(lane_e == kk, jnp.exp(m - top0), ex)
        lg = jnp.where(lane_e == idx, -jnp.inf, lg)
    ids_ref[...] = ids
    gates_ref[...] = ex / jnp.sum(ex, axis=-1, keepdims=True)


def _outproj(x2, u, attn, gin, bin_, wpool_bd, pscale, wout, g1, b1, wr_hi, wr_lo, br, L, tq):
    T = x2.shape[0]
    row = lambda i: (i, 0)
    fixed = lambda i: (0, 0)
    halo = lambda i: (jnp.maximum(i * (tq // POOL_HALO) - 1, 0), 0)
    out_shapes = (
        jax.ShapeDtypeStruct((T, D_MODEL), F32),
        jax.ShapeDtypeStruct((T, D_MODEL), BF16),
        jax.ShapeDtypeStruct((T, LANES), jnp.int32),
        jax.ShapeDtypeStruct((T, LANES), F32),
    )
    return pl.pallas_call(
        functools.partial(_outproj_kernel, tq=tq, tiles_per_seq=L // tq),
        grid=(T // tq,),
        in_specs=[
            pl.BlockSpec((tq, D_MODEL), row),
            pl.BlockSpec((tq, POOL_WIDTH), row),
            pl.BlockSpec((POOL_HALO, POOL_WIDTH), halo),
            pl.BlockSpec((tq, ATTN_WIDTH), row),
            pl.BlockSpec((1, D_MODEL), fixed),
            pl.BlockSpec((1, D_MODEL), fixed),
            pl.BlockSpec((POOL_WIDTH, POOL_WIDTH), fixed),
            pl.BlockSpec((1, POOL_WIDTH), fixed),
            pl.BlockSpec((D_MODEL, D_MODEL), fixed),
            pl.BlockSpec((1, D_MODEL), fixed),
            pl.BlockSpec((1, D_MODEL), fixed),
            pl.BlockSpec((D_MODEL, LANES), fixed),
            pl.BlockSpec((D_MODEL, LANES), fixed),
            pl.BlockSpec((1, LANES), fixed),
        ],
        out_specs=tuple(pl.BlockSpec((tq, s.shape[1]), row) for s in out_shapes),
        out_shape=out_shapes,
        compiler_params=pltpu.CompilerParams(
            dimension_semantics=("parallel",), vmem_limit_bytes=VMEM_LIMIT),
    )(x2, u, u, attn, gin, bin_, wpool_bd, pscale, wout, g1, b1, wr_hi, wr_lo, br)


ROUTE_TB = 512
ROW_ALIGN = 16
RUN_BITS = 6
COMMON_RUN_BITS = 3
STAGE_ROWS = 2560
STAGE_CHUNK = 512
COMBINE_CHUNK = 1280
FFN_TM = 512
TAIL_BITS = 5


def _route_kernel(ids_ref, slot_ref, slot_t_ref, meta_ref, carry_ref):
    c = pl.program_id(0)

    @pl.when(c == 0)
    def _():
        carry_ref[...] = jnp.zeros(carry_ref.shape, F32)

    tb = ids_ref.shape[0]
    ids = ids_ref[...]
    lane = lax.broadcasted_iota(jnp.int32, (tb, LANES), 1)
    onehot = [jnp.where(lane == ids[:, k:k + 1], 1.0, 0.0) for k in range(TOP_K)]
    member = onehot[0] + onehot[1] + onehot[2] + onehot[3]
    r = lax.broadcasted_iota(jnp.int32, (tb, tb), 0)
    cc = lax.broadcasted_iota(jnp.int32, (tb, tb), 1)
    before = jnp.where(cc < r, 1.0, 0.0).astype(BF16)
    lrank = jnp.dot(before, member.astype(BF16), preferred_element_type=F32)
    n = jnp.sum(member, axis=0, keepdims=True)
    units = jnp.ceil(n * (1.0 / ROW_ALIGN))
    er = lax.broadcasted_iota(jnp.int32, (LANES, LANES), 0)
    ec = lax.broadcasted_iota(jnp.int32, (LANES, LANES), 1)
    lower = jnp.where(er < ec, 1.0, 0.0).astype(BF16)
    off_units = jnp.dot(jnp.broadcast_to(units, (8, LANES)).astype(BF16), lower,
                        preferred_element_type=F32)[0:1, :]
    base = off_units * ROW_ALIGN + lrank
    slot = jnp.zeros((tb, LANES), jnp.int32)
    for k in range(TOP_K):
        sk = jnp.sum(onehot[k] * base, axis=-1, keepdims=True).astype(jnp.int32)
        slot = jnp.where(lane == k, sk, slot)
    slot_ref[...] = slot
    slot_t_ref[...] = slot.T[0:8, :]
    row = lax.broadcasted_iota(jnp.int32, (8, LANES), 0)
    meta = jnp.where(row == 0, units, jnp.where(row == 1, off_units, carry_ref[...]))
    meta_ref[...] = meta.astype(jnp.int32)
    carry_ref[...] = carry_ref[...] + units


def _route(ids):
    T = ids.shape[0]
    nblk = T // ROUTE_TB
    return pl.pallas_call(
        _route_kernel,
        grid=(nblk,),
        in_specs=[pl.BlockSpec((ROUTE_TB, LANES), lambda c: (c, 0))],
        out_specs=(pl.BlockSpec((ROUTE_TB, LANES), lambda c: (c, 0)),
                   pl.BlockSpec((8, ROUTE_TB), lambda c: (0, c)),
                   pl.BlockSpec((8, LANES), lambda c: (c, 0))),
        out_shape=(jax.ShapeDtypeStruct((T, LANES), jnp.int32),
                   jax.ShapeDtypeStruct((8, T), jnp.int32),
                   jax.ShapeDtypeStruct((nblk * 8, LANES), jnp.int32)),
        scratch_shapes=[pltpu.VMEM((1, LANES), F32)],
        compiler_params=pltpu.CompilerParams(
            dimension_semantics=("arbitrary",), vmem_limit_bytes=VMEM_LIMIT),
    )(ids)


def _pieces_of(m, so, do, bits, visit):
    for b in bits:
        done = m & ((1 << b) - 1)

        @pl.when(((m >> b) & 1) == 1)
        def _(b=b, done=done):
            visit(b, pl.multiple_of((so + done) * ROW_ALIGN, ROW_ALIGN),
                  pl.multiple_of((do + done) * ROW_ALIGN, ROW_ALIGN), ROW_ALIGN << b)


def _run_pieces(units_ref, off_ref, dst_ref, c, visit):
    for e in range(N_EXPERTS):
        idx = c * N_EXPERTS + e
        m = units_ref[idx]
        so = off_ref[idx]
        do = dst_ref[idx]
        each = lambda b, s, g, rows, e=e: visit(e, b, s, g, rows)
        _pieces_of(m, so, do, range(COMMON_RUN_BITS), each)
        pl.when(m >= (1 << COMMON_RUN_BITS))(
            lambda m=m, so=so, do=do, each=each: _pieces_of(
                m, so, do, range(COMMON_RUN_BITS, RUN_BITS), each))


def _dispatch_kernel(units_ref, off_ref, dst_ref, tail_ref, tail_dst_ref, slot_t_ref, h_ref,
                     xs_ref, stage_ref, zero_ref, sems, *, nblk):
    c = pl.program_id(0)
    par = c % 2
    tb = h_ref.shape[0]
    for r0 in range(0, STAGE_ROWS, STAGE_CHUNK):
        srow = r0 + lax.broadcasted_iota(jnp.int32, (STAGE_CHUNK, tb), 0)
        hit = srow == slot_t_ref[0:1, :]
        for k in range(1, TOP_K):
            hit = hit | (srow == slot_t_ref[k:k + 1, :])
        perm = jnp.where(hit, 1.0, 0.0).astype(BF16)
        stage_ref[par, r0:r0 + STAGE_CHUNK, :] = jnp.dot(
            perm, h_ref[...], preferred_element_type=F32).astype(BF16)

    def piece(p):
        return lambda e, b, s_row, g_row, rows: pltpu.make_async_copy(
            stage_ref.at[p, pl.ds(s_row, rows), :], xs_ref.at[pl.ds(g_row, rows), :],
            sems.at[p, e, b])

    _run_pieces(units_ref, off_ref, dst_ref, c, lambda *a: piece(par)(*a).start())

    @pl.when(c > 0)
    def _():
        _run_pieces(units_ref, off_ref, dst_ref, c - 1, lambda *a: piece(1 - par)(*a).wait())

    @pl.when(c == nblk - 1)
    def _():
        _run_pieces(units_ref, off_ref, dst_ref, c, lambda *a: piece(par)(*a).wait())
        zero_ref[...] = jnp.zeros(zero_ref.shape, BF16)

        def tail(e, b, z_row, g_row, rows):
            del z_row
            return pltpu.make_async_copy(zero_ref.at[pl.ds(0, rows), :],
                                         xs_ref.at[pl.ds(g_row, rows), :], sems.at[0, e, b])

        def tails(act):
            for e in range(N_EXPERTS):
                _pieces_of(tail_ref[e], 0, tail_dst_ref[e], range(TAIL_BITS),
                           lambda b, z, g, rows, e=e: act(tail(e, b, z, g, rows)))

        tails(lambda cp: cp.start())
        tails(lambda cp: cp.wait())

        zrows = zero_ref.shape[0]
        used = (tail_dst_ref[N_EXPERTS - 1] + tail_ref[N_EXPERTS - 1]) * ROW_ALIGN
        n_rest = (xs_ref.shape[0] - used) // zrows

        def rest(i):
            row0 = pl.multiple_of(used + i * zrows, ROW_ALIGN)
            return pltpu.make_async_copy(
                zero_ref, xs_ref.at[pl.ds(row0, zrows), :],
                sems.at[i // (N_EXPERTS * RUN_BITS), (i // RUN_BITS) % N_EXPERTS, i % RUN_BITS])

        lax.fori_loop(0, n_rest, lambda i, _: (rest(i).start(), 0)[1], 0)
        lax.fori_loop(0, n_rest, lambda i, _: (rest(i).wait(), 0)[1], 0)


def _dispatch(units, off, dst, tail, tail_dst, slot_t, h1b, n_rows):
    T = h1b.shape[0]
    nblk = T // ROUTE_TB
    grid_spec = pltpu.PrefetchScalarGridSpec(
        num_scalar_prefetch=5,
        grid=(nblk,),
        in_specs=[pl.BlockSpec((8, ROUTE_TB), lambda c, *_: (0, c)),
                  pl.BlockSpec((ROUTE_TB, D_MODEL), lambda c, *_: (c, 0))],
        out_specs=pl.BlockSpec(memory_space=pl.ANY),
        scratch_shapes=[pltpu.VMEM((2, STAGE_ROWS, D_MODEL), BF16),
                        pltpu.VMEM((ROW_ALIGN << (TAIL_BITS - 1), D_MODEL), BF16),
                        pltpu.SemaphoreType.DMA((2, N_EXPERTS, RUN_BITS))],
    )
    return pl.pallas_call(
        functools.partial(_dispatch_kernel, nblk=nblk),
        grid_spec=grid_spec,
        out_shape=jax.ShapeDtypeStruct((n_rows, D_MODEL), BF16),
        compiler_params=pltpu.CompilerParams(
            dimension_semantics=("arbitrary",), vmem_limit_bytes=VMEM_LIMIT),
    )(units, off, dst, tail, tail_dst, slot_t, h1b)


def _ffn_kernel(tile_e_ref, n_tiles_ref, xs_ref, wgu_ref, bgu_ref, wd_ref, bd_ref, y_ref,
                wgu_b, wd_b, *, n_chunk):
    j = pl.program_id(0)
    active = j < n_tiles_ref[0]
    new_expert = (j == 0) | (tile_e_ref[j] != tile_e_ref[jnp.maximum(j - 1, 0)])

    @pl.when(active & new_expert)
    def _():
        wgu_b[...] = wgu_ref[0].astype(BF16)
        wd_b[...] = wd_ref[0].astype(BF16)

    @pl.when(active)
    def _():
        xs = xs_ref[...]
        cw = D_FF // n_chunk
        acc = jnp.zeros(y_ref.shape, F32)
        for c in range(n_chunk):
            gate = jnp.dot(xs, wgu_b[:, c * cw:(c + 1) * cw], preferred_element_type=F32)
            gate = gate + bgu_ref[0, :, c * cw:(c + 1) * cw]
            up = jnp.dot(xs, wgu_b[:, D_FF + c * cw:D_FF + (c + 1) * cw],
                         preferred_element_type=F32)
            up = up + bgu_ref[0, :, D_FF + c * cw:D_FF + (c + 1) * cw]
            gate = jnp.minimum(gate, SWIGLU_LIMIT)
            up = jnp.clip(up, -SWIGLU_LIMIT, SWIGLU_LIMIT)
            act = (up + 1.0) * gate * jax.nn.sigmoid(SWIGLU_ALPHA * gate)
            acc = acc + jnp.dot(act.astype(BF16), wd_b[c * cw:(c + 1) * cw, :],
                                preferred_element_type=F32)
        y_ref[...] = (acc + bd_ref[0]).astype(BF16)

    @pl.when(jnp.logical_not(active))
    def _():
        y_ref[...] = jnp.zeros(y_ref.shape, BF16)


def _ffn(tile_e, n_tiles, xs, wgu, bgu, wd, bd):
    n_rows = xs.shape[0]
    last = lambda j, nt: jnp.minimum(j, nt[0] - 1)
    row = lambda j, te, nt: (last(j, nt), 0)
    exp3 = lambda j, te, nt: (te[last(j, nt)], 0, 0)
    grid_spec = pltpu.PrefetchScalarGridSpec(
        num_scalar_prefetch=2,
        grid=(n_rows // FFN_TM,),
        in_specs=[
            pl.BlockSpec((FFN_TM, D_MODEL), row),
            pl.BlockSpec((1, D_MODEL, 2 * D_FF), exp3),
            pl.BlockSpec((1, 1, 2 * D_FF), exp3),
            pl.BlockSpec((1, D_FF, D_MODEL), exp3),
            pl.BlockSpec((1, 1, D_MODEL), exp3),
        ],
        out_specs=pl.BlockSpec((FFN_TM, D_MODEL), lambda j, te, nt: (j, 0)),
        scratch_shapes=[pltpu.VMEM((D_MODEL, 2 * D_FF), BF16), pltpu.VMEM((D_FF, D_MODEL), BF16)],
    )
    return pl.pallas_call(
        functools.partial(_ffn_kernel, n_chunk=2),
        grid_spec=grid_spec,
        out_shape=jax.ShapeDtypeStruct((n_rows, D_MODEL), BF16),
        compiler_params=pltpu.CompilerParams(
            dimension_semantics=("arbitrary",), vmem_limit_bytes=FFN_VMEM_LIMIT),
    )(tile_e, n_tiles, xs, wgu, bgu, wd, bd)


def _combine_kernel(units_ref, off_ref, dst_ref, slot_ref, gates_ref, h1_ref, ys_ref,
                    g_ref, b_ref, o_ref, stage_ref, sems, *, nblk):
    c = pl.program_id(0)
    par = c % 2
    tb = h1_ref.shape[0]

    def piece(p):
        return lambda e, b, s_row, g_row, rows: pltpu.make_async_copy(
            ys_ref.at[pl.ds(g_row, rows), :], stage_ref.at[p, pl.ds(s_row, rows), :],
            sems.at[p, e, b])

    @pl.when(c == 0)
    def _():
        stage_ref[...] = jnp.zeros(stage_ref.shape, BF16)
        _run_pieces(units_ref, off_ref, dst_ref, c, lambda *a: piece(0)(*a).start())

    _run_pieces(units_ref, off_ref, dst_ref, c, lambda *a: piece(par)(*a).wait())

    @pl.when(c + 1 < nblk)
    def _():
        _run_pieces(units_ref, off_ref, dst_ref, c + 1, lambda *a: piece(1 - par)(*a).start())

    slot = slot_ref[...]
    gates = gates_ref[...]
    ffn = jnp.zeros((tb, D_MODEL), F32)
    for r0 in range(0, STAGE_ROWS, COMBINE_CHUNK):
        scol = r0 + lax.broadcasted_iota(jnp.int32, (tb, COMBINE_CHUNK), 1)
        w = jnp.zeros((tb, COMBINE_CHUNK), F32)
        for k in range(TOP_K):
            w = w + jnp.where(scol == slot[:, k:k + 1], gates[:, k:k + 1], 0.0)
        ffn = ffn + jnp.dot(w.astype(BF16), stage_ref[par, r0:r0 + COMBINE_CHUNK, :],
                            preferred_element_type=F32)
    o_ref[...] = _layer_norm(ALPHA_RES * h1_ref[...] + ffn, g_ref[...], b_ref[...])


def _combine(units, off, dst, slot, gates, h1, ys, g, b):
    T = h1.shape[0]
    nblk = T // ROUTE_TB
    blk = lambda c, *_: (c, 0)
    fixed = lambda c, *_: (0, 0)
    grid_spec = pltpu.PrefetchScalarGridSpec(
        num_scalar_prefetch=3,
        grid=(nblk,),
        in_specs=[pl.BlockSpec((ROUTE_TB, LANES), blk),
                  pl.BlockSpec((ROUTE_TB, LANES), blk),
                  pl.BlockSpec((ROUTE_TB, D_MODEL), blk),
                  pl.BlockSpec(memory_space=pl.ANY),
                  pl.BlockSpec((1, D_MODEL), fixed),
                  pl.BlockSpec((1, D_MODEL), fixed)],
        out_specs=pl.BlockSpec((ROUTE_TB, D_MODEL), blk),
        scratch_shapes=[pltpu.VMEM((2, STAGE_ROWS, D_MODEL), BF16),
                        pltpu.SemaphoreType.DMA((2, N_EXPERTS, RUN_BITS))],
    )
    return pl.pallas_call(
        functools.partial(_combine_kernel, nblk=nblk),
        grid_spec=grid_spec,
        out_shape=jax.ShapeDtypeStruct((T, D_MODEL), F32),
        compiler_params=pltpu.CompilerParams(
            dimension_semantics=("arbitrary",), vmem_limit_bytes=VMEM_LIMIT),
    )(units, off, dst, slot, gates, h1, ys, g, b)


def kernel(x, positions, ln_in_g, ln_in_b, w_in, w_pool, pool_scale, w_out, ln1_g, ln1_b,
           w_router, b_router, w_gate_up, b_gate_up, w_down, b_down, ln2_g, ln2_b):
    B, L, D = x.shape
    T = B * L
    assert T % ROUTE_TB == 0 and D == D_MODEL
    tq = min(256, L)
    x2 = x.reshape(T, D)
    pos2 = positions.reshape(T, 1)
    gin = ln_in_g.reshape(1, D)
    bin_ = ln_in_b.reshape(1, D)

    w_pad = jnp.pad(w_in[0], ((0, 0), (0, IN_WIDTH_PAD - IN_WIDTH))).astype(BF16)
    u, q, k, v, iq, ik, iw = _inproj(x2, pos2, gin, bin_, w_pad, tq)

    attn = _attn(q, k, v, iq, ik, iw, B, L, min(256, L // CAUSAL_BANDS))

    wpool_bd = jnp.zeros((POOL_WIDTH, POOL_WIDTH), F32)
    for gi in range(POOL_GROUPS):
        sl = slice(gi * POOL_GDIM, (gi + 1) * POOL_GDIM)
        wpool_bd = wpool_bd.at[sl, sl].set(w_pool[0, gi])
    wr = jnp.pad(w_router[0], ((0, 0), (0, LANES - N_EXPERTS)))
    wr_hi = wr.astype(BF16)
    wr_lo = (wr - wr_hi.astype(F32)).astype(BF16)
    br = jnp.pad(b_router[0], (0, LANES - N_EXPERTS)).reshape(1, LANES)
    h1, h1b, ids, gates = _outproj(
        x2, u, attn, gin, bin_, wpool_bd.astype(BF16), pool_scale[0].reshape(1, POOL_WIDTH),
        w_out[0].astype(BF16), ln1_g[0].reshape(1, D), ln1_b[0].reshape(1, D),
        wr_hi, wr_lo, br, L, tq)

    nblk = T // ROUTE_TB
    slot, slot_t, meta = _route(ids)
    meta = meta.reshape(nblk, 8, LANES)[:, :, :N_EXPERTS]
    units, off_units, base_units = meta[:, 0], meta[:, 1], meta[:, 2]
    tile_units = FFN_TM // ROW_ALIGN
    total_units = base_units[-1] + units[-1]
    region_units = ((total_units + tile_units - 1) // tile_units) * tile_units
    region_end = jnp.cumsum(region_units)
    dst_units = (region_end - region_units)[None, :] + base_units
    max_rows = T * TOP_K + nblk * N_EXPERTS * (ROW_ALIGN - 1) + N_EXPERTS * (FFN_TM - 1)
    n_rows = -(-max_rows // FFN_TM) * FFN_TM
    tile_start = jnp.arange(n_rows // FFN_TM, dtype=jnp.int32) * tile_units
    tile_e = jnp.minimum(jnp.sum(tile_start[:, None] >= region_end[None, :], axis=1),
                         N_EXPERTS - 1).astype(jnp.int32)
    n_tiles = (region_end[-1:] // tile_units).astype(jnp.int32)
    units_f = units.reshape(-1)
    off_f = off_units.reshape(-1)
    dst_f = dst_units.reshape(-1).astype(jnp.int32)
    tail_units = (region_units - total_units).astype(jnp.int32)
    tail_dst = (region_end - tail_units).astype(jnp.int32)

    xs = _dispatch(units_f, off_f, dst_f, tail_units, tail_dst, slot_t, h1b, n_rows)
    ys = _ffn(tile_e, n_tiles, xs, w_gate_up[0], b_gate_up[0].reshape(N_EXPERTS, 1, 2 * D_FF),
              w_down[0], b_down[0].reshape(N_EXPERTS, 1, D))
    out = _combine(units_f, off_f, dst_f, slot, gates, h1, ys,
                   ln2_g[0].reshape(1, D), ln2_b[0].reshape(1, D))
    return out.reshape(B, L, D)
```

```python
import functools

import jax
import jax.numpy as jnp
import numpy as np
from jax import lax
from jax.experimental import pallas as pl
from jax.experimental.pallas import tpu as pltpu

F32 = jnp.float32
BF16 = jnp.bfloat16

D_MODEL = 1024
POOL_WIDTH = 256
POOL_GROUPS = 4
POOL_GDIM = 64
POOL_WINDOWS = (2, 4, 8, 16)
POOL_HALO = 16
HEAD_DIM = 128
ATTN_WIDTH = 768
N_HEADS = 6
N_KV_HEADS = 2
KV_WIDTH = 256
IDX_HEADS = 8
IDX_DIM = 64
TOPK_MAX = 256
ROPE_THETA = 500000.0
ROPE_DIM = 32
IDX_ROPE_DIM = 16
N_EXPERTS = 32
TOP_K = 4
D_FF = 1024
SWIGLU_LIMIT = 7.0
SWIGLU_ALPHA = 1.702
DEPTH = 1
ALPHA_RES = (2.0 * DEPTH) ** 0.25
LN_EPS = 1e-5
NEG_INF = -1e30
OFF_Q = 256
OFF_K = 1024
OFF_V = 1280
OFF_IQ = 1536
OFF_IK = 2048
IN_WIDTH = 2120
IN_WIDTH_PAD = 2176

LANES = 128
VMEM_LIMIT = 48 * 1024 * 1024
FFN_VMEM_LIMIT = 56 * 1024 * 1024
INT_MIN = -2 ** 31
NOT_TIED = 2 ** 30
CAUSAL_BANDS = 4
SEARCH_UNROLL = 4


def _layer_norm(x, g, b):
    mu = jnp.mean(x, axis=-1, keepdims=True)
    xc = x - mu
    var = jnp.mean(xc * xc, axis=-1, keepdims=True)
    return xc * lax.rsqrt(var + LN_EPS) * g + b


def _rope(xh, cos, sin_lo, sin_hi, half):
    return (xh * cos + pltpu.roll(xh, LANES - half, 1) * sin_lo
            + pltpu.roll(xh, half, 1) * sin_hi)


def _inproj_kernel(x_ref, pos_ref, g_ref, b_ref, w_ref, tab_ref,
                   u_ref, q_ref, k_ref, v_ref, iq_ref, ik_ref, iw_ref):
    h = _layer_norm(x_ref[...], g_ref[...], b_ref[...])
    proj = jnp.dot(h.astype(BF16), w_ref[...], preferred_element_type=F32)
    pos = pos_ref[...].astype(F32)

    lane = lax.broadcasted_iota(jnp.int32, (x_ref.shape[0], LANES), 1)
    ang = pos * tab_ref[0:1, :]
    cos_a = jnp.cos(ang)
    sin_a = jnp.sin(ang)
    cos_q = jnp.where(lane < ROPE_DIM, cos_a, 1.0)
    sin_q_lo = sin_a * tab_ref[1:2, :]
    sin_q_hi = sin_a * tab_ref[2:3, :]
    in_first = lane < IDX_ROPE_DIM
    in_second = (lane >= IDX_DIM) & (lane < IDX_DIM + IDX_ROPE_DIM)
    cos_i = jnp.where(in_first, pltpu.roll(cos_a, LANES - ROPE_DIM, 1),
                      jnp.where(in_second, pltpu.roll(cos_a, ROPE_DIM, 1), 1.0))
    sin_i = jnp.where(lane < IDX_DIM, pltpu.roll(sin_a, LANES - ROPE_DIM, 1),
                      pltpu.roll(sin_a, ROPE_DIM, 1))
    sin_i_lo = sin_i * tab_ref[4:5, :]
    sin_i_hi = sin_i * tab_ref[5:6, :]

    u_ref[...] = proj[:, 0:OFF_Q]
    scale = HEAD_DIM ** -0.5
    for hh in range(N_HEADS):
        c0 = OFF_Q + hh * HEAD_DIM
        r = _rope(proj[:, c0:c0 + HEAD_DIM], cos_q, sin_q_lo, sin_q_hi, ROPE_DIM // 2)
        q_ref[:, hh * HEAD_DIM:(hh + 1) * HEAD_DIM] = (r * scale).astype(BF16)
    for hh in range(N_KV_HEADS):
        c0 = OFF_K + hh * HEAD_DIM
        r = _rope(proj[:, c0:c0 + HEAD_DIM], cos_q, sin_q_lo, sin_q_hi, ROPE_DIM // 2)
        k_ref[:, hh * HEAD_DIM:(hh + 1) * HEAD_DIM] = r.astype(BF16)
    ones = jnp.ones((x_ref.shape[0], HEAD_DIM), BF16)
    for hh in range(N_KV_HEADS):
        c0 = OFF_V + hh * HEAD_DIM
        v_ref[:, 2 * hh * HEAD_DIM:(2 * hh + 1) * HEAD_DIM] = proj[:, c0:c0 + HEAD_DIM].astype(BF16)
        v_ref[:, (2 * hh + 1) * HEAD_DIM:(2 * hh + 2) * HEAD_DIM] = ones
    for t in range(IDX_HEADS * IDX_DIM // LANES):
        c0 = OFF_IQ + t * LANES
        r = _rope(proj[:, c0:c0 + LANES], cos_i, sin_i_lo, sin_i_hi, IDX_ROPE_DIM // 2)
        iq_ref[:, t * LANES:(t + 1) * LANES] = r.astype(BF16)
    tail = proj[:, OFF_IK:OFF_IK + LANES]
    r = _rope(tail, cos_i, sin_i_lo, sin_i_hi, IDX_ROPE_DIM // 2)
    ik_ref[...] = r[:, 0:IDX_DIM].astype(BF16)
    iw_ref[...] = tail[:, IDX_DIM:IDX_DIM + IDX_HEADS]


def _rope_tables():
    lane = np.arange(LANES)
    tab = np.zeros((8, LANES), np.float32)
    f_q = ROPE_THETA ** (-jnp.arange(0, ROPE_DIM, 2, dtype=F32) / ROPE_DIM)
    f_i = ROPE_THETA ** (-jnp.arange(0, IDX_ROPE_DIM, 2, dtype=F32) / IDX_ROPE_DIM)
    hq, hi = ROPE_DIM // 2, IDX_ROPE_DIM // 2
    in_q = lane < ROPE_DIM
    li = lane % IDX_DIM
    in_i = li < IDX_ROPE_DIM
    tab[1] = np.where(lane < hq, -1.0, 0.0)
    tab[2] = np.where(in_q & (lane >= hq), 1.0, 0.0)
    tab[4] = np.where(li < hi, -1.0, 0.0)
    tab[5] = np.where(in_i & (li >= hi), 1.0, 0.0)
    tab = jnp.asarray(tab)
    in_i_slot = (lane >= ROPE_DIM) & (lane < ROPE_DIM + IDX_ROPE_DIM)
    freq = jnp.where(in_q, f_q[lane % hq], jnp.where(in_i_slot, f_i[lane % hi], 0.0))
    return tab.at[0].set(freq)


def _inproj(x2, pos2, g, b, w_pad, tq):
    T = x2.shape[0]
    row = lambda i: (i, 0)
    fixed = lambda i: (0, 0)
    out_shapes = (
        jax.ShapeDtypeStruct((T, POOL_WIDTH), F32),
        jax.ShapeDtypeStruct((T, ATTN_WIDTH), BF16),
        jax.ShapeDtypeStruct((T, KV_WIDTH), BF16),
        jax.ShapeDtypeStruct((T, 2 * KV_WIDTH), BF16),
        jax.ShapeDtypeStruct((T, IDX_HEADS * IDX_DIM), BF16),
        jax.ShapeDtypeStruct((T, IDX_DIM), BF16),
        jax.ShapeDtypeStruct((T, IDX_HEADS), F32),
    )
    return pl.pallas_call(
        _inproj_kernel,
        grid=(T // tq,),
        in_specs=[
            pl.BlockSpec((tq, D_MODEL), row),
            pl.BlockSpec((tq, 1), row),
            pl.BlockSpec((1, D_MODEL), fixed),
            pl.BlockSpec((1, D_MODEL), fixed),
            pl.BlockSpec((D_MODEL, IN_WIDTH_PAD), fixed),
            pl.BlockSpec((8, LANES), fixed),
        ],
        out_specs=tuple(pl.BlockSpec((tq, s.shape[1]), row) for s in out_shapes),
        out_shape=out_shapes,
        compiler_params=pltpu.CompilerParams(
            dimension_semantics=("parallel",), vmem_limit_bytes=VMEM_LIMIT),
    )(x2, pos2, g, b, w_pad, _rope_tables())


def _row_count(mask):
    return jnp.sum(jnp.where(mask, 1.0, 0.0), axis=-1, keepdims=True)


def _attn_tile(q_ref, k_ref, v_ref, iq_ref, ik_ref, iw_ref, o_ref, key_ref, bias_ref,
               *, i, tq, S, n_sel):
    w_scale = (IDX_HEADS ** -0.5) * (IDX_DIM ** -0.5)
    iw = iw_ref[...] * w_scale
    ik = ik_ref[0:S, :]
    nt = (((1,), (1,)), ((), ()))
    sc = jnp.zeros((tq, S), F32)
    for hh in range(IDX_HEADS):
        s = lax.dot_general(iq_ref[:, hh * IDX_DIM:(hh + 1) * IDX_DIM], ik, nt,
                            preferred_element_type=F32)
        sc = sc + jnp.maximum(s, 0.0) * iw[:, hh:hh + 1]

    q_pos = i * tq + lax.broadcasted_iota(jnp.int32, (tq, S), 0)
    k_pos = lax.broadcasted_iota(jnp.int32, (tq, S), 1)
    causal = k_pos <= q_pos
    sc = jnp.where(sc == 0.0, 0.0, sc)
    sc = jnp.where(causal, sc, NEG_INF)
    bits = pltpu.bitcast(sc, jnp.int32)
    key_ref[:, 0:S] = bits ^ ((bits >> 31) & jnp.int32(0x7FFFFFFF))

    def value_step(b, t_u):
        cand = t_u | lax.shift_left(jnp.int32(1), 31 - b)
        cnt = _row_count(key_ref[:, 0:S] >= (cand ^ jnp.int32(INT_MIN)))
        return jnp.where(cnt >= n_sel, cand, t_u)

    def value_steps(bb, t_u):
        for r in range(SEARCH_UNROLL):
            t_u = value_step(bb * SEARCH_UNROLL + r, t_u)
        return t_u

    t_u = lax.fori_loop(0, 32 // SEARCH_UNROLL, value_steps, jnp.zeros((tq, 1), jnp.int32))
    thr = t_u ^ jnp.int32(INT_MIN)
    key = key_ref[:, 0:S]
    need = n_sel - _row_count(key > thr)
    tied = key == thr
    bias_ref[:, 0:S] = jnp.where((key >= thr) & causal, 0.0, NEG_INF)
    excess = _row_count(tied & causal) > need
    any_excess = jnp.max(jnp.where(excess, 1.0, 0.0)) > 0.0

    @pl.when(any_excess)
    def _():
        kp = lax.broadcasted_iota(jnp.int32, (tq, S), 1)
        qp = i * tq + lax.broadcasted_iota(jnp.int32, (tq, S), 0)
        bias_ref[:, 0:S] = jnp.where(key_ref[:, 0:S] == thr, kp.astype(F32), NOT_TIED)
        idx_bits = (S - 1).bit_length()

        j0 = jnp.zeros((tq, 1), jnp.int32)
        for b in range(idx_bits - 1, -1, -1):
            cand = j0 | jnp.int32(1 << b)
            cnt = _row_count(bias_ref[:, 0:S] < cand.astype(F32))
            j0 = jnp.where(cnt < need, cand, j0)
        admit = ((key_ref[:, 0:S] > thr) | (bias_ref[:, 0:S] <= j0.astype(F32))) & (kp <= qp)
        bias_ref[:, 0:S] = jnp.where(admit, 0.0, NEG_INF)

    for hh in range(N_HEADS):
        g = hh // (N_HEADS // N_KV_HEADS)
        kg = k_ref[0:S, g * HEAD_DIM:(g + 1) * HEAD_DIM]
        vg = v_ref[0:S, 2 * g * HEAD_DIM:2 * (g + 1) * HEAD_DIM]
        logits = lax.dot_general(q_ref[:, hh * HEAD_DIM:(hh + 1) * HEAD_DIM], kg, nt,
                                 preferred_element_type=F32) + bias_ref[:, 0:S]
        m = jnp.max(logits, axis=-1, keepdims=True)
        p = jnp.exp((logits - m).astype(BF16))
        ol = jnp.dot(p, vg, preferred_element_type=F32)
        o = ol[:, 0:HEAD_DIM] / ol[:, HEAD_DIM:HEAD_DIM + 1]
        o_ref[:, hh * HEAD_DIM:(hh + 1) * HEAD_DIM] = o.astype(BF16)


def _attn_kernel(*refs, tq, L, n_sel):
    i = pl.program_id(1)
    band = L // CAUSAL_BANDS
    for v in range(CAUSAL_BANDS):
        @pl.when((i * tq) // band == v)
        def _(v=v):
            _attn_tile(*refs, i=i, tq=tq, S=(v + 1) * band, n_sel=n_sel)


def _attn(q, k, v, iq, ik, iw, B, L, tq):
    T = B * L
    nq = L // tq
    n_sel = min(TOPK_MAX, L // 4)
    assert L % (CAUSAL_BANDS * tq) == 0 and L // CAUSAL_BANDS >= n_sel
    qrow = lambda b, i: (b * nq + i, 0)
    seq = lambda b, i: (b, 0)
    return pl.pallas_call(
        functools.partial(_attn_kernel, tq=tq, L=L, n_sel=n_sel),
        grid=(B, nq),
        in_specs=[
            pl.BlockSpec((tq, ATTN_WIDTH), qrow),
            pl.BlockSpec((L, KV_WIDTH), seq),
            pl.BlockSpec((L, 2 * KV_WIDTH), seq),
            pl.BlockSpec((tq, IDX_HEADS * IDX_DIM), qrow),
            pl.BlockSpec((L, IDX_DIM), seq),
            pl.BlockSpec((tq, IDX_HEADS), qrow),
        ],
        out_specs=pl.BlockSpec((tq, ATTN_WIDTH), qrow),
        out_shape=jax.ShapeDtypeStruct((T, ATTN_WIDTH), BF16),
        scratch_shapes=[pltpu.VMEM((tq, L), jnp.int32), pltpu.VMEM((tq, L), F32)],
        compiler_params=pltpu.CompilerParams(
            dimension_semantics=("parallel", "arbitrary"), vmem_limit_bytes=VMEM_LIMIT),
    )(q, k, v, iq, ik, iw)


def _outproj_kernel(x_ref, u_ref, halo_ref, a_ref, gin_ref, bin_ref, wpool_ref, pscale_ref,
                    wout_ref, g1_ref, b1_ref, wr_hi_ref, wr_lo_ref, br_ref,
                    h1_ref, h1b_ref, ids_ref, gates_ref, *, tq, tiles_per_seq):
    i = pl.program_id(0)
    seq_tile = i % tiles_per_seq
    u = u_ref[...]
    halo = jnp.where(seq_tile == 0, 0.0, halo_ref[...])
    ext = jnp.concatenate([halo, u], axis=0)
    lane = lax.broadcasted_iota(jnp.int32, (tq, POOL_WIDTH), 1)
    grp = lane // POOL_GDIM
    win = jnp.zeros((tq, POOL_WIDTH), F32)
    s = ext
    for gi, w in enumerate(POOL_WINDOWS):
        s = s + pltpu.roll(s, w // 2, 0)
        win = jnp.where(grp == gi, s[POOL_HALO:, :], win)
    t_seq = seq_tile * tq + lax.broadcasted_iota(jnp.int32, (tq, POOL_WIDTH), 0)
    width = lax.shift_left(jnp.int32(2), grp)
    cnt = jnp.minimum(t_seq + 1, width).astype(F32)
    d = win / cnt - u
    y_pool = jnp.dot(d.astype(BF16), wpool_ref[...], preferred_element_type=F32) * pscale_ref[...]

    mix = jnp.dot(y_pool.astype(BF16), wout_ref[0:POOL_WIDTH, :], preferred_element_type=F32)
    mix = mix + jnp.dot(a_ref[...], wout_ref[POOL_WIDTH:, :], preferred_element_type=F32)
    h = _layer_norm(x_ref[...], gin_ref[...], bin_ref[...])
    h1 = _layer_norm(ALPHA_RES * h + mix, g1_ref[...], b1_ref[...])
    h1_ref[...] = h1
    h1_hi = h1.astype(BF16)
    h1b_ref[...] = h1_hi
    h1_lo = (h1 - h1_hi.astype(F32)).astype(BF16)
    wr_hi = wr_hi_ref[...]
    logits = (jnp.dot(h1_hi, wr_hi, preferred_element_type=F32)
              + jnp.dot(h1_lo, wr_hi, preferred_element_type=F32)
              + jnp.dot(h1_hi, wr_lo_ref[...], preferred_element_type=F32)) + br_ref[...]
    lane_e = lax.broadcasted_iota(jnp.int32, (tq, LANES), 1)
    lg = jnp.where(lane_e < N_EXPERTS, logits, -jnp.inf)
    ids = jnp.zeros((tq, LANES), jnp.int32)
    ex = jnp.zeros((tq, LANES), F32)
    top0 = None
    for kk in range(TOP_K):
        m = jnp.max(lg, axis=-1, keepdims=True)
        idx = jnp.min(jnp.where(lg == m, lane_e, LANES), axis=-1, keepdims=True)
        if top0 is None:
            top0 = m
        ids = jnp.where(lane_e == kk, idx, ids)
        ex = jnp.where(lane_e == kk, jnp.exp(m - top0), ex)
        lg = jnp.where(lane_e == idx, -jnp.inf, lg)
    ids_ref[...] = ids
    gates_ref[...] = ex / jnp.sum(ex, axis=-1, keepdims=True)


def _outproj(x2, u, attn, gin, bin_, wpool_bd, pscale, wout, g1, b1, wr_hi, wr_lo, br, L, tq):
    T = x2.shape[0]
    row = lambda i: (i, 0)
    fixed = lambda i: (0, 0)
    halo = lambda i: (jnp.maximum(i * (tq // POOL_HALO) - 1, 0), 0)
    out_shapes = (
        jax.ShapeDtypeStruct((T, D_MODEL), F32),
        jax.ShapeDtypeStruct((T, D_MODEL), BF16),
        jax.ShapeDtypeStruct((T, LANES), jnp.int32),
        jax.ShapeDtypeStruct((T, LANES), F32),
    )
    return pl.pallas_call(
        functools.partial(_outproj_kernel, tq=tq, tiles_per_seq=L // tq),
        grid=(T // tq,),
        in_specs=[
            pl.BlockSpec((tq, D_MODEL), row),
            pl.BlockSpec((tq, POOL_WIDTH), row),
            pl.BlockSpec((POOL_HALO, POOL_WIDTH), halo),
            pl.BlockSpec((tq, ATTN_WIDTH), row),
            pl.BlockSpec((1, D_MODEL), fixed),
            pl.BlockSpec((1, D_MODEL), fixed),
            pl.BlockSpec((POOL_WIDTH, POOL_WIDTH), fixed),
            pl.BlockSpec((1, POOL_WIDTH), fixed),
            pl.BlockSpec((D_MODEL, D_MODEL), fixed),
            pl.BlockSpec((1, D_MODEL), fixed),
            pl.BlockSpec((1, D_MODEL), fixed),
            pl.BlockSpec((D_MODEL, LANES), fixed),
            pl.BlockSpec((D_MODEL, LANES), fixed),
            pl.BlockSpec((1, LANES), fixed),
        ],
        out_specs=tuple(pl.BlockSpec((tq, s.shape[1]), row) for s in out_shapes),
        out_shape=out_shapes,
        compiler_params=pltpu.CompilerParams(
            dimension_semantics=("parallel",), vmem_limit_bytes=VMEM_LIMIT),
    )(x2, u, u, attn, gin, bin_, wpool_bd, pscale, wout, g1, b1, wr_hi, wr_lo, br)


ROUTE_TB = 512
ROW_ALIGN = 16
RUN_BITS = 6
STAGE_ROWS = 2560
STAGE_CHUNK = 512
COMBINE_CHUNK = 1280
FFN_TM = 512
TAIL_BITS = 5


def _route_kernel(ids_ref, slot_ref, slot_t_ref, meta_ref, carry_ref):
    c = pl.program_id(0)

    @pl.when(c == 0)
    def _():
        carry_ref[...] = jnp.zeros(carry_ref.shape, F32)

    tb = ids_ref.shape[0]
    ids = ids_ref[...]
    lane = lax.broadcasted_iota(jnp.int32, (tb, LANES), 1)
    onehot = [jnp.where(lane == ids[:, k:k + 1], 1.0, 0.0) for k in range(TOP_K)]
    member = onehot[0] + onehot[1] + onehot[2] + onehot[3]
    r = lax.broadcasted_iota(jnp.int32, (tb, tb), 0)
    cc = lax.broadcasted_iota(jnp.int32, (tb, tb), 1)
    before = jnp.where(cc < r, 1.0, 0.0).astype(BF16)
    lrank = jnp.dot(before, member.astype(BF16), preferred_element_type=F32)
    n = jnp.sum(member, axis=0, keepdims=True)
    units = jnp.ceil(n * (1.0 / ROW_ALIGN))
    er = lax.broadcasted_iota(jnp.int32, (LANES, LANES), 0)
    ec = lax.broadcasted_iota(jnp.int32, (LANES, LANES), 1)
    lower = jnp.where(er < ec, 1.0, 0.0).astype(BF16)
    off_units = jnp.dot(jnp.broadcast_to(units, (8, LANES)).astype(BF16), lower,
                        preferred_element_type=F32)[0:1, :]
    base = off_units * ROW_ALIGN + lrank
    slot = jnp.zeros((tb, LANES), jnp.int32)
    for k in range(TOP_K):
        sk = jnp.sum(onehot[k] * base, axis=-1, keepdims=True).astype(jnp.int32)
        slot = jnp.where(lane == k, sk, slot)
    slot_ref[...] = slot
    slot_t_ref[...] = slot.T[0:8, :]
    row = lax.broadcasted_iota(jnp.int32, (8, LANES), 0)
    meta = jnp.where(row == 0, units, jnp.where(row == 1, off_units, carry_ref[...]))
    meta_ref[...] = meta.astype(jnp.int32)
    carry_ref[...] = carry_ref[...] + units


def _route(ids):
    T = ids.shape[0]
    nblk = T // ROUTE_TB
    return pl.pallas_call(
        _route_kernel,
        grid=(nblk,),
        in_specs=[pl.BlockSpec((ROUTE_TB, LANES), lambda c: (c, 0))],
        out_specs=(pl.BlockSpec((ROUTE_TB, LANES), lambda c: (c, 0)),
                   pl.BlockSpec((8, ROUTE_TB), lambda c: (0, c)),
                   pl.BlockSpec((8, LANES), lambda c: (c, 0))),
        out_shape=(jax.ShapeDtypeStruct((T, LANES), jnp.int32),
                   jax.ShapeDtypeStruct((8, T), jnp.int32),
                   jax.ShapeDtypeStruct((nblk * 8, LANES), jnp.int32)),
        scratch_shapes=[pltpu.VMEM((1, LANES), F32)],
        compiler_params=pltpu.CompilerParams(
            dimension_semantics=("arbitrary",), vmem_limit_bytes=VMEM_LIMIT),
    )(ids)


def _pieces_of(m, so, do, bits, visit):
    for b in bits:
        done = m & ((1 << b) - 1)

        @pl.when(((m >> b) & 1) == 1)
        def _(b=b, done=done):
            visit(b, pl.multiple_of((so + done) * ROW_ALIGN, ROW_ALIGN),
                  pl.multiple_of((do + done) * ROW_ALIGN, ROW_ALIGN), ROW_ALIGN << b)


def _piece_lists(units, off_units, dst_units):
    bits = jnp.arange(RUN_BITS, dtype=jnp.int32)[None, :, None]
    m = units[:, None, :]
    has = ((m >> bits) & 1) == 1
    done = m & ((1 << bits) - 1)
    pos = jnp.cumsum(has, axis=-1) - 1
    at = has[..., None] & (pos[..., None] == jnp.arange(N_EXPERTS, dtype=jnp.int32))
    pick = lambda v: jnp.sum(jnp.where(at, v[..., None], 0), axis=2).reshape(-1).astype(jnp.int32)
    cnt = jnp.sum(has, axis=-1).reshape(-1).astype(jnp.int32)
    return cnt, pick(off_units[:, None, :] + done), pick(dst_units[:, None, :] + done)


def _run_pieces(cnt_ref, srow_ref, grow_ref, c, visit):
    for b in range(RUN_BITS):
        base = (c * RUN_BITS + b) * N_EXPERTS

        def body(j, carry, b=b, base=base):
            visit(b, j, pl.multiple_of(srow_ref[base + j] * ROW_ALIGN, ROW_ALIGN),
                  pl.multiple_of(grow_ref[base + j] * ROW_ALIGN, ROW_ALIGN), ROW_ALIGN << b)
            return carry

        lax.fori_loop(0, cnt_ref[c * RUN_BITS + b], body, 0)


def _dispatch_kernel(cnt_ref, srow_ref, grow_ref, tail_ref, tail_dst_ref, slot_t_ref, h_ref,
                     xs_ref, stage_ref, zero_ref, sems, *, nblk):
    c = pl.program_id(0)
    par = c % 2
    tb = h_ref.shape[0]
    for r0 in range(0, STAGE_ROWS, STAGE_CHUNK):
        srow = r0 + lax.broadcasted_iota(jnp.int32, (STAGE_CHUNK, tb), 0)
        hit = srow == slot_t_ref[0:1, :]
        for k in range(1, TOP_K):
            hit = hit | (srow == slot_t_ref[k:k + 1, :])
        perm = jnp.where(hit, 1.0, 0.0).astype(BF16)
        stage_ref[par, r0:r0 + STAGE_CHUNK, :] = jnp.dot(
            perm, h_ref[...], preferred_element_type=F32).astype(BF16)

    def piece(p):
        return lambda b, j, s_row, g_row, rows: pltpu.make_async_copy(
            stage_ref.at[p, pl.ds(s_row, rows), :], xs_ref.at[pl.ds(g_row, rows), :],
            sems.at[p, j, b])

    _run_pieces(cnt_ref, srow_ref, grow_ref, c, lambda *a: piece(par)(*a).start())

    @pl.when(c > 0)
    def _():
        _run_pieces(cnt_ref, srow_ref, grow_ref, c - 1, lambda *a: piece(1 - par)(*a).wait())

    @pl.when(c == nblk - 1)
    def _():
        _run_pieces(cnt_ref, srow_ref, grow_ref, c, lambda *a: piece(par)(*a).wait())
        zero_ref[...] = jnp.zeros(zero_ref.shape, BF16)

        def tail(e, b, z_row, g_row, rows):
            del z_row
            return pltpu.make_async_copy(zero_ref.at[pl.ds(0, rows), :],
                                         xs_ref.at[pl.ds(g_row, rows), :], sems.at[0, e, b])

        def tails(act):
            for e in range(N_EXPERTS):
                _pieces_of(tail_ref[e], 0, tail_dst_ref[e], range(TAIL_BITS),
                           lambda b, z, g, rows, e=e: act(tail(e, b, z, g, rows)))

        tails(lambda cp: cp.start())
        tails(lambda cp: cp.wait())

        zrows = zero_ref.shape[0]
        used = (tail_dst_ref[N_EXPERTS - 1] + tail_ref[N_EXPERTS - 1]) * ROW_ALIGN
        n_rest = (xs_ref.shape[0] - used) // zrows

        def rest(i):
            row0 = pl.multiple_of(used + i * zrows, ROW_ALIGN)
            return pltpu.make_async_copy(
                zero_ref, xs_ref.at[pl.ds(row0, zrows), :],
                sems.at[i // (N_EXPERTS * RUN_BITS), (i // RUN_BITS) % N_EXPERTS, i % RUN_BITS])

        lax.fori_loop(0, n_rest, lambda i, _: (rest(i).start(), 0)[1], 0)
        lax.fori_loop(0, n_rest, lambda i, _: (rest(i).wait(), 0)[1], 0)


def _dispatch(units, off, dst, tail, tail_dst, slot_t, h1b, n_rows):
    T = h1b.shape[0]
    nblk = T // ROUTE_TB
    grid_spec = pltpu.PrefetchScalarGridSpec(
        num_scalar_prefetch=5,
        grid=(nblk,),
        in_specs=[pl.BlockSpec((8, ROUTE_TB), lambda c, *_: (0, c)),
                  pl.BlockSpec((ROUTE_TB, D_MODEL), lambda c, *_: (c, 0))],
        out_specs=pl.BlockSpec(memory_space=pl.ANY),
        scratch_shapes=[pltpu.VMEM((2, STAGE_ROWS, D_MODEL), BF16),
                        pltpu.VMEM((ROW_ALIGN << (TAIL_BITS - 1), D_MODEL), BF16),
                        pltpu.SemaphoreType.DMA((2, N_EXPERTS, RUN_BITS))],
    )
    return pl.pallas_call(
        functools.partial(_dispatch_kernel, nblk=nblk),
        grid_spec=grid_spec,
        out_shape=jax.ShapeDtypeStruct((n_rows, D_MODEL), BF16),
        compiler_params=pltpu.CompilerParams(
            dimension_semantics=("arbitrary",), vmem_limit_bytes=VMEM_LIMIT),
    )(units, off, dst, tail, tail_dst, slot_t, h1b)


def _ffn_kernel(tile_e_ref, n_tiles_ref, xs_ref, wgu_ref, bgu_ref, wd_ref, bd_ref, y_ref,
                wgu_b, wd_b, *, n_chunk):
    j = pl.program_id(0)
    active = j < n_tiles_ref[0]
    new_expert = (j == 0) | (tile_e_ref[j] != tile_e_ref[jnp.maximum(j - 1, 0)])

    @pl.when(active & new_expert)
    def _():
        wgu_b[...] = wgu_ref[0].astype(BF16)
        wd_b[...] = wd_ref[0].astype(BF16)

    @pl.when(active)
    def _():
        xs = xs_ref[...]
        cw = D_FF // n_chunk
        acc = jnp.zeros(y_ref.shape, F32)
        for c in range(n_chunk):
            gate = jnp.dot(xs, wgu_b[:, c * cw:(c + 1) * cw], preferred_element_type=F32)
            gate = gate + bgu_ref[0, :, c * cw:(c + 1) * cw]
            up = jnp.dot(xs, wgu_b[:, D_FF + c * cw:D_FF + (c + 1) * cw],
                         preferred_element_type=F32)
            up = up + bgu_ref[0, :, D_FF + c * cw:D_FF + (c + 1) * cw]
            gate = jnp.minimum(gate, SWIGLU_LIMIT)
            up = jnp.clip(up, -SWIGLU_LIMIT, SWIGLU_LIMIT)
            act = (up + 1.0) * gate * jax.nn.sigmoid(SWIGLU_ALPHA * gate)
            acc = acc + jnp.dot(act.astype(BF16), wd_b[c * cw:(c + 1) * cw, :],
                                preferred_element_type=F32)
        y_ref[...] = (acc + bd_ref[0]).astype(BF16)

    @pl.when(jnp.logical_not(active))
    def _():
        y_ref[...] = jnp.zeros(y_ref.shape, BF16)


def _ffn(tile_e, n_tiles, xs, wgu, bgu, wd, bd):
    n_rows = xs.shape[0]
    last = lambda j, nt: jnp.minimum(j, nt[0] - 1)
    row = lambda j, te, nt: (last(j, nt), 0)
    exp3 = lambda j, te, nt: (te[last(j, nt)], 0, 0)
    grid_spec = pltpu.PrefetchScalarGridSpec(
        num_scalar_prefetch=2,
        grid=(n_rows // FFN_TM,),
        in_specs=[
            pl.BlockSpec((FFN_TM, D_MODEL), row),
            pl.BlockSpec((1, D_MODEL, 2 * D_FF), exp3),
            pl.BlockSpec((1, 1, 2 * D_FF), exp3),
            pl.BlockSpec((1, D_FF, D_MODEL), exp3),
            pl.BlockSpec((1, 1, D_MODEL), exp3),
        ],
        out_specs=pl.BlockSpec((FFN_TM, D_MODEL), lambda j, te, nt: (j, 0)),
        scratch_shapes=[pltpu.VMEM((D_MODEL, 2 * D_FF), BF16), pltpu.VMEM((D_FF, D_MODEL), BF16)],
    )
    return pl.pallas_call(
        functools.partial(_ffn_kernel, n_chunk=2),
        grid_spec=grid_spec,
        out_shape=jax.ShapeDtypeStruct((n_rows, D_MODEL), BF16),
        compiler_params=pltpu.CompilerParams(
            dimension_semantics=("arbitrary",), vmem_limit_bytes=FFN_VMEM_LIMIT),
    )(tile_e, n_tiles, xs, wgu, bgu, wd, bd)


def _combine_kernel(cnt_ref, srow_ref, grow_ref, slot_ref, gates_ref, h1_ref, ys_ref,
                    g_ref, b_ref, o_ref, stage_ref, sems, *, nblk):
    c = pl.program_id(0)
    par = c % 2
    tb = h1_ref.shape[0]

    def piece(p):
        return lambda b, j, s_row, g_row, rows: pltpu.make_async_copy(
            ys_ref.at[pl.ds(g_row, rows), :], stage_ref.at[p, pl.ds(s_row, rows), :],
            sems.at[p, j, b])

    @pl.when(c == 0)
    def _():
        stage_ref[...] = jnp.zeros(stage_ref.shape, BF16)
        _run_pieces(cnt_ref, srow_ref, grow_ref, c, lambda *a: piece(0)(*a).start())

    _run_pieces(cnt_ref, srow_ref, grow_ref, c, lambda *a: piece(par)(*a).wait())

    @pl.when(c + 1 < nblk)
    def _():
        _run_pieces(cnt_ref, srow_ref, grow_ref, c + 1, lambda *a: piece(1 - par)(*a).start())

    slot = slot_ref[...]
    gates = gates_ref[...]
    ffn = jnp.zeros((tb, D_MODEL), F32)
    for r0 in range(0, STAGE_ROWS, COMBINE_CHUNK):
        scol = r0 + lax.broadcasted_iota(jnp.int32, (tb, COMBINE_CHUNK), 1)
        w = jnp.zeros((tb, COMBINE_CHUNK), F32)
        for k in range(TOP_K):
            w = w + jnp.where(scol == slot[:, k:k + 1], gates[:, k:k + 1], 0.0)
        ffn = ffn + jnp.dot(w.astype(BF16), stage_ref[par, r0:r0 + COMBINE_CHUNK, :],
                            preferred_element_type=F32)
    o_ref[...] = _layer_norm(ALPHA_RES * h1_ref[...] + ffn, g_ref[...], b_ref[...])


def _combine(units, off, dst, slot, gates, h1, ys, g, b):
    T = h1.shape[0]
    nblk = T // ROUTE_TB
    blk = lambda c, *_: (c, 0)
    fixed = lambda c, *_: (0, 0)
    grid_spec = pltpu.PrefetchScalarGridSpec(
        num_scalar_prefetch=3,
        grid=(nblk,),
        in_specs=[pl.BlockSpec((ROUTE_TB, LANES), blk),
                  pl.BlockSpec((ROUTE_TB, LANES), blk),
                  pl.BlockSpec((ROUTE_TB, D_MODEL), blk),
                  pl.BlockSpec(memory_space=pl.ANY),
                  pl.BlockSpec((1, D_MODEL), fixed),
                  pl.BlockSpec((1, D_MODEL), fixed)],
        out_specs=pl.BlockSpec((ROUTE_TB, D_MODEL), blk),
        scratch_shapes=[pltpu.VMEM((2, STAGE_ROWS, D_MODEL), BF16),
                        pltpu.SemaphoreType.DMA((2, N_EXPERTS, RUN_BITS))],
    )
    return pl.pallas_call(
        functools.partial(_combine_kernel, nblk=nblk),
        grid_spec=grid_spec,
        out_shape=jax.ShapeDtypeStruct((T, D_MODEL), F32),
        compiler_params=pltpu.CompilerParams(
            dimension_semantics=("arbitrary",), vmem_limit_bytes=VMEM_LIMIT),
    )(units, off, dst, slot, gates, h1, ys, g, b)


def kernel(x, positions, ln_in_g, ln_in_b, w_in, w_pool, pool_scale, w_out, ln1_g, ln1_b,
           w_router, b_router, w_gate_up, b_gate_up, w_down, b_down, ln2_g, ln2_b):
    B, L, D = x.shape
    T = B * L
    assert T % ROUTE_TB == 0 and D == D_MODEL
    tq = min(256, L)
    x2 = x.reshape(T, D)
    pos2 = positions.reshape(T, 1)
    gin = ln_in_g.reshape(1, D)
    bin_ = ln_in_b.reshape(1, D)

    w_pad = jnp.pad(w_in[0], ((0, 0), (0, IN_WIDTH_PAD - IN_WIDTH))).astype(BF16)
    u, q, k, v, iq, ik, iw = _inproj(x2, pos2, gin, bin_, w_pad, tq)

    attn = _attn(q, k, v, iq, ik, iw, B, L, min(256, L // CAUSAL_BANDS))

    wpool_bd = jnp.zeros((POOL_WIDTH, POOL_WIDTH), F32)
    for gi in range(POOL_GROUPS):
        sl = slice(gi * POOL_GDIM, (gi + 1) * POOL_GDIM)
        wpool_bd = wpool_bd.at[sl, sl].set(w_pool[0, gi])
    wr = jnp.pad(w_router[0], ((0, 0), (0, LANES - N_EXPERTS)))
    wr_hi = wr.astype(BF16)
    wr_lo = (wr - wr_hi.astype(F32)).astype(BF16)
    br = jnp.pad(b_router[0], (0, LANES - N_EXPERTS)).reshape(1, LANES)
    h1, h1b, ids, gates = _outproj(
        x2, u, attn, gin, bin_, wpool_bd.astype(BF16), pool_scale[0].reshape(1, POOL_WIDTH),
        w_out[0].astype(BF16), ln1_g[0].reshape(1, D), ln1_b[0].reshape(1, D),
        wr_hi, wr_lo, br, L, tq)

    nblk = T // ROUTE_TB
    slot, slot_t, meta = _route(ids)
    meta = meta.reshape(nblk, 8, LANES)[:, :, :N_EXPERTS]
    units, off_units, base_units = meta[:, 0], meta[:, 1], meta[:, 2]
    tile_units = FFN_TM // ROW_ALIGN
    total_units = base_units[-1] + units[-1]
    region_units = ((total_units + tile_units - 1) // tile_units) * tile_units
    region_end = jnp.cumsum(region_units)
    dst_units = (region_end - region_units)[None, :] + base_units
    max_rows = T * TOP_K + nblk * N_EXPERTS * (ROW_ALIGN - 1) + N_EXPERTS * (FFN_TM - 1)
    n_rows = -(-max_rows // FFN_TM) * FFN_TM
    tile_start = jnp.arange(n_rows // FFN_TM, dtype=jnp.int32) * tile_units
    tile_e = jnp.minimum(jnp.sum(tile_start[:, None] >= region_end[None, :], axis=1),
                         N_EXPERTS - 1).astype(jnp.int32)
    n_tiles = (region_end[-1:] // tile_units).astype(jnp.int32)
    cnt, srow, grow = _piece_lists(units, off_units, dst_units)
    tail_units = (region_units - total_units).astype(jnp.int32)
    tail_dst = (region_end - tail_units).astype(jnp.int32)

    xs = _dispatch(cnt, srow, grow, tail_units, tail_dst, slot_t, h1b, n_rows)
    ys = _ffn(tile_e, n_tiles, xs, w_gate_up[0], b_gate_up[0].reshape(N_EXPERTS, 1, 2 * D_FF),
              w_down[0], b_down[0].reshape(N_EXPERTS, 1, D))
    out = _combine(cnt, srow, grow, slot, gates, h1, ys,
                   ln2_g[0].reshape(1, D), ln2_b[0].reshape(1, D))
    return out.reshape(B, L, D)
```

```python
import functools

import jax
import jax.numpy as jnp
import numpy as np
from jax import lax
from jax.experimental import pallas as pl
from jax.experimental.pallas import tpu as pltpu

F32 = jnp.float32
BF16 = jnp.bfloat16

D_MODEL = 1024
POOL_WIDTH = 256
POOL_GROUPS = 4
POOL_GDIM = 64
POOL_WINDOWS = (2, 4, 8, 16)
POOL_HALO = 16
HEAD_DIM = 128
ATTN_WIDTH = 768
N_HEADS = 6
N_KV_HEADS = 2
KV_WIDTH = 256
IDX_HEADS = 8
IDX_DIM = 64
TOPK_MAX = 256
ROPE_THETA = 500000.0
ROPE_DIM = 32
IDX_ROPE_DIM = 16
N_EXPERTS = 32
TOP_K = 4
D_FF = 1024
SWIGLU_LIMIT = 7.0
SWIGLU_ALPHA = 1.702
DEPTH = 1
ALPHA_RES = (2.0 * DEPTH) ** 0.25
LN_EPS = 1e-5
NEG_INF = -1e30
OFF_Q = 256
OFF_K = 1024
OFF_V = 1280
OFF_IQ = 1536
OFF_IK = 2048
IN_WIDTH = 2120
IN_WIDTH_PAD = 2176

LANES = 128
VMEM_LIMIT = 48 * 1024 * 1024
FFN_VMEM_LIMIT = 56 * 1024 * 1024
INT_MIN = -2 ** 31
NOT_TIED = 2 ** 30
CAUSAL_BANDS = 8
SEARCH_UNROLL = 4


def _layer_norm(x, g, b):
    mu = jnp.mean(x, axis=-1, keepdims=True)
    xc = x - mu
    var = jnp.mean(xc * xc, axis=-1, keepdims=True)
    return xc * lax.rsqrt(var + LN_EPS) * g + b


def _rope(xh, cos, sin_lo, sin_hi, half):
    return (xh * cos + pltpu.roll(xh, LANES - half, 1) * sin_lo
            + pltpu.roll(xh, half, 1) * sin_hi)


def _inproj_kernel(x_ref, pos_ref, g_ref, b_ref, w_ref, tab_ref,
                   u_ref, q_ref, k_ref, v_ref, iq_ref, ik_ref, iw_ref):
    h = _layer_norm(x_ref[...], g_ref[...], b_ref[...])
    proj = jnp.dot(h.astype(BF16), w_ref[...], preferred_element_type=F32)
    pos = pos_ref[...].astype(F32)

    lane = lax.broadcasted_iota(jnp.int32, (x_ref.shape[0], LANES), 1)
    ang = pos * tab_ref[0:1, :]
    cos_a = jnp.cos(ang)
    sin_a = jnp.sin(ang)
    cos_q = jnp.where(lane < ROPE_DIM, cos_a, 1.0)
    sin_q_lo = sin_a * tab_ref[1:2, :]
    sin_q_hi = sin_a * tab_ref[2:3, :]
    in_first = lane < IDX_ROPE_DIM
    in_second = (lane >= IDX_DIM) & (lane < IDX_DIM + IDX_ROPE_DIM)
    cos_i = jnp.where(in_first, pltpu.roll(cos_a, LANES - ROPE_DIM, 1),
                      jnp.where(in_second, pltpu.roll(cos_a, ROPE_DIM, 1), 1.0))
    sin_i = jnp.where(lane < IDX_DIM, pltpu.roll(sin_a, LANES - ROPE_DIM, 1),
                      pltpu.roll(sin_a, ROPE_DIM, 1))
    sin_i_lo = sin_i * tab_ref[4:5, :]
    sin_i_hi = sin_i * tab_ref[5:6, :]

    u_ref[...] = proj[:, 0:OFF_Q]
    scale = HEAD_DIM ** -0.5
    for hh in range(N_HEADS):
        c0 = OFF_Q + hh * HEAD_DIM
        r = _rope(proj[:, c0:c0 + HEAD_DIM], cos_q, sin_q_lo, sin_q_hi, ROPE_DIM // 2)
        q_ref[:, hh * HEAD_DIM:(hh + 1) * HEAD_DIM] = (r * scale).astype(BF16)
    for hh in range(N_KV_HEADS):
        c0 = OFF_K + hh * HEAD_DIM
        r = _rope(proj[:, c0:c0 + HEAD_DIM], cos_q, sin_q_lo, sin_q_hi, ROPE_DIM // 2)
        k_ref[:, hh * HEAD_DIM:(hh + 1) * HEAD_DIM] = r.astype(BF16)
    ones = jnp.ones((x_ref.shape[0], HEAD_DIM), BF16)
    for hh in range(N_KV_HEADS):
        c0 = OFF_V + hh * HEAD_DIM
        v_ref[:, 2 * hh * HEAD_DIM:(2 * hh + 1) * HEAD_DIM] = proj[:, c0:c0 + HEAD_DIM].astype(BF16)
        v_ref[:, (2 * hh + 1) * HEAD_DIM:(2 * hh + 2) * HEAD_DIM] = ones
    for t in range(IDX_HEADS * IDX_DIM // LANES):
        c0 = OFF_IQ + t * LANES
        r = _rope(proj[:, c0:c0 + LANES], cos_i, sin_i_lo, sin_i_hi, IDX_ROPE_DIM // 2)
        iq_ref[:, t * LANES:(t + 1) * LANES] = r.astype(BF16)
    tail = proj[:, OFF_IK:OFF_IK + LANES]
    r = _rope(tail, cos_i, sin_i_lo, sin_i_hi, IDX_ROPE_DIM // 2)
    ik_ref[...] = r[:, 0:IDX_DIM].astype(BF16)
    iw_ref[...] = tail[:, IDX_DIM:IDX_DIM + IDX_HEADS]


def _rope_tables():
    lane = np.arange(LANES)
    tab = np.zeros((8, LANES), np.float32)
    f_q = ROPE_THETA ** (-jnp.arange(0, ROPE_DIM, 2, dtype=F32) / ROPE_DIM)
    f_i = ROPE_THETA ** (-jnp.arange(0, IDX_ROPE_DIM, 2, dtype=F32) / IDX_ROPE_DIM)
    hq, hi = ROPE_DIM // 2, IDX_ROPE_DIM // 2
    in_q = lane < ROPE_DIM
    li = lane % IDX_DIM
    in_i = li < IDX_ROPE_DIM
    tab[1] = np.where(lane < hq, -1.0, 0.0)
    tab[2] = np.where(in_q & (lane >= hq), 1.0, 0.0)
    tab[4] = np.where(li < hi, -1.0, 0.0)
    tab[5] = np.where(in_i & (li >= hi), 1.0, 0.0)
    tab = jnp.asarray(tab)
    in_i_slot = (lane >= ROPE_DIM) & (lane < ROPE_DIM + IDX_ROPE_DIM)
    freq = jnp.where(in_q, f_q[lane % hq], jnp.where(in_i_slot, f_i[lane % hi], 0.0))
    return tab.at[0].set(freq)


def _inproj(x2, pos2, g, b, w_pad, tq):
    T = x2.shape[0]
    row = lambda i: (i, 0)
    fixed = lambda i: (0, 0)
    out_shapes = (
        jax.ShapeDtypeStruct((T, POOL_WIDTH), F32),
        jax.ShapeDtypeStruct((T, ATTN_WIDTH), BF16),
        jax.ShapeDtypeStruct((T, KV_WIDTH), BF16),
        jax.ShapeDtypeStruct((T, 2 * KV_WIDTH), BF16),
        jax.ShapeDtypeStruct((T, IDX_HEADS * IDX_DIM), BF16),
        jax.ShapeDtypeStruct((T, IDX_DIM), BF16),
        jax.ShapeDtypeStruct((T, IDX_HEADS), F32),
    )
    return pl.pallas_call(
        _inproj_kernel,
        grid=(T // tq,),
        in_specs=[
            pl.BlockSpec((tq, D_MODEL), row),
            pl.BlockSpec((tq, 1), row),
            pl.BlockSpec((1, D_MODEL), fixed),
            pl.BlockSpec((1, D_MODEL), fixed),
            pl.BlockSpec((D_MODEL, IN_WIDTH_PAD), fixed),
            pl.BlockSpec((8, LANES), fixed),
        ],
        out_specs=tuple(pl.BlockSpec((tq, s.shape[1]), row) for s in out_shapes),
        out_shape=out_shapes,
        compiler_params=pltpu.CompilerParams(
            dimension_semantics=("parallel",), vmem_limit_bytes=VMEM_LIMIT),
    )(x2, pos2, g, b, w_pad, _rope_tables())


def _row_count(mask):
    return jnp.sum(jnp.where(mask, 1.0, 0.0), axis=-1, keepdims=True)


def _select_bias(iq_ref, ik_ref, iw_ref, key_ref, bias_ref, *, i, tq, S, n_sel):
    q_pos = i * tq + lax.broadcasted_iota(jnp.int32, (tq, S), 0)
    k_pos = lax.broadcasted_iota(jnp.int32, (tq, S), 1)
    causal = k_pos <= q_pos
    if S <= n_sel:
        bias_ref[:, 0:S] = jnp.where(causal, 0.0, NEG_INF)
        return

    w_scale = (IDX_HEADS ** -0.5) * (IDX_DIM ** -0.5)
    iw = iw_ref[...] * w_scale
    ik = ik_ref[0:S, :]
    nt = (((1,), (1,)), ((), ()))
    sc = jnp.zeros((tq, S), F32)
    for hh in range(IDX_HEADS):
        s = lax.dot_general(iq_ref[:, hh * IDX_DIM:(hh + 1) * IDX_DIM], ik, nt,
                            preferred_element_type=F32)
        sc = sc + jnp.maximum(s, 0.0) * iw[:, hh:hh + 1]
    sc = jnp.where(sc == 0.0, 0.0, sc)
    sc = jnp.where(causal, sc, NEG_INF)
    bits = pltpu.bitcast(sc, jnp.int32)
    key_ref[:, 0:S] = bits ^ ((bits >> 31) & jnp.int32(0x7FFFFFFF))

    def value_step(b, t_u):
        cand = t_u | lax.shift_left(jnp.int32(1), 31 - b)
        cnt = _row_count(key_ref[:, 0:S] >= (cand ^ jnp.int32(INT_MIN)))
        return jnp.where(cnt >= n_sel, cand, t_u)

    def value_steps(bb, t_u):
        for r in range(SEARCH_UNROLL):
            t_u = value_step(bb * SEARCH_UNROLL + r, t_u)
        return t_u

    t_u = lax.fori_loop(0, 32 // SEARCH_UNROLL, value_steps, jnp.zeros((tq, 1), jnp.int32))
    thr = t_u ^ jnp.int32(INT_MIN)
    key = key_ref[:, 0:S]
    need = n_sel - _row_count(key > thr)
    tied = key == thr
    bias_ref[:, 0:S] = jnp.where((key >= thr) & causal, 0.0, NEG_INF)
    excess = _row_count(tied & causal) > need
    any_excess = jnp.max(jnp.where(excess, 1.0, 0.0)) > 0.0

    @pl.when(any_excess)
    def _():
        kp = lax.broadcasted_iota(jnp.int32, (tq, S), 1)
        qp = i * tq + lax.broadcasted_iota(jnp.int32, (tq, S), 0)
        bias_ref[:, 0:S] = jnp.where(key_ref[:, 0:S] == thr, kp.astype(F32), NOT_TIED)
        idx_bits = (S - 1).bit_length()

        j0 = jnp.zeros((tq, 1), jnp.int32)
        for b in range(idx_bits - 1, -1, -1):
            cand = j0 | jnp.int32(1 << b)
            cnt = _row_count(bias_ref[:, 0:S] < cand.astype(F32))
            j0 = jnp.where(cnt < need, cand, j0)
        admit = ((key_ref[:, 0:S] > thr) | (bias_ref[:, 0:S] <= j0.astype(F32))) & (kp <= qp)
        bias_ref[:, 0:S] = jnp.where(admit, 0.0, NEG_INF)


def _attn_tile(q_ref, k_ref, v_ref, iq_ref, ik_ref, iw_ref, o_ref, key_ref, bias_ref,
               *, i, tq, S, n_sel):
    _select_bias(iq_ref, ik_ref, iw_ref, key_ref, bias_ref, i=i, tq=tq, S=S, n_sel=n_sel)
    nt = (((1,), (1,)), ((), ()))
    for hh in range(N_HEADS):
        g = hh // (N_HEADS // N_KV_HEADS)
        kg = k_ref[0:S, g * HEAD_DIM:(g + 1) * HEAD_DIM]
        vg = v_ref[0:S, 2 * g * HEAD_DIM:2 * (g + 1) * HEAD_DIM]
        logits = lax.dot_general(q_ref[:, hh * HEAD_DIM:(hh + 1) * HEAD_DIM], kg, nt,
                                 preferred_element_type=F32) + bias_ref[:, 0:S]
        m = jnp.max(logits, axis=-1, keepdims=True)
        p = jnp.exp((logits - m).astype(BF16))
        ol = jnp.dot(p, vg, preferred_element_type=F32)
        o = ol[:, 0:HEAD_DIM] / ol[:, HEAD_DIM:HEAD_DIM + 1]
        o_ref[:, hh * HEAD_DIM:(hh + 1) * HEAD_DIM] = o.astype(BF16)


def _attn_kernel(*refs, tq, L, n_sel):
    i = pl.program_id(1)
    band = L // CAUSAL_BANDS
    for v in range(CAUSAL_BANDS):
        @pl.when((i * tq) // band == v)
        def _(v=v):
            _attn_tile(*refs, i=i, tq=tq, S=(v + 1) * band, n_sel=n_sel)


def _attn(q, k, v, iq, ik, iw, B, L, tq):
    T = B * L
    nq = L // tq
    n_sel = min(TOPK_MAX, L // 4)
    assert L % (CAUSAL_BANDS * tq) == 0
    qrow = lambda b, i: (b * nq + i, 0)
    seq = lambda b, i: (b, 0)
    return pl.pallas_call(
        functools.partial(_attn_kernel, tq=tq, L=L, n_sel=n_sel),
        grid=(B, nq),
        in_specs=[
            pl.BlockSpec((tq, ATTN_WIDTH), qrow),
            pl.BlockSpec((L, KV_WIDTH), seq),
            pl.BlockSpec((L, 2 * KV_WIDTH), seq),
            pl.BlockSpec((tq, IDX_HEADS * IDX_DIM), qrow),
            pl.BlockSpec((L, IDX_DIM), seq),
            pl.BlockSpec((tq, IDX_HEADS), qrow),
        ],
        out_specs=pl.BlockSpec((tq, ATTN_WIDTH), qrow),
        out_shape=jax.ShapeDtypeStruct((T, ATTN_WIDTH), BF16),
        scratch_shapes=[pltpu.VMEM((tq, L), jnp.int32), pltpu.VMEM((tq, L), F32)],
        compiler_params=pltpu.CompilerParams(
            dimension_semantics=("parallel", "arbitrary"), vmem_limit_bytes=VMEM_LIMIT),
    )(q, k, v, iq, ik, iw)


def _outproj_kernel(x_ref, u_ref, halo_ref, a_ref, gin_ref, bin_ref, wpool_ref, pscale_ref,
                    wout_ref, g1_ref, b1_ref, wr_hi_ref, wr_lo_ref, br_ref,
                    h1_ref, h1b_ref, ids_ref, gates_ref, *, tq, tiles_per_seq):
    i = pl.program_id(0)
    seq_tile = i % tiles_per_seq
    u = u_ref[...]
    halo = jnp.where(seq_tile == 0, 0.0, halo_ref[...])
    ext = jnp.concatenate([halo, u], axis=0)
    lane = lax.broadcasted_iota(jnp.int32, (tq, POOL_WIDTH), 1)
    grp = lane // POOL_GDIM
    win = jnp.zeros((tq, POOL_WIDTH), F32)
    s = ext
    for gi, w in enumerate(POOL_WINDOWS):
        s = s + pltpu.roll(s, w // 2, 0)
        win = jnp.where(grp == gi, s[POOL_HALO:, :], win)
    t_seq = seq_tile * tq + lax.broadcasted_iota(jnp.int32, (tq, POOL_WIDTH), 0)
    width = lax.shift_left(jnp.int32(2), grp)
    cnt = jnp.minimum(t_seq + 1, width).astype(F32)
    d = win / cnt - u
    y_pool = jnp.dot(d.astype(BF16), wpool_ref[...], preferred_element_type=F32) * pscale_ref[...]

    mix = jnp.dot(y_pool.astype(BF16), wout_ref[0:POOL_WIDTH, :], preferred_element_type=F32)
    mix = mix + jnp.dot(a_ref[...], wout_ref[POOL_WIDTH:, :], preferred_element_type=F32)
    h = _layer_norm(x_ref[...], gin_ref[...], bin_ref[...])
    h1 = _layer_norm(ALPHA_RES * h + mix, g1_ref[...], b1_ref[...])
    h1_ref[...] = h1
    h1_hi = h1.astype(BF16)
    h1b_ref[...] = h1_hi
    h1_lo = (h1 - h1_hi.astype(F32)).astype(BF16)
    wr_hi = wr_hi_ref[...]
    logits = (jnp.dot(h1_hi, wr_hi, preferred_element_type=F32)
              + jnp.dot(h1_lo, wr_hi, preferred_element_type=F32)
              + jnp.dot(h1_hi, wr_lo_ref[...], preferred_element_type=F32)) + br_ref[...]
    lane_e = lax.broadcasted_iota(jnp.int32, (tq, LANES), 1)
    lg = jnp.where(lane_e < N_EXPERTS, logits, -jnp.inf)
    ids = jnp.zeros((tq, LANES), jnp.int32)
    ex = jnp.zeros((tq, LANES), F32)
    top0 = None
    for kk in range(TOP_K):
        m = jnp.max(lg, axis=-1, keepdims=True)
        idx = jnp.min(jnp.where(lg == m, lane_e, LANES), axis=-1, keepdims=True)
        if top0 is None:
            top0 = m
        ids = jnp.where(lane_e == kk, idx, ids)
        ex = jnp.where(lane_e == kk, jnp.exp(m - top0), ex)
        lg = jnp.where(lane_e == idx, -jnp.inf, lg)
    ids_ref[...] = ids
    gates_ref[...] = ex / jnp.sum(ex, axis=-1, keepdims=True)


def _outproj(x2, u, attn, gin, bin_, wpool_bd, pscale, wout, g1, b1, wr_hi, wr_lo, br, L, tq):
    T = x2.shape[0]
    row = lambda i: (i, 0)
    fixed = lambda i: (0, 0)
    halo = lambda i: (jnp.maximum(i * (tq // POOL_HALO) - 1, 0), 0)
    out_shapes = (
        jax.ShapeDtypeStruct((T, D_MODEL), F32),
        jax.ShapeDtypeStruct((T, D_MODEL), BF16),
        jax.ShapeDtypeStruct((T, LANES), jnp.int32),
        jax.ShapeDtypeStruct((T, LANES), F32),
    )
    return pl.pallas_call(
        functools.partial(_outproj_kernel, tq=tq, tiles_per_seq=L // tq),
        grid=(T // tq,),
        in_specs=[
            pl.BlockSpec((tq, D_MODEL), row),
            pl.BlockSpec((tq, POOL_WIDTH), row),
            pl.BlockSpec((POOL_HALO, POOL_WIDTH), halo),
            pl.BlockSpec((tq, ATTN_WIDTH), row),
            pl.BlockSpec((1, D_MODEL), fixed),
            pl.BlockSpec((1, D_MODEL), fixed),
            pl.BlockSpec((POOL_WIDTH, POOL_WIDTH), fixed),
            pl.BlockSpec((1, POOL_WIDTH), fixed),
            pl.BlockSpec((D_MODEL, D_MODEL), fixed),
            pl.BlockSpec((1, D_MODEL), fixed),
            pl.BlockSpec((1, D_MODEL), fixed),
            pl.BlockSpec((D_MODEL, LANES), fixed),
            pl.BlockSpec((D_MODEL, LANES), fixed),
            pl.BlockSpec((1, LANES), fixed),
        ],
        out_specs=tuple(pl.BlockSpec((tq, s.shape[1]), row) for s in out_shapes),
        out_shape=out_shapes,
        compiler_params=pltpu.CompilerParams(
            dimension_semantics=("parallel",), vmem_limit_bytes=VMEM_LIMIT),
    )(x2, u, u, attn, gin, bin_, wpool_bd, pscale, wout, g1, b1, wr_hi, wr_lo, br)


ROUTE_TB = 512
ROW_ALIGN = 16
RUN_BITS = 6
STAGE_ROWS = 2560
STAGE_CHUNK = 512
COMBINE_CHUNK = 1280
FFN_TM = 512
TAIL_BITS = 5


def _route_kernel(ids_ref, slot_ref, slot_t_ref, meta_ref, carry_ref):
    c = pl.program_id(0)

    @pl.when(c == 0)
    def _():
        carry_ref[...] = jnp.zeros(carry_ref.shape, F32)

    tb = ids_ref.shape[0]
    ids = ids_ref[...]
    lane = lax.broadcasted_iota(jnp.int32, (tb, LANES), 1)
    onehot = [jnp.where(lane == ids[:, k:k + 1], 1.0, 0.0) for k in range(TOP_K)]
    member = onehot[0] + onehot[1] + onehot[2] + onehot[3]
    r = lax.broadcasted_iota(jnp.int32, (tb, tb), 0)
    cc = lax.broadcasted_iota(jnp.int32, (tb, tb), 1)
    before = jnp.where(cc < r, 1.0, 0.0).astype(BF16)
    lrank = jnp.dot(before, member.astype(BF16), preferred_element_type=F32)
    n = jnp.sum(member, axis=0, keepdims=True)
    units = jnp.ceil(n * (1.0 / ROW_ALIGN))
    er = lax.broadcasted_iota(jnp.int32, (LANES, LANES), 0)
    ec = lax.broadcasted_iota(jnp.int32, (LANES, LANES), 1)
    lower = jnp.where(er < ec, 1.0, 0.0).astype(BF16)
    off_units = jnp.dot(jnp.broadcast_to(units, (8, LANES)).astype(BF16), lower,
                        preferred_element_type=F32)[0:1, :]
    base = off_units * ROW_ALIGN + lrank
    slot = jnp.zeros((tb, LANES), jnp.int32)
    for k in range(TOP_K):
        sk = jnp.sum(onehot[k] * base, axis=-1, keepdims=True).astype(jnp.int32)
        slot = jnp.where(lane == k, sk, slot)
    slot_ref[...] = slot
    slot_t_ref[...] = slot.T[0:8, :]
    row = lax.broadcasted_iota(jnp.int32, (8, LANES), 0)
    meta = jnp.where(row == 0, units, jnp.where(row == 1, off_units, carry_ref[...]))
    meta_ref[...] = meta.astype(jnp.int32)
    carry_ref[...] = carry_ref[...] + units


def _route(ids):
    T = ids.shape[0]
    nblk = T // ROUTE_TB
    return pl.pallas_call(
        _route_kernel,
        grid=(nblk,),
        in_specs=[pl.BlockSpec((ROUTE_TB, LANES), lambda c: (c, 0))],
        out_specs=(pl.BlockSpec((ROUTE_TB, LANES), lambda c: (c, 0)),
                   pl.BlockSpec((8, ROUTE_TB), lambda c: (0, c)),
                   pl.BlockSpec((8, LANES), lambda c: (c, 0))),
        out_shape=(jax.ShapeDtypeStruct((T, LANES), jnp.int32),
                   jax.ShapeDtypeStruct((8, T), jnp.int32),
                   jax.ShapeDtypeStruct((nblk * 8, LANES), jnp.int32)),
        scratch_shapes=[pltpu.VMEM((1, LANES), F32)],
        compiler_params=pltpu.CompilerParams(
            dimension_semantics=("arbitrary",), vmem_limit_bytes=VMEM_LIMIT),
    )(ids)


def _pieces_of(m, so, do, bits, visit):
    for b in bits:
        done = m & ((1 << b) - 1)

        @pl.when(((m >> b) & 1) == 1)
        def _(b=b, done=done):
            visit(b, pl.multiple_of((so + done) * ROW_ALIGN, ROW_ALIGN),
                  pl.multiple_of((do + done) * ROW_ALIGN, ROW_ALIGN), ROW_ALIGN << b)


def _piece_lists(units, off_units, dst_units):
    bits = jnp.arange(RUN_BITS, dtype=jnp.int32)[None, :, None]
    m = units[:, None, :]
    has = ((m >> bits) & 1) == 1
    done = m & ((1 << bits) - 1)
    pos = jnp.cumsum(has, axis=-1) - 1
    at = has[..., None] & (pos[..., None] == jnp.arange(N_EXPERTS, dtype=jnp.int32))
    pick = lambda v: jnp.sum(jnp.where(at, v[..., None], 0), axis=2).reshape(-1).astype(jnp.int32)
    cnt = jnp.sum(has, axis=-1).reshape(-1).astype(jnp.int32)
    return cnt, pick(off_units[:, None, :] + done), pick(dst_units[:, None, :] + done)


def _run_pieces(cnt_ref, srow_ref, grow_ref, c, visit):
    for b in range(RUN_BITS):
        base = (c * RUN_BITS + b) * N_EXPERTS

        def body(j, carry, b=b, base=base):
            visit(b, j, pl.multiple_of(srow_ref[base + j] * ROW_ALIGN, ROW_ALIGN),
                  pl.multiple_of(grow_ref[base + j] * ROW_ALIGN, ROW_ALIGN), ROW_ALIGN << b)
            return carry

        lax.fori_loop(0, cnt_ref[c * RUN_BITS + b], body, 0)


def _dispatch_kernel(cnt_ref, srow_ref, grow_ref, tail_ref, tail_dst_ref, slot_t_ref, h_ref,
                     xs_ref, stage_ref, zero_ref, sems, *, nblk):
    c = pl.program_id(0)
    par = c % 2
    tb = h_ref.shape[0]
    for r0 in range(0, STAGE_ROWS, STAGE_CHUNK):
        srow = r0 + lax.broadcasted_iota(jnp.int32, (STAGE_CHUNK, tb), 0)
        hit = srow == slot_t_ref[0:1, :]
        for k in range(1, TOP_K):
            hit = hit | (srow == slot_t_ref[k:k + 1, :])
        perm = jnp.where(hit, 1.0, 0.0).astype(BF16)
        stage_ref[par, r0:r0 + STAGE_CHUNK, :] = jnp.dot(
            perm, h_ref[...], preferred_element_type=F32).astype(BF16)

    def piece(p):
        return lambda b, j, s_row, g_row, rows: pltpu.make_async_copy(
            stage_ref.at[p, pl.ds(s_row, rows), :], xs_ref.at[pl.ds(g_row, rows), :],
            sems.at[p, j, b])

    _run_pieces(cnt_ref, srow_ref, grow_ref, c, lambda *a: piece(par)(*a).start())

    @pl.when(c > 0)
    def _():
        _run_pieces(cnt_ref, srow_ref, grow_ref, c - 1, lambda *a: piece(1 - par)(*a).wait())

    @pl.when(c == nblk - 1)
    def _():
        _run_pieces(cnt_ref, srow_ref, grow_ref, c, lambda *a: piece(par)(*a).wait())
        zero_ref[...] = jnp.zeros(zero_ref.shape, BF16)

        def tail(e, b, z_row, g_row, rows):
            del z_row
            return pltpu.make_async_copy(zero_ref.at[pl.ds(0, rows), :],
                                         xs_ref.at[pl.ds(g_row, rows), :], sems.at[0, e, b])

        def tails(act):
            for e in range(N_EXPERTS):
                _pieces_of(tail_ref[e], 0, tail_dst_ref[e], range(TAIL_BITS),
                           lambda b, z, g, rows, e=e: act(tail(e, b, z, g, rows)))

        tails(lambda cp: cp.start())
        tails(lambda cp: cp.wait())

        zrows = zero_ref.shape[0]
        used = (tail_dst_ref[N_EXPERTS - 1] + tail_ref[N_EXPERTS - 1]) * ROW_ALIGN
        n_rest = (xs_ref.shape[0] - used) // zrows

        def rest(i):
            row0 = pl.multiple_of(used + i * zrows, ROW_ALIGN)
            return pltpu.make_async_copy(
                zero_ref, xs_ref.at[pl.ds(row0, zrows), :],
                sems.at[i // (N_EXPERTS * RUN_BITS), (i // RUN_BITS) % N_EXPERTS, i % RUN_BITS])

        lax.fori_loop(0, n_rest, lambda i, _: (rest(i).start(), 0)[1], 0)
        lax.fori_loop(0, n_rest, lambda i, _: (rest(i).wait(), 0)[1], 0)


def _dispatch(units, off, dst, tail, tail_dst, slot_t, h1b, n_rows):
    T = h1b.shape[0]
    nblk = T // ROUTE_TB
    grid_spec = pltpu.PrefetchScalarGridSpec(
        num_scalar_prefetch=5,
        grid=(nblk,),
        in_specs=[pl.BlockSpec((8, ROUTE_TB), lambda c, *_: (0, c)),
                  pl.BlockSpec((ROUTE_TB, D_MODEL), lambda c, *_: (c, 0))],
        out_specs=pl.BlockSpec(memory_space=pl.ANY),
        scratch_shapes=[pltpu.VMEM((2, STAGE_ROWS, D_MODEL), BF16),
                        pltpu.VMEM((ROW_ALIGN << (TAIL_BITS - 1), D_MODEL), BF16),
                        pltpu.SemaphoreType.DMA((2, N_EXPERTS, RUN_BITS))],
    )
    return pl.pallas_call(
        functools.partial(_dispatch_kernel, nblk=nblk),
        grid_spec=grid_spec,
        out_shape=jax.ShapeDtypeStruct((n_rows, D_MODEL), BF16),
        compiler_params=pltpu.CompilerParams(
            dimension_semantics=("arbitrary",), vmem_limit_bytes=VMEM_LIMIT),
    )(units, off, dst, tail, tail_dst, slot_t, h1b)


def _ffn_kernel(tile_e_ref, n_tiles_ref, xs_ref, wgu_ref, bgu_ref, wd_ref, bd_ref, y_ref,
                wgu_b, wd_b, *, n_chunk):
    j = pl.program_id(0)
    active = j < n_tiles_ref[0]
    new_expert = (j == 0) | (tile_e_ref[j] != tile_e_ref[jnp.maximum(j - 1, 0)])

    @pl.when(active & new_expert)
    def _():
        wgu_b[...] = wgu_ref[0].astype(BF16)
        wd_b[...] = wd_ref[0].astype(BF16)

    @pl.when(active)
    def _():
        xs = xs_ref[...]
        cw = D_FF // n_chunk
        acc = jnp.zeros(y_ref.shape, F32)
        for c in range(n_chunk):
            gate = jnp.dot(xs, wgu_b[:, c * cw:(c + 1) * cw], preferred_element_type=F32)
            gate = gate + bgu_ref[0, :, c * cw:(c + 1) * cw]
            up = jnp.dot(xs, wgu_b[:, D_FF + c * cw:D_FF + (c + 1) * cw],
                         preferred_element_type=F32)
            up = up + bgu_ref[0, :, D_FF + c * cw:D_FF + (c + 1) * cw]
            gate = jnp.minimum(gate, SWIGLU_LIMIT)
            up = jnp.clip(up, -SWIGLU_LIMIT, SWIGLU_LIMIT)
            act = (up + 1.0) * gate * jax.nn.sigmoid(SWIGLU_ALPHA * gate)
            acc = acc + jnp.dot(act.astype(BF16), wd_b[c * cw:(c + 1) * cw, :],
                                preferred_element_type=F32)
        y_ref[...] = (acc + bd_ref[0]).astype(BF16)

    @pl.when(jnp.logical_not(active))
    def _():
        y_ref[...] = jnp.zeros(y_ref.shape, BF16)


def _ffn(tile_e, n_tiles, xs, wgu, bgu, wd, bd):
    n_rows = xs.shape[0]
    last = lambda j, nt: jnp.minimum(j, nt[0] - 1)
    row = lambda j, te, nt: (last(j, nt), 0)
    exp3 = lambda j, te, nt: (te[last(j, nt)], 0, 0)
    grid_spec = pltpu.PrefetchScalarGridSpec(
        num_scalar_prefetch=2,
        grid=(n_rows // FFN_TM,),
        in_specs=[
            pl.BlockSpec((FFN_TM, D_MODEL), row),
            pl.BlockSpec((1, D_MODEL, 2 * D_FF), exp3),
            pl.BlockSpec((1, 1, 2 * D_FF), exp3),
            pl.BlockSpec((1, D_FF, D_MODEL), exp3),
            pl.BlockSpec((1, 1, D_MODEL), exp3),
        ],
        out_specs=pl.BlockSpec((FFN_TM, D_MODEL), lambda j, te, nt: (j, 0)),
        scratch_shapes=[pltpu.VMEM((D_MODEL, 2 * D_FF), BF16), pltpu.VMEM((D_FF, D_MODEL), BF16)],
    )
    return pl.pallas_call(
        functools.partial(_ffn_kernel, n_chunk=2),
        grid_spec=grid_spec,
        out_shape=jax.ShapeDtypeStruct((n_rows, D_MODEL), BF16),
        compiler_params=pltpu.CompilerParams(
            dimension_semantics=("arbitrary",), vmem_limit_bytes=FFN_VMEM_LIMIT),
    )(tile_e, n_tiles, xs, wgu, bgu, wd, bd)


def _combine_kernel(cnt_ref, srow_ref, grow_ref, slot_ref, gates_ref, h1_ref, ys_ref,
                    g_ref, b_ref, o_ref, stage_ref, sems, *, nblk):
    c = pl.program_id(0)
    par = c % 2
    tb = h1_ref.shape[0]

    def piece(p):
        return lambda b, j, s_row, g_row, rows: pltpu.make_async_copy(
            ys_ref.at[pl.ds(g_row, rows), :], stage_ref.at[p, pl.ds(s_row, rows), :],
            sems.at[p, j, b])

    @pl.when(c == 0)
    def _():
        stage_ref[...] = jnp.zeros(stage_ref.shape, BF16)
        _run_pieces(cnt_ref, srow_ref, grow_ref, c, lambda *a: piece(0)(*a).start())

    _run_pieces(cnt_ref, srow_ref, grow_ref, c, lambda *a: piece(par)(*a).wait())

    @pl.when(c + 1 < nblk)
    def _():
        _run_pieces(cnt_ref, srow_ref, grow_ref, c + 1, lambda *a: piece(1 - par)(*a).start())

    slot = slot_ref[...]
    gates = gates_ref[...]
    ffn = jnp.zeros((tb, D_MODEL), F32)
    for r0 in range(0, STAGE_ROWS, COMBINE_CHUNK):
        scol = r0 + lax.broadcasted_iota(jnp.int32, (tb, COMBINE_CHUNK), 1)
        w = jnp.zeros((tb, COMBINE_CHUNK), F32)
        for k in range(TOP_K):
            w = w + jnp.where(scol == slot[:, k:k + 1], gates[:, k:k + 1], 0.0)
        ffn = ffn + jnp.dot(w.astype(BF16), stage_ref[par, r0:r0 + COMBINE_CHUNK, :],
                            preferred_element_type=F32)
    o_ref[...] = _layer_norm(ALPHA_RES * h1_ref[...] + ffn, g_ref[...], b_ref[...])


def _combine(units, off, dst, slot, gates, h1, ys, g, b):
    T = h1.shape[0]
    nblk = T // ROUTE_TB
    blk = lambda c, *_: (c, 0)
    fixed = lambda c, *_: (0, 0)
    grid_spec = pltpu.PrefetchScalarGridSpec(
        num_scalar_prefetch=3,
        grid=(nblk,),
        in_specs=[pl.BlockSpec((ROUTE_TB, LANES), blk),
                  pl.BlockSpec((ROUTE_TB, LANES), blk),
                  pl.BlockSpec((ROUTE_TB, D_MODEL), blk),
                  pl.BlockSpec(memory_space=pl.ANY),
                  pl.BlockSpec((1, D_MODEL), fixed),
                  pl.BlockSpec((1, D_MODEL), fixed)],
        out_specs=pl.BlockSpec((ROUTE_TB, D_MODEL), blk),
        scratch_shapes=[pltpu.VMEM((2, STAGE_ROWS, D_MODEL), BF16),
                        pltpu.SemaphoreType.DMA((2, N_EXPERTS, RUN_BITS))],
    )
    return pl.pallas_call(
        functools.partial(_combine_kernel, nblk=nblk),
        grid_spec=grid_spec,
        out_shape=jax.ShapeDtypeStruct((T, D_MODEL), F32),
        compiler_params=pltpu.CompilerParams(
            dimension_semantics=("arbitrary",), vmem_limit_bytes=VMEM_LIMIT),
    )(units, off, dst, slot, gates, h1, ys, g, b)


def kernel(x, positions, ln_in_g, ln_in_b, w_in, w_pool, pool_scale, w_out, ln1_g, ln1_b,
           w_router, b_router, w_gate_up, b_gate_up, w_down, b_down, ln2_g, ln2_b):
    B, L, D = x.shape
    T = B * L
    assert T % ROUTE_TB == 0 and D == D_MODEL
    tq = min(256, L)
    x2 = x.reshape(T, D)
    pos2 = positions.reshape(T, 1)
    gin = ln_in_g.reshape(1, D)
    bin_ = ln_in_b.reshape(1, D)

    w_pad = jnp.pad(w_in[0], ((0, 0), (0, IN_WIDTH_PAD - IN_WIDTH))).astype(BF16)
    u, q, k, v, iq, ik, iw = _inproj(x2, pos2, gin, bin_, w_pad, tq)

    attn = _attn(q, k, v, iq, ik, iw, B, L, min(256, L // CAUSAL_BANDS))

    wpool_bd = jnp.zeros((POOL_WIDTH, POOL_WIDTH), F32)
    for gi in range(POOL_GROUPS):
        sl = slice(gi * POOL_GDIM, (gi + 1) * POOL_GDIM)
        wpool_bd = wpool_bd.at[sl, sl].set(w_pool[0, gi])
    wr = jnp.pad(w_router[0], ((0, 0), (0, LANES - N_EXPERTS)))
    wr_hi = wr.astype(BF16)
    wr_lo = (wr - wr_hi.astype(F32)).astype(BF16)
    br = jnp.pad(b_router[0], (0, LANES - N_EXPERTS)).reshape(1, LANES)
    h1, h1b, ids, gates = _outproj(
        x2, u, attn, gin, bin_, wpool_bd.astype(BF16), pool_scale[0].reshape(1, POOL_WIDTH),
        w_out[0].astype(BF16), ln1_g[0].reshape(1, D), ln1_b[0].reshape(1, D),
        wr_hi, wr_lo, br, L, tq)

    nblk = T // ROUTE_TB
    slot, slot_t, meta = _route(ids)
    meta = meta.reshape(nblk, 8, LANES)[:, :, :N_EXPERTS]
    units, off_units, base_units = meta[:, 0], meta[:, 1], meta[:, 2]
    tile_units = FFN_TM // ROW_ALIGN
    total_units = base_units[-1] + units[-1]
    region_units = ((total_units + tile_units - 1) // tile_units) * tile_units
    region_end = jnp.cumsum(region_units)
    dst_units = (region_end - region_units)[None, :] + base_units
    max_rows = T * TOP_K + nblk * N_EXPERTS * (ROW_ALIGN - 1) + N_EXPERTS * (FFN_TM - 1)
    n_rows = -(-max_rows // FFN_TM) * FFN_TM
    tile_start = jnp.arange(n_rows // FFN_TM, dtype=jnp.int32) * tile_units
    tile_e = jnp.minimum(jnp.sum(tile_start[:, None] >= region_end[None, :], axis=1),
                         N_EXPERTS - 1).astype(jnp.int32)
    n_tiles = (region_end[-1:] // tile_units).astype(jnp.int32)
    cnt, srow, grow = _piece_lists(units, off_units, dst_units)
    tail_units = (region_units - total_units).astype(jnp.int32)
    tail_dst = (region_end - tail_units).astype(jnp.int32)

    xs = _dispatch(cnt, srow, grow, tail_units, tail_dst, slot_t, h1b, n_rows)
    ys = _ffn(tile_e, n_tiles, xs, w_gate_up[0], b_gate_up[0].reshape(N_EXPERTS, 1, 2 * D_FF),
              w_down[0], b_down[0].reshape(N_EXPERTS, 1, D))
    out = _combine(cnt, srow, grow, slot, gates, h1, ys,
                   ln2_g[0].reshape(1, D), ln2_b[0].reshape(1, D))
    return out.reshape(B, L, D)
```

```python
import functools

import jax
import jax.numpy as jnp
from jax import lax
from jax.experimental import pallas as pl
from jax.experimental.pallas import tpu as pltpu

F32 = jnp.float32
BF16 = jnp.bfloat16

D_MODEL = 1024
POOL_WIDTH = 256
POOL_GROUPS = 4
POOL_GDIM = 64
POOL_WINDOWS = (2, 4, 8, 16)
POOL_HALO = 16
HEAD_DIM = 128
ATTN_WIDTH = 768
N_HEADS = 6
N_KV_HEADS = 2
KV_WIDTH = 256
IDX_HEADS = 8
IDX_DIM = 64
TOPK_MAX = 256
ROPE_THETA = 500000.0
ROPE_DIM = 32
IDX_ROPE_DIM = 16
N_EXPERTS = 32
TOP_K = 4
D_FF = 1024
SWIGLU_LIMIT = 7.0
SWIGLU_ALPHA = 1.702
DEPTH = 1
ALPHA_RES = (2.0 * DEPTH) ** 0.25
LN_EPS = 1e-5
NEG_INF = -1e30
OFF_Q = 256
OFF_K = 1024
OFF_V = 1280
OFF_IQ = 1536
OFF_IK = 2048
IN_WIDTH = 2120
IN_WIDTH_PAD = 2176

LANES = 128
VMEM_LIMIT = 48 * 1024 * 1024
FFN_VMEM_LIMIT = 56 * 1024 * 1024
INT_MIN = -2 ** 31
NOT_TIED = 2 ** 30
CAUSAL_BANDS = 4
SEARCH_UNROLL = 4


def _layer_norm(x, g, b):
    mu = jnp.mean(x, axis=-1, keepdims=True)
    xc = x - mu
    var = jnp.mean(xc * xc, axis=-1, keepdims=True)
    return xc * lax.rsqrt(var + LN_EPS) * g + b


def _rope(xh, cos, sin, first_half, half):
    partner = jnp.where(first_half, pltpu.roll(xh, LANES - half, 1), pltpu.roll(xh, half, 1))
    return xh * cos + partner * sin


def _inproj_kernel(x_ref, pos_ref, g_ref, b_ref, w_ref, freq_ref,
                   u_ref, q_ref, k_ref, v_ref, iq_ref, ik_ref, iw_ref):
    tq = x_ref.shape[0]
    h = _layer_norm(x_ref[...], g_ref[...], b_ref[...])
    proj = jnp.dot(h.astype(BF16), w_ref[...], preferred_element_type=F32)

    pos = pos_ref[...].astype(F32)
    ang_q = freq_ref[0:ROPE_DIM // 2, :] * pos
    ang_i = freq_ref[ROPE_DIM // 2:ROPE_DIM // 2 + IDX_ROPE_DIM // 2, :] * pos
    cq, sq = jnp.cos(ang_q), jnp.sin(ang_q)
    ci, si = jnp.cos(ang_i), jnp.sin(ang_i)
    rot = jnp.concatenate(
        [cq, cq, -sq, sq, ci, ci, -si, si, jnp.zeros((LANES - 96, tq), F32)], axis=0)
    c = rot.T
    lane = lax.broadcasted_iota(jnp.int32, (tq, LANES), 1)
    in_q = lane < ROPE_DIM
    cos_q = jnp.where(in_q, c, 1.0)
    sin_q = jnp.where(in_q, pltpu.roll(c, LANES - 32, 1), 0.0)
    first_i = lane < IDX_ROPE_DIM
    second_i = (lane >= IDX_DIM) & (lane < IDX_DIM + IDX_ROPE_DIM)
    cos_i = jnp.where(first_i, pltpu.roll(c, LANES - 64, 1),
                      jnp.where(second_i, c, 1.0))
    sin_i = jnp.where(first_i, pltpu.roll(c, LANES - 80, 1),
                      jnp.where(second_i, pltpu.roll(c, LANES - 16, 1), 0.0))
    half_q = lane < ROPE_DIM // 2
    half_i = (lane % IDX_DIM) < IDX_ROPE_DIM // 2

    u_ref[...] = proj[:, 0:OFF_Q]
    scale = HEAD_DIM ** -0.5
    for hh in range(N_HEADS):
        c0 = OFF_Q + hh * HEAD_DIM
        r = _rope(proj[:, c0:c0 + HEAD_DIM], cos_q, sin_q, half_q, ROPE_DIM // 2)
        q_ref[:, hh * HEAD_DIM:(hh + 1) * HEAD_DIM] = (r * scale).astype(BF16)
    for hh in range(N_KV_HEADS):
        c0 = OFF_K + hh * HEAD_DIM
        r = _rope(proj[:, c0:c0 + HEAD_DIM], cos_q, sin_q, half_q, ROPE_DIM // 2)
        k_ref[:, hh * HEAD_DIM:(hh + 1) * HEAD_DIM] = r.astype(BF16)
    ones = jnp.ones((tq, HEAD_DIM), BF16)
    for hh in range(N_KV_HEADS):
        c0 = OFF_V + hh * HEAD_DIM
        v_ref[:, 2 * hh * HEAD_DIM:(2 * hh + 1) * HEAD_DIM] = proj[:, c0:c0 + HEAD_DIM].astype(BF16)
        v_ref[:, (2 * hh + 1) * HEAD_DIM:(2 * hh + 2) * HEAD_DIM] = ones
    for t in range(IDX_HEADS * IDX_DIM // LANES):
        c0 = OFF_IQ + t * LANES
        r = _rope(proj[:, c0:c0 + LANES], cos_i, sin_i, half_i, IDX_ROPE_DIM // 2)
        iq_ref[:, t * LANES:(t + 1) * LANES] = r.astype(BF16)
    tail = proj[:, OFF_IK:OFF_IK + LANES]
    r = _rope(tail, cos_i, sin_i, half_i, IDX_ROPE_DIM // 2)
    ik_ref[...] = r[:, 0:IDX_DIM].astype(BF16)
    iw_ref[...] = tail[:, IDX_DIM:IDX_DIM + IDX_HEADS]


def _rope_freqs():
    f_q = ROPE_THETA ** (-jnp.arange(0, ROPE_DIM, 2, dtype=F32) / ROPE_DIM)
    f_i = ROPE_THETA ** (-jnp.arange(0, IDX_ROPE_DIM, 2, dtype=F32) / IDX_ROPE_DIM)
    return jnp.concatenate([f_q, f_i]).reshape(-1, 1)


def _inproj(x2, pos2, g, b, w_pad, tq):
    T = x2.shape[0]
    row = lambda i: (i, 0)
    fixed = lambda i: (0, 0)
    out_shapes = (
        jax.ShapeDtypeStruct((T, POOL_WIDTH), F32),
        jax.ShapeDtypeStruct((T, ATTN_WIDTH), BF16),
        jax.ShapeDtypeStruct((T, KV_WIDTH), BF16),
        jax.ShapeDtypeStruct((T, 2 * KV_WIDTH), BF16),
        jax.ShapeDtypeStruct((T, IDX_HEADS * IDX_DIM), BF16),
        jax.ShapeDtypeStruct((T, IDX_DIM), BF16),
        jax.ShapeDtypeStruct((T, IDX_HEADS), F32),
    )
    return pl.pallas_call(
        _inproj_kernel,
        grid=(T // tq,),
        in_specs=[
            pl.BlockSpec((tq, D_MODEL), row),
            pl.BlockSpec((1, tq), lambda i: (0, i)),
            pl.BlockSpec((1, D_MODEL), fixed),
            pl.BlockSpec((1, D_MODEL), fixed),
            pl.BlockSpec((D_MODEL, IN_WIDTH_PAD), fixed),
            pl.BlockSpec((ROPE_DIM // 2 + IDX_ROPE_DIM // 2, 1), fixed),
        ],
        out_specs=tuple(pl.BlockSpec((tq, s.shape[1]), row) for s in out_shapes),
        out_shape=out_shapes,
        compiler_params=pltpu.CompilerParams(
            dimension_semantics=("parallel",), vmem_limit_bytes=VMEM_LIMIT),
    )(x2, pos2, g, b, w_pad, _rope_freqs())


def _row_count(mask):
    return jnp.sum(jnp.where(mask, 1.0, 0.0), axis=-1, keepdims=True)


def _select_bias(iq_ref, ik_ref, iw_ref, key_ref, bias_ref, *, i, tq, S, n_sel):
    q_pos = i * tq + lax.broadcasted_iota(jnp.int32, (tq, S), 0)
    k_pos = lax.broadcasted_iota(jnp.int32, (tq, S), 1)
    causal = k_pos <= q_pos
    if S <= n_sel:
        bias_ref[:, 0:S] = jnp.where(causal, 0.0, NEG_INF)
        return

    w_scale = (IDX_HEADS ** -0.5) * (IDX_DIM ** -0.5)
    iw = iw_ref[...] * w_scale
    ik = ik_ref[0:S, :]
    nt = (((1,), (1,)), ((), ()))
    sc = jnp.zeros((tq, S), F32)
    for hh in range(IDX_HEADS):
        s = lax.dot_general(iq_ref[:, hh * IDX_DIM:(hh + 1) * IDX_DIM], ik, nt,
                            preferred_element_type=F32)
        sc = sc + jnp.maximum(s, 0.0) * iw[:, hh:hh + 1]
    sc = jnp.where(sc == 0.0, 0.0, sc)
    sc = jnp.where(causal, sc, NEG_INF)
    bits = pltpu.bitcast(sc, jnp.int32)
    key_ref[:, 0:S] = bits ^ ((bits >> 31) & jnp.int32(0x7FFFFFFF))

    def value_step(b, t_u):
        cand = t_u | lax.shift_left(jnp.int32(1), 31 - b)
        cnt = _row_count(key_ref[:, 0:S] >= (cand ^ jnp.int32(INT_MIN)))
        return jnp.where(cnt >= n_sel, cand, t_u)

    def value_steps(bb, t_u):
        for r in range(SEARCH_UNROLL):
            t_u = value_step(bb * SEARCH_UNROLL + r, t_u)
        return t_u

    t_u = lax.fori_loop(0, 32 // SEARCH_UNROLL, value_steps, jnp.zeros((tq, 1), jnp.int32))
    thr = t_u ^ jnp.int32(INT_MIN)
    key = key_ref[:, 0:S]
    need = n_sel - _row_count(key > thr)
    tied = key == thr
    bias_ref[:, 0:S] = jnp.where((key >= thr) & causal, 0.0, NEG_INF)
    excess = _row_count(tied & causal) > need
    any_excess = jnp.max(jnp.where(excess, 1.0, 0.0)) > 0.0

    @pl.when(any_excess)
    def _():
        kp = lax.broadcasted_iota(jnp.int32, (tq, S), 1)
        qp = i * tq + lax.broadcasted_iota(jnp.int32, (tq, S), 0)
        bias_ref[:, 0:S] = jnp.where(key_ref[:, 0:S] == thr, kp.astype(F32), NOT_TIED)
        idx_bits = (S - 1).bit_length()

        j0 = jnp.zeros((tq, 1), jnp.int32)
        for b in range(idx_bits - 1, -1, -1):
            cand = j0 | jnp.int32(1 << b)
            cnt = _row_count(bias_ref[:, 0:S] < cand.astype(F32))
            j0 = jnp.where(cnt < need, cand, j0)
        admit = ((key_ref[:, 0:S] > thr) | (bias_ref[:, 0:S] <= j0.astype(F32))) & (kp <= qp)
        bias_ref[:, 0:S] = jnp.where(admit, 0.0, NEG_INF)


def _attn_tile(q_ref, k_ref, v_ref, iq_ref, ik_ref, iw_ref, o_ref, key_ref, bias_ref,
               *, i, tq, S, n_sel):
    _select_bias(iq_ref, ik_ref, iw_ref, key_ref, bias_ref, i=i, tq=tq, S=S, n_sel=n_sel)
    nt = (((1,), (1,)), ((), ()))
    for hh in range(N_HEADS):
        g = hh // (N_HEADS // N_KV_HEADS)
        kg = k_ref[0:S, g * HEAD_DIM:(g + 1) * HEAD_DIM]
        vg = v_ref[0:S, 2 * g * HEAD_DIM:2 * (g + 1) * HEAD_DIM]
        logits = lax.dot_general(q_ref[:, hh * HEAD_DIM:(hh + 1) * HEAD_DIM], kg, nt,
                                 preferred_element_type=F32) + bias_ref[:, 0:S]
        m = jnp.max(logits, axis=-1, keepdims=True)
        p = jnp.exp((logits - m).astype(BF16))
        ol = jnp.dot(p, vg, preferred_element_type=F32)
        o = ol[:, 0:HEAD_DIM] / ol[:, HEAD_DIM:HEAD_DIM + 1]
        o_ref[:, hh * HEAD_DIM:(hh + 1) * HEAD_DIM] = o.astype(BF16)


def _attn_kernel(*refs, tq, L, n_sel):
    i = pl.program_id(1)
    band = L // CAUSAL_BANDS
    for v in range(CAUSAL_BANDS):
        @pl.when((i * tq) // band == v)
        def _(v=v):
            _attn_tile(*refs, i=i, tq=tq, S=(v + 1) * band, n_sel=n_sel)


def _attn(q, k, v, iq, ik, iw, B, L, tq):
    T = B * L
    nq = L // tq
    n_sel = min(TOPK_MAX, L // 4)
    assert L % (CAUSAL_BANDS * tq) == 0
    qrow = lambda b, i: (b * nq + i, 0)
    seq = lambda b, i: (b, 0)
    return pl.pallas_call(
        functools.partial(_attn_kernel, tq=tq, L=L, n_sel=n_sel),
        grid=(B, nq),
        in_specs=[
            pl.BlockSpec((tq, ATTN_WIDTH), qrow),
            pl.BlockSpec((L, KV_WIDTH), seq),
            pl.BlockSpec((L, 2 * KV_WIDTH), seq),
            pl.BlockSpec((tq, IDX_HEADS * IDX_DIM), qrow),
            pl.BlockSpec((L, IDX_DIM), seq),
            pl.BlockSpec((tq, IDX_HEADS), qrow),
        ],
        out_specs=pl.BlockSpec((tq, ATTN_WIDTH), qrow),
        out_shape=jax.ShapeDtypeStruct((T, ATTN_WIDTH), BF16),
        scratch_shapes=[pltpu.VMEM((tq, L), jnp.int32), pltpu.VMEM((tq, L), F32)],
        compiler_params=pltpu.CompilerParams(
            dimension_semantics=("parallel", "arbitrary"), vmem_limit_bytes=VMEM_LIMIT),
    )(q, k, v, iq, ik, iw)


def _outproj_kernel(x_ref, u_ref, halo_ref, a_ref, gin_ref, bin_ref, wpool_ref, pscale_ref,
                    wout_ref, g1_ref, b1_ref, wr_hi_ref, wr_lo_ref, br_ref,
                    h1_ref, h1b_ref, ids_ref, gates_ref, *, tq, tiles_per_seq):
    i = pl.program_id(0)
    seq_tile = i % tiles_per_seq
    u = u_ref[...]
    halo = jnp.where(seq_tile == 0, 0.0, halo_ref[...])
    ext = jnp.concatenate([halo, u], axis=0)
    lane = lax.broadcasted_iota(jnp.int32, (tq, POOL_WIDTH), 1)
    grp = lane // POOL_GDIM
    win = jnp.zeros((tq, POOL_WIDTH), F32)
    s = ext
    for gi, w in enumerate(POOL_WINDOWS):
        s = s + pltpu.roll(s, w // 2, 0)
        win = jnp.where(grp == gi, s[POOL_HALO:, :], win)
    t_seq = seq_tile * tq + lax.broadcasted_iota(jnp.int32, (tq, POOL_WIDTH), 0)
    width = lax.shift_left(jnp.int32(2), grp)
    cnt = jnp.minimum(t_seq + 1, width).astype(F32)
    d = win / cnt - u
    y_pool = jnp.dot(d.astype(BF16), wpool_ref[...], preferred_element_type=F32) * pscale_ref[...]

    mix = jnp.dot(y_pool.astype(BF16), wout_ref[0:POOL_WIDTH, :], preferred_element_type=F32)
    mix = mix + jnp.dot(a_ref[...], wout_ref[POOL_WIDTH:, :], preferred_element_type=F32)
    h = _layer_norm(x_ref[...], gin_ref[...], bin_ref[...])
    h1 = _layer_norm(ALPHA_RES * h + mix, g1_ref[...], b1_ref[...])
    h1_ref[...] = h1
    h1_hi = h1.astype(BF16)
    h1b_ref[...] = h1_hi
    h1_lo = (h1 - h1_hi.astype(F32)).astype(BF16)
    wr_hi = wr_hi_ref[...]
    logits = (jnp.dot(h1_hi, wr_hi, preferred_element_type=F32)
              + jnp.dot(h1_lo, wr_hi, preferred_element_type=F32)
              + jnp.dot(h1_hi, wr_lo_ref[...], preferred_element_type=F32)) + br_ref[...]
    lane_e = lax.broadcasted_iota(jnp.int32, (tq, LANES), 1)
    lg = jnp.where(lane_e < N_EXPERTS, logits, -jnp.inf)
    ids = jnp.zeros((tq, LANES), jnp.int32)
    ex = jnp.zeros((tq, LANES), F32)
    top0 = None
    for kk in range(TOP_K):
        m = jnp.max(lg, axis=-1, keepdims=True)
        idx = jnp.min(jnp.where(lg == m, lane_e, LANES), axis=-1, keepdims=True)
        if top0 is None:
            top0 = m
        ids = jnp.where(lane_e == kk, idx, ids)
        ex = jnp.where(lane_e == kk, jnp.exp(m - top0), ex)
        lg = jnp.where(lane_e == idx, -jnp.inf, lg)
    ids_ref[...] = ids
    gates_ref[...] = ex / jnp.sum(ex, axis=-1, keepdims=True)


def _outproj(x2, u, attn, gin, bin_, wpool_bd, pscale, wout, g1, b1, wr_hi, wr_lo, br, L, tq):
    T = x2.shape[0]
    row = lambda i: (i, 0)
    fixed = lambda i: (0, 0)
    halo = lambda i: (jnp.maximum(i * (tq // POOL_HALO) - 1, 0), 0)
    out_shapes = (
        jax.ShapeDtypeStruct((T, D_MODEL), F32),
        jax.ShapeDtypeStruct((T, D_MODEL), BF16),
        jax.ShapeDtypeStruct((T, LANES), jnp.int32),
        jax.ShapeDtypeStruct((T, LANES), F32),
    )
    return pl.pallas_call(
        functools.partial(_outproj_kernel, tq=tq, tiles_per_seq=L // tq),
        grid=(T // tq,),
        in_specs=[
            pl.BlockSpec((tq, D_MODEL), row),
            pl.BlockSpec((tq, POOL_WIDTH), row),
            pl.BlockSpec((POOL_HALO, POOL_WIDTH), halo),
            pl.BlockSpec((tq, ATTN_WIDTH), row),
            pl.BlockSpec((1, D_MODEL), fixed),
            pl.BlockSpec((1, D_MODEL), fixed),
            pl.BlockSpec((POOL_WIDTH, POOL_WIDTH), fixed),
            pl.BlockSpec((1, POOL_WIDTH), fixed),
            pl.BlockSpec((D_MODEL, D_MODEL), fixed),
            pl.BlockSpec((1, D_MODEL), fixed),
            pl.BlockSpec((1, D_MODEL), fixed),
            pl.BlockSpec((D_MODEL, LANES), fixed),
            pl.BlockSpec((D_MODEL, LANES), fixed),
            pl.BlockSpec((1, LANES), fixed),
        ],
        out_specs=tuple(pl.BlockSpec((tq, s.shape[1]), row) for s in out_shapes),
        out_shape=out_shapes,
        compiler_params=pltpu.CompilerParams(
            dimension_semantics=("parallel",), vmem_limit_bytes=VMEM_LIMIT),
    )(x2, u, u, attn, gin, bin_, wpool_bd, pscale, wout, g1, b1, wr_hi, wr_lo, br)


ROUTE_TB = 512
ROW_ALIGN = 16
RUN_BITS = 6
STAGE_ROWS = 2560
STAGE_CHUNK = 512
COMBINE_CHUNK = 1280
FFN_TM = 512
TAIL_BITS = 5


def _route_kernel(ids_ref, slot_ref, slot_t_ref, meta_ref, carry_ref):
    c = pl.program_id(0)

    @pl.when(c == 0)
    def _():
        carry_ref[...] = jnp.zeros(carry_ref.shape, F32)

    tb = ids_ref.shape[0]
    ids = ids_ref[...]
    lane = lax.broadcasted_iota(jnp.int32, (tb, LANES), 1)
    onehot = [jnp.where(lane == ids[:, k:k + 1], 1.0, 0.0) for k in range(TOP_K)]
    member = onehot[0] + onehot[1] + onehot[2] + onehot[3]
    r = lax.broadcasted_iota(jnp.int32, (tb, tb), 0)
    cc = lax.broadcasted_iota(jnp.int32, (tb, tb), 1)
    before = jnp.where(cc < r, 1.0, 0.0).astype(BF16)
    lrank = jnp.dot(before, member.astype(BF16), preferred_element_type=F32)
    n = jnp.sum(member, axis=0, keepdims=True)
    units = jnp.ceil(n * (1.0 / ROW_ALIGN))
    er = lax.broadcasted_iota(jnp.int32, (LANES, LANES), 0)
    ec = lax.broadcasted_iota(jnp.int32, (LANES, LANES), 1)
    lower = jnp.where(er < ec, 1.0, 0.0).astype(BF16)
    off_units = jnp.dot(jnp.broadcast_to(units, (8, LANES)).astype(BF16), lower,
                        preferred_element_type=F32)[0:1, :]
    base = off_units * ROW_ALIGN + lrank
    slot = jnp.zeros((tb, LANES), jnp.int32)
    for k in range(TOP_K):
        sk = jnp.sum(onehot[k] * base, axis=-1, keepdims=True).astype(jnp.int32)
        slot = jnp.where(lane == k, sk, slot)
    slot_ref[...] = slot
    slot_t_ref[...] = slot.T[0:8, :]
    row = lax.broadcasted_iota(jnp.int32, (8, LANES), 0)
    meta = jnp.where(row == 0, units, jnp.where(row == 1, off_units, carry_ref[...]))
    meta_ref[...] = meta.astype(jnp.int32)
    carry_ref[...] = carry_ref[...] + units


def _route(ids):
    T = ids.shape[0]
    nblk = T // ROUTE_TB
    return pl.pallas_call(
        _route_kernel,
        grid=(nblk,),
        in_specs=[pl.BlockSpec((ROUTE_TB, LANES), lambda c: (c, 0))],
        out_specs=(pl.BlockSpec((ROUTE_TB, LANES), lambda c: (c, 0)),
                   pl.BlockSpec((8, ROUTE_TB), lambda c: (0, c)),
                   pl.BlockSpec((8, LANES), lambda c: (c, 0))),
        out_shape=(jax.ShapeDtypeStruct((T, LANES), jnp.int32),
                   jax.ShapeDtypeStruct((8, T), jnp.int32),
                   jax.ShapeDtypeStruct((nblk * 8, LANES), jnp.int32)),
        scratch_shapes=[pltpu.VMEM((1, LANES), F32)],
        compiler_params=pltpu.CompilerParams(
            dimension_semantics=("arbitrary",), vmem_limit_bytes=VMEM_LIMIT),
    )(ids)


def _pieces_of(m, so, do, bits, visit):
    for b in bits:
        done = m & ((1 << b) - 1)

        @pl.when(((m >> b) & 1) == 1)
        def _(b=b, done=done):
            visit(b, pl.multiple_of((so + done) * ROW_ALIGN, ROW_ALIGN),
                  pl.multiple_of((do + done) * ROW_ALIGN, ROW_ALIGN), ROW_ALIGN << b)


def _piece_lists(units, off_units, dst_units):
    bits = jnp.arange(RUN_BITS, dtype=jnp.int32)[None, :, None]
    m = units[:, None, :]
    has = ((m >> bits) & 1) == 1
    done = m & ((1 << bits) - 1)
    pos = jnp.cumsum(has, axis=-1) - 1
    at = has[..., None] & (pos[..., None] == jnp.arange(N_EXPERTS, dtype=jnp.int32))
    pick = lambda v: jnp.sum(jnp.where(at, v[..., None], 0), axis=2).reshape(-1).astype(jnp.int32)
    cnt = jnp.sum(has, axis=-1).reshape(-1).astype(jnp.int32)
    return cnt, pick(off_units[:, None, :] + done), pick(dst_units[:, None, :] + done)


def _run_pieces(cnt_ref, srow_ref, grow_ref, c, visit):
    for b in range(RUN_BITS):
        base = (c * RUN_BITS + b) * N_EXPERTS

        def body(j, carry, b=b, base=base):
            visit(b, j, pl.multiple_of(srow_ref[base + j] * ROW_ALIGN, ROW_ALIGN),
                  pl.multiple_of(grow_ref[base + j] * ROW_ALIGN, ROW_ALIGN), ROW_ALIGN << b)
            return carry

        lax.fori_loop(0, cnt_ref[c * RUN_BITS + b], body, 0)


def _dispatch_kernel(cnt_ref, srow_ref, grow_ref, tail_ref, tail_dst_ref, slot_t_ref, h_ref,
                     xs_ref, stage_ref, zero_ref, sems, *, nblk):
    c = pl.program_id(0)
    par = c % 2
    tb = h_ref.shape[0]
    for r0 in range(0, STAGE_ROWS, STAGE_CHUNK):
        srow = r0 + lax.broadcasted_iota(jnp.int32, (STAGE_CHUNK, tb), 0)
        hit = srow == slot_t_ref[0:1, :]
        for k in range(1, TOP_K):
            hit = hit | (srow == slot_t_ref[k:k + 1, :])
        perm = jnp.where(hit, 1.0, 0.0).astype(BF16)
        stage_ref[par, r0:r0 + STAGE_CHUNK, :] = jnp.dot(
            perm, h_ref[...], preferred_element_type=F32).astype(BF16)

    def piece(p):
        return lambda b, j, s_row, g_row, rows: pltpu.make_async_copy(
            stage_ref.at[p, pl.ds(s_row, rows), :], xs_ref.at[pl.ds(g_row, rows), :],
            sems.at[p, j, b])

    _run_pieces(cnt_ref, srow_ref, grow_ref, c, lambda *a: piece(par)(*a).start())

    @pl.when(c > 0)
    def _():
        _run_pieces(cnt_ref, srow_ref, grow_ref, c - 1, lambda *a: piece(1 - par)(*a).wait())

    @pl.when(c == nblk - 1)
    def _():
        _run_pieces(cnt_ref, srow_ref, grow_ref, c, lambda *a: piece(par)(*a).wait())
        zero_ref[...] = jnp.zeros(zero_ref.shape, BF16)

        def tail(e, b, z_row, g_row, rows):
            del z_row
            return pltpu.make_async_copy(zero_ref.at[pl.ds(0, rows), :],
                                         xs_ref.at[pl.ds(g_row, rows), :], sems.at[0, e, b])

        def tails(act):
            for e in range(N_EXPERTS):
                _pieces_of(tail_ref[e], 0, tail_dst_ref[e], range(TAIL_BITS),
                           lambda b, z, g, rows, e=e: act(tail(e, b, z, g, rows)))

        tails(lambda cp: cp.start())
        tails(lambda cp: cp.wait())

        zrows = zero_ref.shape[0]
        used = (tail_dst_ref[N_EXPERTS - 1] + tail_ref[N_EXPERTS - 1]) * ROW_ALIGN
        n_rest = (xs_ref.shape[0] - used) // zrows

        def rest(i):
            row0 = pl.multiple_of(used + i * zrows, ROW_ALIGN)
            return pltpu.make_async_copy(
                zero_ref, xs_ref.at[pl.ds(row0, zrows), :],
                sems.at[i // (N_EXPERTS * RUN_BITS), (i // RUN_BITS) % N_EXPERTS, i % RUN_BITS])

        lax.fori_loop(0, n_rest, lambda i, _: (rest(i).start(), 0)[1], 0)
        lax.fori_loop(0, n_rest, lambda i, _: (rest(i).wait(), 0)[1], 0)


def _dispatch(units, off, dst, tail, tail_dst, slot_t, h1b, n_rows):
    T = h1b.shape[0]
    nblk = T // ROUTE_TB
    grid_spec = pltpu.PrefetchScalarGridSpec(
        num_scalar_prefetch=5,
        grid=(nblk,),
        in_specs=[pl.BlockSpec((8, ROUTE_TB), lambda c, *_: (0, c)),
                  pl.BlockSpec((ROUTE_TB, D_MODEL), lambda c, *_: (c, 0))],
        out_specs=pl.BlockSpec(memory_space=pl.ANY),
        scratch_shapes=[pltpu.VMEM((2, STAGE_ROWS, D_MODEL), BF16),
                        pltpu.VMEM((ROW_ALIGN << (TAIL_BITS - 1), D_MODEL), BF16),
                        pltpu.SemaphoreType.DMA((2, N_EXPERTS, RUN_BITS))],
    )
    return pl.pallas_call(
        functools.partial(_dispatch_kernel, nblk=nblk),
        grid_spec=grid_spec,
        out_shape=jax.ShapeDtypeStruct((n_rows, D_MODEL), BF16),
        compiler_params=pltpu.CompilerParams(
            dimension_semantics=("arbitrary",), vmem_limit_bytes=VMEM_LIMIT),
    )(units, off, dst, tail, tail_dst, slot_t, h1b)


def _ffn_kernel(tile_e_ref, n_tiles_ref, xs_ref, wgu_ref, bgu_ref, wd_ref, bd_ref, y_ref,
                wgu_b, wd_b, *, n_chunk):
    j = pl.program_id(0)
    active = j < n_tiles_ref[0]
    new_expert = (j == 0) | (tile_e_ref[j] != tile_e_ref[jnp.maximum(j - 1, 0)])

    @pl.when(active & new_expert)
    def _():
        wgu_b[...] = wgu_ref[0].astype(BF16)
        wd_b[...] = wd_ref[0].astype(BF16)

    @pl.when(active)
    def _():
        xs = xs_ref[...]
        cw = D_FF // n_chunk
        acc = jnp.zeros(y_ref.shape, F32)
        for c in range(n_chunk):
            gate = jnp.dot(xs, wgu_b[:, c * cw:(c + 1) * cw], preferred_element_type=F32)
            gate = gate + bgu_ref[0, :, c * cw:(c + 1) * cw]
            up = jnp.dot(xs, wgu_b[:, D_FF + c * cw:D_FF + (c + 1) * cw],
                         preferred_element_type=F32)
            up = up + bgu_ref[0, :, D_FF + c * cw:D_FF + (c + 1) * cw]
            gate = jnp.minimum(gate, SWIGLU_LIMIT)
            up = jnp.clip(up, -SWIGLU_LIMIT, SWIGLU_LIMIT)
            act = (up + 1.0) * gate * jax.nn.sigmoid(SWIGLU_ALPHA * gate)
            acc = acc + jnp.dot(act.astype(BF16), wd_b[c * cw:(c + 1) * cw, :],
                                preferred_element_type=F32)
        y_ref[...] = (acc + bd_ref[0]).astype(BF16)

    @pl.when(jnp.logical_not(active))
    def _():
        y_ref[...] = jnp.zeros(y_ref.shape, BF16)


def _ffn(tile_e, n_tiles, xs, wgu, bgu, wd, bd):
    n_rows = xs.shape[0]
    last = lambda j, nt: jnp.minimum(j, nt[0] - 1)
    row = lambda j, te, nt: (last(j, nt), 0)
    exp3 = lambda j, te, nt: (te[last(j, nt)], 0, 0)
    grid_spec = pltpu.PrefetchScalarGridSpec(
        num_scalar_prefetch=2,
        grid=(n_rows // FFN_TM,),
        in_specs=[
            pl.BlockSpec((FFN_TM, D_MODEL), row),
            pl.BlockSpec((1, D_MODEL, 2 * D_FF), exp3),
            pl.BlockSpec((1, 1, 2 * D_FF), exp3),
            pl.BlockSpec((1, D_FF, D_MODEL), exp3),
            pl.BlockSpec((1, 1, D_MODEL), exp3),
        ],
        out_specs=pl.BlockSpec((FFN_TM, D_MODEL), lambda j, te, nt: (j, 0)),
        scratch_shapes=[pltpu.VMEM((D_MODEL, 2 * D_FF), BF16), pltpu.VMEM((D_FF, D_MODEL), BF16)],
    )
    return pl.pallas_call(
        functools.partial(_ffn_kernel, n_chunk=2),
        grid_spec=grid_spec,
        out_shape=jax.ShapeDtypeStruct((n_rows, D_MODEL), BF16),
        compiler_params=pltpu.CompilerParams(
            dimension_semantics=("arbitrary",), vmem_limit_bytes=FFN_VMEM_LIMIT),
    )(tile_e, n_tiles, xs, wgu, bgu, wd, bd)


def _combine_kernel(cnt_ref, srow_ref, grow_ref, slot_ref, gates_ref, h1_ref, ys_ref,
                    g_ref, b_ref, o_ref, stage_ref, sems, *, nblk):
    c = pl.program_id(0)
    par = c % 2
    tb = h1_ref.shape[0]

    def piece(p):
        return lambda b, j, s_row, g_row, rows: pltpu.make_async_copy(
            ys_ref.at[pl.ds(g_row, rows), :], stage_ref.at[p, pl.ds(s_row, rows), :],
            sems.at[p, j, b])

    @pl.when(c == 0)
    def _():
        stage_ref[...] = jnp.zeros(stage_ref.shape, BF16)
        _run_pieces(cnt_ref, srow_ref, grow_ref, c, lambda *a: piece(0)(*a).start())

    _run_pieces(cnt_ref, srow_ref, grow_ref, c, lambda *a: piece(par)(*a).wait())

    @pl.when(c + 1 < nblk)
    def _():
        _run_pieces(cnt_ref, srow_ref, grow_ref, c + 1, lambda *a: piece(1 - par)(*a).start())

    slot = slot_ref[...]
    gates = gates_ref[...]
    ffn = jnp.zeros((tb, D_MODEL), F32)
    for r0 in range(0, STAGE_ROWS, COMBINE_CHUNK):
        scol = r0 + lax.broadcasted_iota(jnp.int32, (tb, COMBINE_CHUNK), 1)
        w = jnp.zeros((tb, COMBINE_CHUNK), F32)
        for k in range(TOP_K):
            w = w + jnp.where(scol == slot[:, k:k + 1], gates[:, k:k + 1], 0.0)
        ffn = ffn + jnp.dot(w.astype(BF16), stage_ref[par, r0:r0 + COMBINE_CHUNK, :],
                            preferred_element_type=F32)
    o_ref[...] = _layer_norm(ALPHA_RES * h1_ref[...] + ffn, g_ref[...], b_ref[...])


def _combine(units, off, dst, slot, gates, h1, ys, g, b):
    T = h1.shape[0]
    nblk = T // ROUTE_TB
    blk = lambda c, *_: (c, 0)
    fixed = lambda c, *_: (0, 0)
    grid_spec = pltpu.PrefetchScalarGridSpec(
        num_scalar_prefetch=3,
        grid=(nblk,),
        in_specs=[pl.BlockSpec((ROUTE_TB, LANES), blk),
                  pl.BlockSpec((ROUTE_TB, LANES), blk),
                  pl.BlockSpec((ROUTE_TB, D_MODEL), blk),
                  pl.BlockSpec(memory_space=pl.ANY),
                  pl.BlockSpec((1, D_MODEL), fixed),
                  pl.BlockSpec((1, D_MODEL), fixed)],
        out_specs=pl.BlockSpec((ROUTE_TB, D_MODEL), blk),
        scratch_shapes=[pltpu.VMEM((2, STAGE_ROWS, D_MODEL), BF16),
                        pltpu.SemaphoreType.DMA((2, N_EXPERTS, RUN_BITS))],
    )
    return pl.pallas_call(
        functools.partial(_combine_kernel, nblk=nblk),
        grid_spec=grid_spec,
        out_shape=jax.ShapeDtypeStruct((T, D_MODEL), F32),
        compiler_params=pltpu.CompilerParams(
            dimension_semantics=("arbitrary",), vmem_limit_bytes=VMEM_LIMIT),
    )(units, off, dst, slot, gates, h1, ys, g, b)


def kernel(x, positions, ln_in_g, ln_in_b, w_in, w_pool, pool_scale, w_out, ln1_g, ln1_b,
           w_router, b_router, w_gate_up, b_gate_up, w_down, b_down, ln2_g, ln2_b):
    B, L, D = x.shape
    T = B * L
    assert T % ROUTE_TB == 0 and D == D_MODEL
    tq = min(256, L)
    x2 = x.reshape(T, D)
    pos2 = positions.reshape(1, T)
    gin = ln_in_g.reshape(1, D)
    bin_ = ln_in_b.reshape(1, D)

    w_pad = jnp.pad(w_in[0], ((0, 0), (0, IN_WIDTH_PAD - IN_WIDTH))).astype(BF16)
    u, q, k, v, iq, ik, iw = _inproj(x2, pos2, gin, bin_, w_pad, tq)

    attn = _attn(q, k, v, iq, ik, iw, B, L, min(256, L // CAUSAL_BANDS))

    wpool_bd = jnp.zeros((POOL_WIDTH, POOL_WIDTH), F32)
    for gi in range(POOL_GROUPS):
        sl = slice(gi * POOL_GDIM, (gi + 1) * POOL_GDIM)
        wpool_bd = wpool_bd.at[sl, sl].set(w_pool[0, gi])
    wr = jnp.pad(w_router[0], ((0, 0), (0, LANES - N_EXPERTS)))
    wr_hi = wr.astype(BF16)
    wr_lo = (wr - wr_hi.astype(F32)).astype(BF16)
    br = jnp.pad(b_router[0], (0, LANES - N_EXPERTS)).reshape(1, LANES)
    h1, h1b, ids, gates = _outproj(
        x2, u, attn, gin, bin_, wpool_bd.astype(BF16), pool_scale[0].reshape(1, POOL_WIDTH),
        w_out[0].astype(BF16), ln1_g[0].reshape(1, D), ln1_b[0].reshape(1, D),
        wr_hi, wr_lo, br, L, tq)

    nblk = T // ROUTE_TB
    slot, slot_t, meta = _route(ids)
    meta = meta.reshape(nblk, 8, LANES)[:, :, :N_EXPERTS]
    units, off_units, base_units = meta[:, 0], meta[:, 1], meta[:, 2]
    tile_units = FFN_TM // ROW_ALIGN
    total_units = base_units[-1] + units[-1]
    region_units = ((total_units + tile_units - 1) // tile_units) * tile_units
    region_end = jnp.cumsum(region_units)
    dst_units = (region_end - region_units)[None, :] + base_units
    max_rows = T * TOP_K + nblk * N_EXPERTS * (ROW_ALIGN - 1) + N_EXPERTS * (FFN_TM - 1)
    n_rows = -(-max_rows // FFN_TM) * FFN_TM
    tile_start = jnp.arange(n_rows // FFN_TM, dtype=jnp.int32) * tile_units
    tile_e = jnp.minimum(jnp.sum(tile_start[:, None] >= region_end[None, :], axis=1),
                         N_EXPERTS - 1).astype(jnp.int32)
    n_tiles = (region_end[-1:] // tile_units).astype(jnp.int32)
    cnt, srow, grow = _piece_lists(units, off_units, dst_units)
    tail_units = (region_units - total_units).astype(jnp.int32)
    tail_dst = (region_end - tail_units).astype(jnp.int32)

    xs = _dispatch(cnt, srow, grow, tail_units, tail_dst, slot_t, h1b, n_rows)
    ys = _ffn(tile_e, n_tiles, xs, w_gate_up[0], b_gate_up[0].reshape(N_EXPERTS, 1, 2 * D_FF),
              w_down[0], b_down[0].reshape(N_EXPERTS, 1, D))
    out = _combine(cnt, srow, grow, slot, gates, h1, ys,
                   ln2_g[0].reshape(1, D), ln2_b[0].reshape(1, D))
    return out.reshape(B, L, D)
```

```python
import functools

import jax
import jax.numpy as jnp
from jax import lax
from jax.experimental import pallas as pl
from jax.experimental.pallas import tpu as pltpu

F32 = jnp.float32
BF16 = jnp.bfloat16

D_MODEL = 1024
POOL_WIDTH = 256
POOL_GROUPS = 4
POOL_GDIM = 64
POOL_WINDOWS = (2, 4, 8, 16)
POOL_HALO = 16
HEAD_DIM = 128
ATTN_WIDTH = 768
N_HEADS = 6
N_KV_HEADS = 2
KV_WIDTH = 256
IDX_HEADS = 8
IDX_DIM = 64
TOPK_MAX = 256
ROPE_THETA = 500000.0
ROPE_DIM = 32
IDX_ROPE_DIM = 16
N_EXPERTS = 32
TOP_K = 4
D_FF = 1024
SWIGLU_LIMIT = 7.0
SWIGLU_ALPHA = 1.702
DEPTH = 1
ALPHA_RES = (2.0 * DEPTH) ** 0.25
LN_EPS = 1e-5
NEG_INF = -1e30
OFF_Q = 256
OFF_K = 1024
OFF_V = 1280
OFF_IQ = 1536
OFF_IK = 2048
IN_WIDTH = 2120
IN_WIDTH_PAD = 2176

LANES = 128
VMEM_LIMIT = 48 * 1024 * 1024
FFN_VMEM_LIMIT = 56 * 1024 * 1024
INT_MIN = -2 ** 31
NOT_TIED = 2 ** 30
CAUSAL_BANDS = 4
SEARCH_UNROLL = 8


def _layer_norm(x, g, b):
    mu = jnp.mean(x, axis=-1, keepdims=True)
    xc = x - mu
    var = jnp.mean(xc * xc, axis=-1, keepdims=True)
    return xc * lax.rsqrt(var + LN_EPS) * g + b


def _rope(xh, cos, sin, first_half, half):
    partner = jnp.where(first_half, pltpu.roll(xh, LANES - half, 1), pltpu.roll(xh, half, 1))
    return xh * cos + partner * sin


def _inproj_kernel(x_ref, pos_ref, g_ref, b_ref, w_ref, freq_ref,
                   u_ref, q_ref, k_ref, v_ref, iq_ref, ik_ref, iw_ref):
    tq = x_ref.shape[0]
    h = _layer_norm(x_ref[...], g_ref[...], b_ref[...])
    proj = jnp.dot(h.astype(BF16), w_ref[...], preferred_element_type=F32)

    pos = pos_ref[...].astype(F32)
    ang_q = freq_ref[0:ROPE_DIM // 2, :] * pos
    ang_i = freq_ref[ROPE_DIM // 2:ROPE_DIM // 2 + IDX_ROPE_DIM // 2, :] * pos
    cq, sq = jnp.cos(ang_q), jnp.sin(ang_q)
    ci, si = jnp.cos(ang_i), jnp.sin(ang_i)
    rot = jnp.concatenate(
        [cq, cq, -sq, sq, ci, ci, -si, si, jnp.zeros((LANES - 96, tq), F32)], axis=0)
    c = rot.T
    lane = lax.broadcasted_iota(jnp.int32, (tq, LANES), 1)
    in_q = lane < ROPE_DIM
    cos_q = jnp.where(in_q, c, 1.0)
    sin_q = jnp.where(in_q, pltpu.roll(c, LANES - 32, 1), 0.0)
    first_i = lane < IDX_ROPE_DIM
    second_i = (lane >= IDX_DIM) & (lane < IDX_DIM + IDX_ROPE_DIM)
    cos_i = jnp.where(first_i, pltpu.roll(c, LANES - 64, 1),
                      jnp.where(second_i, c, 1.0))
    sin_i = jnp.where(first_i, pltpu.roll(c, LANES - 80, 1),
                      jnp.where(second_i, pltpu.roll(c, LANES - 16, 1), 0.0))
    half_q = lane < ROPE_DIM // 2
    half_i = (lane % IDX_DIM) < IDX_ROPE_DIM // 2

    u_ref[...] = proj[:, 0:OFF_Q]
    scale = HEAD_DIM ** -0.5
    for hh in range(N_HEADS):
        c0 = OFF_Q + hh * HEAD_DIM
        r = _rope(proj[:, c0:c0 + HEAD_DIM], cos_q, sin_q, half_q, ROPE_DIM // 2)
        q_ref[:, hh * HEAD_DIM:(hh + 1) * HEAD_DIM] = (r * scale).astype(BF16)
    for hh in range(N_KV_HEADS):
        c0 = OFF_K + hh * HEAD_DIM
        r = _rope(proj[:, c0:c0 + HEAD_DIM], cos_q, sin_q, half_q, ROPE_DIM // 2)
        k_ref[:, hh * HEAD_DIM:(hh + 1) * HEAD_DIM] = r.astype(BF16)
    ones = jnp.ones((tq, HEAD_DIM), BF16)
    for hh in range(N_KV_HEADS):
        c0 = OFF_V + hh * HEAD_DIM
        v_ref[:, 2 * hh * HEAD_DIM:(2 * hh + 1) * HEAD_DIM] = proj[:, c0:c0 + HEAD_DIM].astype(BF16)
        v_ref[:, (2 * hh + 1) * HEAD_DIM:(2 * hh + 2) * HEAD_DIM] = ones
    for t in range(IDX_HEADS * IDX_DIM // LANES):
        c0 = OFF_IQ + t * LANES
        r = _rope(proj[:, c0:c0 + LANES], cos_i, sin_i, half_i, IDX_ROPE_DIM // 2)
        iq_ref[:, t * LANES:(t + 1) * LANES] = r.astype(BF16)
    tail = proj[:, OFF_IK:OFF_IK + LANES]
    r = _rope(tail, cos_i, sin_i, half_i, IDX_ROPE_DIM // 2)
    ik_ref[...] = r[:, 0:IDX_DIM].astype(BF16)
    iw_ref[...] = tail[:, IDX_DIM:IDX_DIM + IDX_HEADS]


def _rope_freqs():
    f_q = ROPE_THETA ** (-jnp.arange(0, ROPE_DIM, 2, dtype=F32) / ROPE_DIM)
    f_i = ROPE_THETA ** (-jnp.arange(0, IDX_ROPE_DIM, 2, dtype=F32) / IDX_ROPE_DIM)
    return jnp.concatenate([f_q, f_i]).reshape(-1, 1)


def _inproj(x2, pos2, g, b, w_pad, tq):
    T = x2.shape[0]
    row = lambda i: (i, 0)
    fixed = lambda i: (0, 0)
    out_shapes = (
        jax.ShapeDtypeStruct((T, POOL_WIDTH), F32),
        jax.ShapeDtypeStruct((T, ATTN_WIDTH), BF16),
        jax.ShapeDtypeStruct((T, KV_WIDTH), BF16),
        jax.ShapeDtypeStruct((T, 2 * KV_WIDTH), BF16),
        jax.ShapeDtypeStruct((T, IDX_HEADS * IDX_DIM), BF16),
        jax.ShapeDtypeStruct((T, IDX_DIM), BF16),
        jax.ShapeDtypeStruct((T, IDX_HEADS), F32),
    )
    return pl.pallas_call(
        _inproj_kernel,
        grid=(T // tq,),
        in_specs=[
            pl.BlockSpec((tq, D_MODEL), row),
            pl.BlockSpec((1, tq), lambda i: (0, i)),
            pl.BlockSpec((1, D_MODEL), fixed),
            pl.BlockSpec((1, D_MODEL), fixed),
            pl.BlockSpec((D_MODEL, IN_WIDTH_PAD), fixed),
            pl.BlockSpec((ROPE_DIM // 2 + IDX_ROPE_DIM // 2, 1), fixed),
        ],
        out_specs=tuple(pl.BlockSpec((tq, s.shape[1]), row) for s in out_shapes),
        out_shape=out_shapes,
        compiler_params=pltpu.CompilerParams(
            dimension_semantics=("parallel",), vmem_limit_bytes=VMEM_LIMIT),
    )(x2, pos2, g, b, w_pad, _rope_freqs())


def _row_count(mask):
    return jnp.sum(jnp.where(mask, 1.0, 0.0), axis=-1, keepdims=True)


def _select_bias(iq_ref, ik_ref, iw_ref, key_ref, bias_ref, *, i, tq, S, n_sel):
    q_pos = i * tq + lax.broadcasted_iota(jnp.int32, (tq, S), 0)
    k_pos = lax.broadcasted_iota(jnp.int32, (tq, S), 1)
    causal = k_pos <= q_pos
    if S <= n_sel:
        bias_ref[:, 0:S] = jnp.where(causal, 0.0, NEG_INF)
        return

    w_scale = (IDX_HEADS ** -0.5) * (IDX_DIM ** -0.5)
    iw = iw_ref[...] * w_scale
    ik = ik_ref[0:S, :]
    nt = (((1,), (1,)), ((), ()))
    sc = jnp.zeros((tq, S), F32)
    for hh in range(IDX_HEADS):
        s = lax.dot_general(iq_ref[:, hh * IDX_DIM:(hh + 1) * IDX_DIM], ik, nt,
                            preferred_element_type=F32)
        sc = sc + jnp.maximum(s, 0.0) * iw[:, hh:hh + 1]
    sc = jnp.where(sc == 0.0, 0.0, sc)
    sc = jnp.where(causal, sc, NEG_INF)
    bits = pltpu.bitcast(sc, jnp.int32)
    key_ref[:, 0:S] = bits ^ ((bits >> 31) & jnp.int32(0x7FFFFFFF))

    def value_step(b, t_u):
        cand = t_u | lax.shift_left(jnp.int32(1), 31 - b)
        cnt = _row_count(key_ref[:, 0:S] >= (cand ^ jnp.int32(INT_MIN)))
        return jnp.where(cnt >= n_sel, cand, t_u)

    def value_steps(bb, t_u):
        for r in range(SEARCH_UNROLL):
            t_u = value_step(bb * SEARCH_UNROLL + r, t_u)
        return t_u

    t_u = lax.fori_loop(0, 32 // SEARCH_UNROLL, value_steps, jnp.zeros((tq, 1), jnp.int32))
    thr = t_u ^ jnp.int32(INT_MIN)
    key = key_ref[:, 0:S]
    need = n_sel - _row_count(key > thr)
    tied = key == thr
    bias_ref[:, 0:S] = jnp.where((key >= thr) & causal, 0.0, NEG_INF)
    excess = _row_count(tied & causal) > need
    any_excess = jnp.max(jnp.where(excess, 1.0, 0.0)) > 0.0

    @pl.when(any_excess)
    def _():
        kp = lax.broadcasted_iota(jnp.int32, (tq, S), 1)
        qp = i * tq + lax.broadcasted_iota(jnp.int32, (tq, S), 0)
        bias_ref[:, 0:S] = jnp.where(key_ref[:, 0:S] == thr, kp.astype(F32), NOT_TIED)
        idx_bits = (S - 1).bit_length()

        j0 = jnp.zeros((tq, 1), jnp.int32)
        for b in range(idx_bits - 1, -1, -1):
            cand = j0 | jnp.int32(1 << b)
            cnt = _row_count(bias_ref[:, 0:S] < cand.astype(F32))
            j0 = jnp.where(cnt < need, cand, j0)
        admit = ((key_ref[:, 0:S] > thr) | (bias_ref[:, 0:S] <= j0.astype(F32))) & (kp <= qp)
        bias_ref[:, 0:S] = jnp.where(admit, 0.0, NEG_INF)


def _attn_tile(q_ref, k_ref, v_ref, iq_ref, ik_ref, iw_ref, o_ref, key_ref, bias_ref,
               *, i, tq, S, n_sel):
    _select_bias(iq_ref, ik_ref, iw_ref, key_ref, bias_ref, i=i, tq=tq, S=S, n_sel=n_sel)
    nt = (((1,), (1,)), ((), ()))
    for hh in range(N_HEADS):
        g = hh // (N_HEADS // N_KV_HEADS)
        kg = k_ref[0:S, g * HEAD_DIM:(g + 1) * HEAD_DIM]
        vg = v_ref[0:S, 2 * g * HEAD_DIM:2 * (g + 1) * HEAD_DIM]
        logits = lax.dot_general(q_ref[:, hh * HEAD_DIM:(hh + 1) * HEAD_DIM], kg, nt,
                                 preferred_element_type=F32) + bias_ref[:, 0:S]
        m = jnp.max(logits, axis=-1, keepdims=True)
        p = jnp.exp((logits - m).astype(BF16))
        ol = jnp.dot(p, vg, preferred_element_type=F32)
        o = ol[:, 0:HEAD_DIM] / ol[:, HEAD_DIM:HEAD_DIM + 1]
        o_ref[:, hh * HEAD_DIM:(hh + 1) * HEAD_DIM] = o.astype(BF16)


def _attn_kernel(*refs, tq, L, n_sel):
    i = pl.program_id(1)
    band = L // CAUSAL_BANDS
    for v in range(CAUSAL_BANDS):
        @pl.when((i * tq) // band == v)
        def _(v=v):
            _attn_tile(*refs, i=i, tq=tq, S=(v + 1) * band, n_sel=n_sel)


def _attn(q, k, v, iq, ik, iw, B, L, tq):
    T = B * L
    nq = L // tq
    n_sel = min(TOPK_MAX, L // 4)
    assert L % (CAUSAL_BANDS * tq) == 0
    qrow = lambda b, i: (b * nq + i, 0)
    seq = lambda b, i: (b, 0)
    return pl.pallas_call(
        functools.partial(_attn_kernel, tq=tq, L=L, n_sel=n_sel),
        grid=(B, nq),
        in_specs=[
            pl.BlockSpec((tq, ATTN_WIDTH), qrow),
            pl.BlockSpec((L, KV_WIDTH), seq),
            pl.BlockSpec((L, 2 * KV_WIDTH), seq),
            pl.BlockSpec((tq, IDX_HEADS * IDX_DIM), qrow),
            pl.BlockSpec((L, IDX_DIM), seq),
            pl.BlockSpec((tq, IDX_HEADS), qrow),
        ],
        out_specs=pl.BlockSpec((tq, ATTN_WIDTH), qrow),
        out_shape=jax.ShapeDtypeStruct((T, ATTN_WIDTH), BF16),
        scratch_shapes=[pltpu.VMEM((tq, L), jnp.int32), pltpu.VMEM((tq, L), F32)],
        compiler_params=pltpu.CompilerParams(
            dimension_semantics=("parallel", "arbitrary"), vmem_limit_bytes=VMEM_LIMIT),
    )(q, k, v, iq, ik, iw)


def _outproj_kernel(x_ref, u_ref, halo_ref, a_ref, gin_ref, bin_ref, wpool_ref, pscale_ref,
                    wout_ref, g1_ref, b1_ref, wr_hi_ref, wr_lo_ref, br_ref,
                    h1_ref, h1b_ref, ids_ref, gates_ref, *, tq, tiles_per_seq):
    i = pl.program_id(0)
    seq_tile = i % tiles_per_seq
    u = u_ref[...]
    halo = jnp.where(seq_tile == 0, 0.0, halo_ref[...])
    ext = jnp.concatenate([halo, u], axis=0)
    lane = lax.broadcasted_iota(jnp.int32, (tq, POOL_WIDTH), 1)
    grp = lane // POOL_GDIM
    win = jnp.zeros((tq, POOL_WIDTH), F32)
    s = ext
    for gi, w in enumerate(POOL_WINDOWS):
        s = s + pltpu.roll(s, w // 2, 0)
        win = jnp.where(grp == gi, s[POOL_HALO:, :], win)
    t_seq = seq_tile * tq + lax.broadcasted_iota(jnp.int32, (tq, POOL_WIDTH), 0)
    width = lax.shift_left(jnp.int32(2), grp)
    cnt = jnp.minimum(t_seq + 1, width).astype(F32)
    d = win / cnt - u
    y_pool = jnp.dot(d.astype(BF16), wpool_ref[...], preferred_element_type=F32) * pscale_ref[...]

    mix = jnp.dot(y_pool.astype(BF16), wout_ref[0:POOL_WIDTH, :], preferred_element_type=F32)
    mix = mix + jnp.dot(a_ref[...], wout_ref[POOL_WIDTH:, :], preferred_element_type=F32)
    h = _layer_norm(x_ref[...], gin_ref[...], bin_ref[...])
    h1 = _layer_norm(ALPHA_RES * h + mix, g1_ref[...], b1_ref[...])
    h1_ref[...] = h1
    h1_hi = h1.astype(BF16)
    h1b_ref[...] = h1_hi
    h1_lo = (h1 - h1_hi.astype(F32)).astype(BF16)
    wr_hi = wr_hi_ref[...]
    logits = (jnp.dot(h1_hi, wr_hi, preferred_element_type=F32)
              + jnp.dot(h1_lo, wr_hi, preferred_element_type=F32)
              + jnp.dot(h1_hi, wr_lo_ref[...], preferred_element_type=F32)) + br_ref[...]
    lane_e = lax.broadcasted_iota(jnp.int32, (tq, LANES), 1).astype(F32)
    lg = jnp.where(lane_e < N_EXPERTS, logits, -jnp.inf)
    ids = jnp.zeros((tq, LANES), F32)
    ex = jnp.zeros((tq, LANES), F32)
    top0 = None
    for kk in range(TOP_K):
        m = jnp.max(lg, axis=-1, keepdims=True)
        idx = jnp.min(jnp.where(lg == m, lane_e, float(LANES)), axis=-1, keepdims=True)
        if top0 is None:
            top0 = m
        ids = jnp.where(lane_e == kk, idx, ids)
        ex = jnp.where(lane_e == kk, jnp.exp(m - top0), ex)
        lg = jnp.where(lane_e == idx, -jnp.inf, lg)
    ids_ref[...] = ids.astype(jnp.int32)
    gates_ref[...] = ex / jnp.sum(ex, axis=-1, keepdims=True)


def _outproj(x2, u, attn, gin, bin_, wpool_bd, pscale, wout, g1, b1, wr_hi, wr_lo, br, L, tq):
    T = x2.shape[0]
    row = lambda i: (i, 0)
    fixed = lambda i: (0, 0)
    halo = lambda i: (jnp.maximum(i * (tq // POOL_HALO) - 1, 0), 0)
    out_shapes = (
        jax.ShapeDtypeStruct((T, D_MODEL), F32),
        jax.ShapeDtypeStruct((T, D_MODEL), BF16),
        jax.ShapeDtypeStruct((T, LANES), jnp.int32),
        jax.ShapeDtypeStruct((T, LANES), F32),
    )
    return pl.pallas_call(
        functools.partial(_outproj_kernel, tq=tq, tiles_per_seq=L // tq),
        grid=(T // tq,),
        in_specs=[
            pl.BlockSpec((tq, D_MODEL), row),
            pl.BlockSpec((tq, POOL_WIDTH), row),
            pl.BlockSpec((POOL_HALO, POOL_WIDTH), halo),
            pl.BlockSpec((tq, ATTN_WIDTH), row),
            pl.BlockSpec((1, D_MODEL), fixed),
            pl.BlockSpec((1, D_MODEL), fixed),
            pl.BlockSpec((POOL_WIDTH, POOL_WIDTH), fixed),
            pl.BlockSpec((1, POOL_WIDTH), fixed),
            pl.BlockSpec((D_MODEL, D_MODEL), fixed),
            pl.BlockSpec((1, D_MODEL), fixed),
            pl.BlockSpec((1, D_MODEL), fixed),
            pl.BlockSpec((D_MODEL, LANES), fixed),
            pl.BlockSpec((D_MODEL, LANES), fixed),
            pl.BlockSpec((1, LANES), fixed),
        ],
        out_specs=tuple(pl.BlockSpec((tq, s.shape[1]), row) for s in out_shapes),
        out_shape=out_shapes,
        compiler_params=pltpu.CompilerParams(
            dimension_semantics=("parallel",), vmem_limit_bytes=VMEM_LIMIT),
    )(x2, u, u, attn, gin, bin_, wpool_bd, pscale, wout, g1, b1, wr_hi, wr_lo, br)


ROUTE_TB = 512
ROW_ALIGN = 16
RUN_BITS = 6
STAGE_ROWS = 2560
STAGE_CHUNK = 512
COMBINE_CHUNK = 1280
FFN_TM = 512
TAIL_BITS = 5


def _route_kernel(ids_ref, slot_ref, slot_t_ref, meta_ref, carry_ref):
    c = pl.program_id(0)

    @pl.when(c == 0)
    def _():
        carry_ref[...] = jnp.zeros(carry_ref.shape, F32)

    tb = ids_ref.shape[0]
    ids = ids_ref[...]
    lane = lax.broadcasted_iota(jnp.int32, (tb, LANES), 1)
    onehot = [jnp.where(lane == ids[:, k:k + 1], 1.0, 0.0) for k in range(TOP_K)]
    member = onehot[0] + onehot[1] + onehot[2] + onehot[3]
    r = lax.broadcasted_iota(jnp.int32, (tb, tb), 0)
    cc = lax.broadcasted_iota(jnp.int32, (tb, tb), 1)
    before = jnp.where(cc < r, 1.0, 0.0).astype(BF16)
    lrank = jnp.dot(before, member.astype(BF16), preferred_element_type=F32)
    n = jnp.sum(member, axis=0, keepdims=True)
    units = jnp.ceil(n * (1.0 / ROW_ALIGN))
    er = lax.broadcasted_iota(jnp.int32, (LANES, LANES), 0)
    ec = lax.broadcasted_iota(jnp.int32, (LANES, LANES), 1)
    lower = jnp.where(er < ec, 1.0, 0.0).astype(BF16)
    off_units = jnp.dot(jnp.broadcast_to(units, (8, LANES)).astype(BF16), lower,
                        preferred_element_type=F32)[0:1, :]
    base = off_units * ROW_ALIGN + lrank
    slot = jnp.zeros((tb, LANES), jnp.int32)
    for k in range(TOP_K):
        sk = jnp.sum(onehot[k] * base, axis=-1, keepdims=True).astype(jnp.int32)
        slot = jnp.where(lane == k, sk, slot)
    slot_ref[...] = slot
    slot_t_ref[...] = slot.T[0:8, :]
    row = lax.broadcasted_iota(jnp.int32, (8, LANES), 0)
    meta = jnp.where(row == 0, units, jnp.where(row == 1, off_units, carry_ref[...]))
    meta_ref[...] = meta.astype(jnp.int32)
    carry_ref[...] = carry_ref[...] + units


def _route(ids):
    T = ids.shape[0]
    nblk = T // ROUTE_TB
    return pl.pallas_call(
        _route_kernel,
        grid=(nblk,),
        in_specs=[pl.BlockSpec((ROUTE_TB, LANES), lambda c: (c, 0))],
        out_specs=(pl.BlockSpec((ROUTE_TB, LANES), lambda c: (c, 0)),
                   pl.BlockSpec((8, ROUTE_TB), lambda c: (0, c)),
                   pl.BlockSpec((8, LANES), lambda c: (c, 0))),
        out_shape=(jax.ShapeDtypeStruct((T, LANES), jnp.int32),
                   jax.ShapeDtypeStruct((8, T), jnp.int32),
                   jax.ShapeDtypeStruct((nblk * 8, LANES), jnp.int32)),
        scratch_shapes=[pltpu.VMEM((1, LANES), F32)],
        compiler_params=pltpu.CompilerParams(
            dimension_semantics=("arbitrary",), vmem_limit_bytes=VMEM_LIMIT),
    )(ids)


def _pieces_of(m, so, do, bits, visit):
    for b in bits:
        done = m & ((1 << b) - 1)

        @pl.when(((m >> b) & 1) == 1)
        def _(b=b, done=done):
            visit(b, pl.multiple_of((so + done) * ROW_ALIGN, ROW_ALIGN),
                  pl.multiple_of((do + done) * ROW_ALIGN, ROW_ALIGN), ROW_ALIGN << b)


def _piece_lists(units, off_units, dst_units):
    bits = jnp.arange(RUN_BITS, dtype=jnp.int32)[None, :, None]
    m = units[:, None, :]
    has = ((m >> bits) & 1) == 1
    done = m & ((1 << bits) - 1)
    pos = jnp.cumsum(has, axis=-1) - 1
    at = has[..., None] & (pos[..., None] == jnp.arange(N_EXPERTS, dtype=jnp.int32))
    pick = lambda v: jnp.sum(jnp.where(at, v[..., None], 0), axis=2).reshape(-1).astype(jnp.int32)
    cnt = jnp.sum(has, axis=-1).reshape(-1).astype(jnp.int32)
    return cnt, pick(off_units[:, None, :] + done), pick(dst_units[:, None, :] + done)


def _run_pieces(cnt_ref, srow_ref, grow_ref, c, visit):
    for b in range(RUN_BITS):
        base = (c * RUN_BITS + b) * N_EXPERTS

        def body(j, carry, b=b, base=base):
            visit(b, j, pl.multiple_of(srow_ref[base + j] * ROW_ALIGN, ROW_ALIGN),
                  pl.multiple_of(grow_ref[base + j] * ROW_ALIGN, ROW_ALIGN), ROW_ALIGN << b)
            return carry

        lax.fori_loop(0, cnt_ref[c * RUN_BITS + b], body, 0)


def _dispatch_kernel(cnt_ref, srow_ref, grow_ref, tail_ref, tail_dst_ref, slot_t_ref, h_ref,
                     xs_ref, stage_ref, zero_ref, sems, *, nblk):
    c = pl.program_id(0)
    par = c % 2
    tb = h_ref.shape[0]
    for r0 in range(0, STAGE_ROWS, STAGE_CHUNK):
        srow = r0 + lax.broadcasted_iota(jnp.int32, (STAGE_CHUNK, tb), 0)
        hit = srow == slot_t_ref[0:1, :]
        for k in range(1, TOP_K):
            hit = hit | (srow == slot_t_ref[k:k + 1, :])
        perm = jnp.where(hit, 1.0, 0.0).astype(BF16)
        stage_ref[par, r0:r0 + STAGE_CHUNK, :] = jnp.dot(
            perm, h_ref[...], preferred_element_type=F32).astype(BF16)

    def piece(p):
        return lambda b, j, s_row, g_row, rows: pltpu.make_async_copy(
            stage_ref.at[p, pl.ds(s_row, rows), :], xs_ref.at[pl.ds(g_row, rows), :],
            sems.at[p, j, b])

    _run_pieces(cnt_ref, srow_ref, grow_ref, c, lambda *a: piece(par)(*a).start())

    @pl.when(c > 0)
    def _():
        _run_pieces(cnt_ref, srow_ref, grow_ref, c - 1, lambda *a: piece(1 - par)(*a).wait())

    @pl.when(c == nblk - 1)
    def _():
        _run_pieces(cnt_ref, srow_ref, grow_ref, c, lambda *a: piece(par)(*a).wait())
        zero_ref[...] = jnp.zeros(zero_ref.shape, BF16)

        def tail(e, b, z_row, g_row, rows):
            del z_row
            return pltpu.make_async_copy(zero_ref.at[pl.ds(0, rows), :],
                                         xs_ref.at[pl.ds(g_row, rows), :], sems.at[0, e, b])

        def tails(act):
            for e in range(N_EXPERTS):
                _pieces_of(tail_ref[e], 0, tail_dst_ref[e], range(TAIL_BITS),
                           lambda b, z, g, rows, e=e: act(tail(e, b, z, g, rows)))

        tails(lambda cp: cp.start())
        tails(lambda cp: cp.wait())

        zrows = zero_ref.shape[0]
        used = (tail_dst_ref[N_EXPERTS - 1] + tail_ref[N_EXPERTS - 1]) * ROW_ALIGN
        n_rest = (xs_ref.shape[0] - used) // zrows

        def rest(i):
            row0 = pl.multiple_of(used + i * zrows, ROW_ALIGN)
            return pltpu.make_async_copy(
                zero_ref, xs_ref.at[pl.ds(row0, zrows), :],
                sems.at[i // (N_EXPERTS * RUN_BITS), (i // RUN_BITS) % N_EXPERTS, i % RUN_BITS])

        lax.fori_loop(0, n_rest, lambda i, _: (rest(i).start(), 0)[1], 0)
        lax.fori_loop(0, n_rest, lambda i, _: (rest(i).wait(), 0)[1], 0)


def _dispatch(units, off, dst, tail, tail_dst, slot_t, h1b, n_rows):
    T = h1b.shape[0]
    nblk = T // ROUTE_TB
    grid_spec = pltpu.PrefetchScalarGridSpec(
        num_scalar_prefetch=5,
        grid=(nblk,),
        in_specs=[pl.BlockSpec((8, ROUTE_TB), lambda c, *_: (0, c)),
                  pl.BlockSpec((ROUTE_TB, D_MODEL), lambda c, *_: (c, 0))],
        out_specs=pl.BlockSpec(memory_space=pl.ANY),
        scratch_shapes=[pltpu.VMEM((2, STAGE_ROWS, D_MODEL), BF16),
                        pltpu.VMEM((ROW_ALIGN << (TAIL_BITS - 1), D_MODEL), BF16),
                        pltpu.SemaphoreType.DMA((2, N_EXPERTS, RUN_BITS))],
    )
    return pl.pallas_call(
        functools.partial(_dispatch_kernel, nblk=nblk),
        grid_spec=grid_spec,
        out_shape=jax.ShapeDtypeStruct((n_rows, D_MODEL), BF16),
        compiler_params=pltpu.CompilerParams(
            dimension_semantics=("arbitrary",), vmem_limit_bytes=VMEM_LIMIT),
    )(units, off, dst, tail, tail_dst, slot_t, h1b)


def _ffn_kernel(tile_e_ref, n_tiles_ref, xs_ref, wgu_ref, bgu_ref, wd_ref, bd_ref, y_ref,
                wgu_b, wd_b, *, n_chunk):
    j = pl.program_id(0)
    active = j < n_tiles_ref[0]
    new_expert = (j == 0) | (tile_e_ref[j] != tile_e_ref[jnp.maximum(j - 1, 0)])

    def tile(refresh):
        def weights(dst, src, rows, cols):
            if refresh:
                dst[rows, cols] = src[0, rows, cols].astype(BF16)
            return dst[rows, cols]

        xs = xs_ref[...]
        cw = D_FF // n_chunk
        full = slice(None)
        acc = jnp.zeros(y_ref.shape, F32)
        for c in range(n_chunk):
            cols = slice(c * cw, (c + 1) * cw)
            ucols = slice(D_FF + c * cw, D_FF + (c + 1) * cw)
            gate = jnp.dot(xs, weights(wgu_b, wgu_ref, full, cols), preferred_element_type=F32)
            gate = gate + bgu_ref[0, :, cols]
            up = jnp.dot(xs, weights(wgu_b, wgu_ref, full, ucols), preferred_element_type=F32)
            up = up + bgu_ref[0, :, ucols]
            gate = jnp.minimum(gate, SWIGLU_LIMIT)
            up = jnp.clip(up, -SWIGLU_LIMIT, SWIGLU_LIMIT)
            act = (up + 1.0) * gate * jax.nn.sigmoid(SWIGLU_ALPHA * gate)
            acc = acc + jnp.dot(act.astype(BF16), weights(wd_b, wd_ref, cols, full),
                                preferred_element_type=F32)
        y_ref[...] = (acc + bd_ref[0]).astype(BF16)

    pl.when(active & new_expert)(lambda: tile(True))
    pl.when(active & jnp.logical_not(new_expert))(lambda: tile(False))

    @pl.when(jnp.logical_not(active))
    def _():
        y_ref[...] = jnp.zeros(y_ref.shape, BF16)


def _ffn(tile_e, n_tiles, xs, wgu, bgu, wd, bd):
    n_rows = xs.shape[0]
    last = lambda j, nt: jnp.minimum(j, nt[0] - 1)
    row = lambda j, te, nt: (last(j, nt), 0)
    exp3 = lambda j, te, nt: (te[last(j, nt)], 0, 0)
    grid_spec = pltpu.PrefetchScalarGridSpec(
        num_scalar_prefetch=2,
        grid=(n_rows // FFN_TM,),
        in_specs=[
            pl.BlockSpec((FFN_TM, D_MODEL), row),
            pl.BlockSpec((1, D_MODEL, 2 * D_FF), exp3),
            pl.BlockSpec((1, 1, 2 * D_FF), exp3),
            pl.BlockSpec((1, D_FF, D_MODEL), exp3),
            pl.BlockSpec((1, 1, D_MODEL), exp3),
        ],
        out_specs=pl.BlockSpec((FFN_TM, D_MODEL), lambda j, te, nt: (j, 0)),
        scratch_shapes=[pltpu.VMEM((D_MODEL, 2 * D_FF), BF16), pltpu.VMEM((D_FF, D_MODEL), BF16)],
    )
    return pl.pallas_call(
        functools.partial(_ffn_kernel, n_chunk=2),
        grid_spec=grid_spec,
        out_shape=jax.ShapeDtypeStruct((n_rows, D_MODEL), BF16),
        compiler_params=pltpu.CompilerParams(
            dimension_semantics=("arbitrary",), vmem_limit_bytes=FFN_VMEM_LIMIT),
    )(tile_e, n_tiles, xs, wgu, bgu, wd, bd)


def _combine_kernel(cnt_ref, srow_ref, grow_ref, slot_ref, gates_ref, h1_ref, ys_ref,
                    g_ref, b_ref, o_ref, stage_ref, sems, *, nblk):
    c = pl.program_id(0)
    par = c % 2
    tb = h1_ref.shape[0]

    def piece(p):
        return lambda b, j, s_row, g_row, rows: pltpu.make_async_copy(
            ys_ref.at[pl.ds(g_row, rows), :], stage_ref.at[p, pl.ds(s_row, rows), :],
            sems.at[p, j, b])

    @pl.when(c == 0)
    def _():
        stage_ref[...] = jnp.zeros(stage_ref.shape, BF16)
        _run_pieces(cnt_ref, srow_ref, grow_ref, c, lambda *a: piece(0)(*a).start())

    _run_pieces(cnt_ref, srow_ref, grow_ref, c, lambda *a: piece(par)(*a).wait())

    @pl.when(c + 1 < nblk)
    def _():
        _run_pieces(cnt_ref, srow_ref, grow_ref, c + 1, lambda *a: piece(1 - par)(*a).start())

    slot = slot_ref[...]
    gates = gates_ref[...]
    ffn = jnp.zeros((tb, D_MODEL), F32)
    for r0 in range(0, STAGE_ROWS, COMBINE_CHUNK):
        scol = r0 + lax.broadcasted_iota(jnp.int32, (tb, COMBINE_CHUNK), 1)
        w = jnp.zeros((tb, COMBINE_CHUNK), F32)
        for k in range(TOP_K):
            w = w + jnp.where(scol == slot[:, k:k + 1], gates[:, k:k + 1], 0.0)
        ffn = ffn + jnp.dot(w.astype(BF16), stage_ref[par, r0:r0 + COMBINE_CHUNK, :],
                            preferred_element_type=F32)
    o_ref[...] = _layer_norm(ALPHA_RES * h1_ref[...] + ffn, g_ref[...], b_ref[...])


def _combine(units, off, dst, slot, gates, h1, ys, g, b):
    T = h1.shape[0]
    nblk = T // ROUTE_TB
    blk = lambda c, *_: (c, 0)
    fixed = lambda c, *_: (0, 0)
    grid_spec = pltpu.PrefetchScalarGridSpec(
        num_scalar_prefetch=3,
        grid=(nblk,),
        in_specs=[pl.BlockSpec((ROUTE_TB, LANES), blk),
                  pl.BlockSpec((ROUTE_TB, LANES), blk),
                  pl.BlockSpec((ROUTE_TB, D_MODEL), blk),
                  pl.BlockSpec(memory_space=pl.ANY),
                  pl.BlockSpec((1, D_MODEL), fixed),
                  pl.BlockSpec((1, D_MODEL), fixed)],
        out_specs=pl.BlockSpec((ROUTE_TB, D_MODEL), blk),
        scratch_shapes=[pltpu.VMEM((2, STAGE_ROWS, D_MODEL), BF16),
                        pltpu.SemaphoreType.DMA((2, N_EXPERTS, RUN_BITS))],
    )
    return pl.pallas_call(
        functools.partial(_combine_kernel, nblk=nblk),
        grid_spec=grid_spec,
        out_shape=jax.ShapeDtypeStruct((T, D_MODEL), F32),
        compiler_params=pltpu.CompilerParams(
            dimension_semantics=("arbitrary",), vmem_limit_bytes=VMEM_LIMIT),
    )(units, off, dst, slot, gates, h1, ys, g, b)


def kernel(x, positions, ln_in_g, ln_in_b, w_in, w_pool, pool_scale, w_out, ln1_g, ln1_b,
           w_router, b_router, w_gate_up, b_gate_up, w_down, b_down, ln2_g, ln2_b):
    B, L, D = x.shape
    T = B * L
    assert T % ROUTE_TB == 0 and D == D_MODEL
    tq = min(256, L)
    x2 = x.reshape(T, D)
    pos2 = positions.reshape(1, T)
    gin = ln_in_g.reshape(1, D)
    bin_ = ln_in_b.reshape(1, D)

    w_pad = jnp.pad(w_in[0], ((0, 0), (0, IN_WIDTH_PAD - IN_WIDTH))).astype(BF16)
    u, q, k, v, iq, ik, iw = _inproj(x2, pos2, gin, bin_, w_pad, tq)

    attn = _attn(q, k, v, iq, ik, iw, B, L, min(256, L // CAUSAL_BANDS))

    wpool_bd = jnp.zeros((POOL_WIDTH, POOL_WIDTH), F32)
    for gi in range(POOL_GROUPS):
        sl = slice(gi * POOL_GDIM, (gi + 1) * POOL_GDIM)
        wpool_bd = wpool_bd.at[sl, sl].set(w_pool[0, gi])
    wr = jnp.pad(w_router[0], ((0, 0), (0, LANES - N_EXPERTS)))
    wr_hi = wr.astype(BF16)
    wr_lo = (wr - wr_hi.astype(F32)).astype(BF16)
    br = jnp.pad(b_router[0], (0, LANES - N_EXPERTS)).reshape(1, LANES)
    h1, h1b, ids, gates = _outproj(
        x2, u, attn, gin, bin_, wpool_bd.astype(BF16), pool_scale[0].reshape(1, POOL_WIDTH),
        w_out[0].astype(BF16), ln1_g[0].reshape(1, D), ln1_b[0].reshape(1, D),
        wr_hi, wr_lo, br, L, tq)

    nblk = T // ROUTE_TB
    slot, slot_t, meta = _route(ids)
    meta = meta.reshape(nblk, 8, LANES)[:, :, :N_EXPERTS]
    units, off_units, base_units = meta[:, 0], meta[:, 1], meta[:, 2]
    tile_units = FFN_TM // ROW_ALIGN
    total_units = base_units[-1] + units[-1]
    region_units = ((total_units + tile_units - 1) // tile_units) * tile_units
    region_end = jnp.cumsum(region_units)
    dst_units = (region_end - region_units)[None, :] + base_units
    max_rows = T * TOP_K + nblk * N_EXPERTS * (ROW_ALIGN - 1) + N_EXPERTS * (FFN_TM - 1)
    n_rows = -(-max_rows // FFN_TM) * FFN_TM
    tile_start = jnp.arange(n_rows // FFN_TM, dtype=jnp.int32) * tile_units
    tile_e = jnp.minimum(jnp.sum(tile_start[:, None] >= region_end[None, :], axis=1),
                         N_EXPERTS - 1).astype(jnp.int32)
    n_tiles = (region_end[-1:] // tile_units).astype(jnp.int32)
    cnt, srow, grow = _piece_lists(units, off_units, dst_units)
    tail_units = (region_units - total_units).astype(jnp.int32)
    tail_dst = (region_end - tail_units).astype(jnp.int32)

    xs = _dispatch(cnt, srow, grow, tail_units, tail_dst, slot_t, h1b, n_rows)
    ys = _ffn(tile_e, n_tiles, xs, w_gate_up[0], b_gate_up[0].reshape(N_EXPERTS, 1, 2 * D_FF),
              w_down[0], b_down[0].reshape(N_EXPERTS, 1, D))
    out = _combine(cnt, srow, grow, slot, gates, h1, ys,
                   ln2_g[0].reshape(1, D), ln2_b[0].reshape(1, D))
    return out.reshape(B, L, D)
```

```python
import functools

import jax
import jax.numpy as jnp
from jax import lax
from jax.experimental import pallas as pl
from jax.experimental.pallas import tpu as pltpu

F32 = jnp.float32
BF16 = jnp.bfloat16

D_MODEL = 1024
POOL_WIDTH = 256
POOL_GROUPS = 4
POOL_GDIM = 64
POOL_WINDOWS = (2, 4, 8, 16)
POOL_HALO = 16
HEAD_DIM = 128
ATTN_WIDTH = 768
N_HEADS = 6
N_KV_HEADS = 2
KV_WIDTH = 256
IDX_HEADS = 8
IDX_DIM = 64
TOPK_MAX = 256
ROPE_THETA = 500000.0
ROPE_DIM = 32
IDX_ROPE_DIM = 16
N_EXPERTS = 32
TOP_K = 4
D_FF = 1024
SWIGLU_LIMIT = 7.0
SWIGLU_ALPHA = 1.702
DEPTH = 1
ALPHA_RES = (2.0 * DEPTH) ** 0.25
LN_EPS = 1e-5
NEG_INF = -1e30
OFF_Q = 256
OFF_K = 1024
OFF_V = 1280
OFF_IQ = 1536
OFF_IK = 2048
IN_WIDTH = 2120
IN_WIDTH_PAD = 2176

LANES = 128
VMEM_LIMIT = 48 * 1024 * 1024
FFN_VMEM_LIMIT = 56 * 1024 * 1024
INT_MIN = -2 ** 31
TIE_BLOCK = 256
CAUSAL_BANDS = 4
SEARCH_UNROLL = 8


def _layer_norm(x, g, b):
    mu = jnp.mean(x, axis=-1, keepdims=True)
    xc = x - mu
    var = jnp.mean(xc * xc, axis=-1, keepdims=True)
    return xc * lax.rsqrt(var + LN_EPS) * g + b


def _rope(xh, cos, sin, first_half, half):
    partner = jnp.where(first_half, pltpu.roll(xh, LANES - half, 1), pltpu.roll(xh, half, 1))
    return xh * cos + partner * sin


def _inproj_kernel(x_ref, pos_ref, g_ref, b_ref, w_ref, freq_ref,
                   u_ref, q_ref, k_ref, v_ref, iq_ref, ik_ref, iw_ref):
    tq = x_ref.shape[0]
    h = _layer_norm(x_ref[...], g_ref[...], b_ref[...])
    proj = jnp.dot(h.astype(BF16), w_ref[...], preferred_element_type=F32)

    pos = pos_ref[...].astype(F32)
    ang_q = freq_ref[0:ROPE_DIM // 2, :] * pos
    ang_i = freq_ref[ROPE_DIM // 2:ROPE_DIM // 2 + IDX_ROPE_DIM // 2, :] * pos
    cq, sq = jnp.cos(ang_q), jnp.sin(ang_q)
    ci, si = jnp.cos(ang_i), jnp.sin(ang_i)
    rot = jnp.concatenate(
        [cq, cq, -sq, sq, ci, ci, -si, si, jnp.zeros((LANES - 96, tq), F32)], axis=0)
    c = rot.T
    lane = lax.broadcasted_iota(jnp.int32, (tq, LANES), 1)
    in_q = lane < ROPE_DIM
    cos_q = jnp.where(in_q, c, 1.0)
    sin_q = jnp.where(in_q, pltpu.roll(c, LANES - 32, 1), 0.0)
    first_i = lane < IDX_ROPE_DIM
    second_i = (lane >= IDX_DIM) & (lane < IDX_DIM + IDX_ROPE_DIM)
    cos_i = jnp.where(first_i, pltpu.roll(c, LANES - 64, 1),
                      jnp.where(second_i, c, 1.0))
    sin_i = jnp.where(first_i, pltpu.roll(c, LANES - 80, 1),
                      jnp.where(second_i, pltpu.roll(c, LANES - 16, 1), 0.0))
    half_q = lane < ROPE_DIM // 2
    half_i = (lane % IDX_DIM) < IDX_ROPE_DIM // 2

    u_ref[...] = proj[:, 0:OFF_Q]
    scale = HEAD_DIM ** -0.5
    for hh in range(N_HEADS):
        c0 = OFF_Q + hh * HEAD_DIM
        r = _rope(proj[:, c0:c0 + HEAD_DIM], cos_q, sin_q, half_q, ROPE_DIM // 2)
        q_ref[:, hh * HEAD_DIM:(hh + 1) * HEAD_DIM] = (r * scale).astype(BF16)
    for hh in range(N_KV_HEADS):
        c0 = OFF_K + hh * HEAD_DIM
        r = _rope(proj[:, c0:c0 + HEAD_DIM], cos_q, sin_q, half_q, ROPE_DIM // 2)
        k_ref[:, hh * HEAD_DIM:(hh + 1) * HEAD_DIM] = r.astype(BF16)
    ones = jnp.ones((tq, HEAD_DIM), BF16)
    for hh in range(N_KV_HEADS):
        c0 = OFF_V + hh * HEAD_DIM
        v_ref[:, 2 * hh * HEAD_DIM:(2 * hh + 1) * HEAD_DIM] = proj[:, c0:c0 + HEAD_DIM].astype(BF16)
        v_ref[:, (2 * hh + 1) * HEAD_DIM:(2 * hh + 2) * HEAD_DIM] = ones
    for t in range(IDX_HEADS * IDX_DIM // LANES):
        c0 = OFF_IQ + t * LANES
        r = _rope(proj[:, c0:c0 + LANES], cos_i, sin_i, half_i, IDX_ROPE_DIM // 2)
        iq_ref[:, t * LANES:(t + 1) * LANES] = r.astype(BF16)
    tail = proj[:, OFF_IK:OFF_IK + LANES]
    r = _rope(tail, cos_i, sin_i, half_i, IDX_ROPE_DIM // 2)
    ik_ref[...] = r[:, 0:IDX_DIM].astype(BF16)
    iw_ref[...] = tail[:, IDX_DIM:IDX_DIM + IDX_HEADS]


def _rope_freqs():
    f_q = ROPE_THETA ** (-jnp.arange(0, ROPE_DIM, 2, dtype=F32) / ROPE_DIM)
    f_i = ROPE_THETA ** (-jnp.arange(0, IDX_ROPE_DIM, 2, dtype=F32) / IDX_ROPE_DIM)
    return jnp.concatenate([f_q, f_i]).reshape(-1, 1)


def _inproj(x2, pos2, g, b, w_pad, tq):
    T = x2.shape[0]
    row = lambda i: (i, 0)
    fixed = lambda i: (0, 0)
    out_shapes = (
        jax.ShapeDtypeStruct((T, POOL_WIDTH), F32),
        jax.ShapeDtypeStruct((T, ATTN_WIDTH), BF16),
        jax.ShapeDtypeStruct((T, KV_WIDTH), BF16),
        jax.ShapeDtypeStruct((T, 2 * KV_WIDTH), BF16),
        jax.ShapeDtypeStruct((T, IDX_HEADS * IDX_DIM), BF16),
        jax.ShapeDtypeStruct((T, IDX_DIM), BF16),
        jax.ShapeDtypeStruct((T, IDX_HEADS), F32),
    )
    return pl.pallas_call(
        _inproj_kernel,
        grid=(T // tq,),
        in_specs=[
            pl.BlockSpec((tq, D_MODEL), row),
            pl.BlockSpec((1, tq), lambda i: (0, i)),
            pl.BlockSpec((1, D_MODEL), fixed),
            pl.BlockSpec((1, D_MODEL), fixed),
            pl.BlockSpec((D_MODEL, IN_WIDTH_PAD), fixed),
            pl.BlockSpec((ROPE_DIM // 2 + IDX_ROPE_DIM // 2, 1), fixed),
        ],
        out_specs=tuple(pl.BlockSpec((tq, s.shape[1]), row) for s in out_shapes),
        out_shape=out_shapes,
        compiler_params=pltpu.CompilerParams(
            dimension_semantics=("parallel",), vmem_limit_bytes=VMEM_LIMIT),
    )(x2, pos2, g, b, w_pad, _rope_freqs())


def _row_count(mask):
    return jnp.sum(jnp.where(mask, 1.0, 0.0), axis=-1, keepdims=True)


def _select_bias(iq_ref, ik_ref, iw_ref, key_ref, bias_ref, *, i, tq, S, n_sel):
    q_pos = i * tq + lax.broadcasted_iota(jnp.int32, (tq, S), 0)
    k_pos = lax.broadcasted_iota(jnp.int32, (tq, S), 1)
    causal = k_pos <= q_pos
    if S <= n_sel:
        bias_ref[:, 0:S] = jnp.where(causal, 0.0, NEG_INF)
        return

    w_scale = (IDX_HEADS ** -0.5) * (IDX_DIM ** -0.5)
    iw = iw_ref[...] * w_scale
    ik = ik_ref[0:S, :]
    nt = (((1,), (1,)), ((), ()))
    sc = jnp.zeros((tq, S), F32)
    for hh in range(IDX_HEADS):
        s = lax.dot_general(iq_ref[:, hh * IDX_DIM:(hh + 1) * IDX_DIM], ik, nt,
                            preferred_element_type=F32)
        sc = sc + jnp.maximum(s, 0.0) * iw[:, hh:hh + 1]
    sc = jnp.where(sc == 0.0, 0.0, sc)
    sc = jnp.where(causal, sc, NEG_INF)
    bits = pltpu.bitcast(sc, jnp.int32)
    key_ref[:, 0:S] = bits ^ ((bits >> 31) & jnp.int32(0x7FFFFFFF))

    def value_step(b, t_u):
        cand = t_u | lax.shift_left(jnp.int32(1), 31 - b)
        cnt = _row_count(key_ref[:, 0:S] >= (cand ^ jnp.int32(INT_MIN)))
        return jnp.where(cnt >= n_sel, cand, t_u)

    def value_steps(bb, t_u):
        for r in range(SEARCH_UNROLL):
            t_u = value_step(bb * SEARCH_UNROLL + r, t_u)
        return t_u

    t_u = lax.fori_loop(0, 32 // SEARCH_UNROLL, value_steps, jnp.zeros((tq, 1), jnp.int32))
    thr = t_u ^ jnp.int32(INT_MIN)
    key = key_ref[:, 0:S]
    need = n_sel - _row_count(key > thr)
    tied = key == thr
    bias_ref[:, 0:S] = jnp.where((key >= thr) & causal, 0.0, NEG_INF)
    excess = _row_count(tied & causal) > need
    any_excess = jnp.max(jnp.where(excess, 1.0, 0.0)) > 0.0

    @pl.when(any_excess)
    def _():
        r = lax.broadcasted_iota(jnp.int32, (TIE_BLOCK, TIE_BLOCK), 0)
        cc = lax.broadcasted_iota(jnp.int32, (TIE_BLOCK, TIE_BLOCK), 1)
        upto = jnp.where(r <= cc, 1.0, 0.0).astype(BF16)
        qp = i * tq + lax.broadcasted_iota(jnp.int32, (tq, TIE_BLOCK), 0)
        carry = jnp.zeros((tq, 1), F32)
        for c0 in range(0, S, TIE_BLOCK):
            kb = key_ref[:, c0:c0 + TIE_BLOCK]
            cz = (c0 + lax.broadcasted_iota(jnp.int32, (tq, TIE_BLOCK), 1)) <= qp
            tb = (kb == thr) & cz
            seen = jnp.dot(jnp.where(tb, 1.0, 0.0).astype(BF16), upto,
                           preferred_element_type=F32) + carry
            admit = ((kb > thr) & cz) | (tb & (seen <= need))
            bias_ref[:, c0:c0 + TIE_BLOCK] = jnp.where(admit, 0.0, NEG_INF)
            carry = seen[:, TIE_BLOCK - 1:TIE_BLOCK]


def _attn_tile(q_ref, k_ref, v_ref, iq_ref, ik_ref, iw_ref, o_ref, key_ref, bias_ref,
               *, i, tq, S, n_sel):
    _select_bias(iq_ref, ik_ref, iw_ref, key_ref, bias_ref, i=i, tq=tq, S=S, n_sel=n_sel)
    nt = (((1,), (1,)), ((), ()))
    for hh in range(N_HEADS):
        g = hh // (N_HEADS // N_KV_HEADS)
        kg = k_ref[0:S, g * HEAD_DIM:(g + 1) * HEAD_DIM]
        vg = v_ref[0:S, 2 * g * HEAD_DIM:2 * (g + 1) * HEAD_DIM]
        logits = lax.dot_general(q_ref[:, hh * HEAD_DIM:(hh + 1) * HEAD_DIM], kg, nt,
                                 preferred_element_type=F32) + bias_ref[:, 0:S]
        m = jnp.max(logits, axis=-1, keepdims=True)
        p = jnp.exp((logits - m).astype(BF16))
        ol = jnp.dot(p, vg, preferred_element_type=F32)
        o = ol[:, 0:HEAD_DIM] / ol[:, HEAD_DIM:HEAD_DIM + 1]
        o_ref[:, hh * HEAD_DIM:(hh + 1) * HEAD_DIM] = o.astype(BF16)


def _attn_kernel(*refs, tq, L, n_sel):
    i = pl.program_id(1)
    band = L // CAUSAL_BANDS
    for v in range(CAUSAL_BANDS):
        @pl.when((i * tq) // band == v)
        def _(v=v):
            _attn_tile(*refs, i=i, tq=tq, S=(v + 1) * band, n_sel=n_sel)


def _attn(q, k, v, iq, ik, iw, B, L, tq):
    T = B * L
    nq = L // tq
    n_sel = min(TOPK_MAX, L // 4)
    assert L % (CAUSAL_BANDS * tq) == 0
    qrow = lambda b, i: (b * nq + i, 0)
    seq = lambda b, i: (b, 0)
    return pl.pallas_call(
        functools.partial(_attn_kernel, tq=tq, L=L, n_sel=n_sel),
        grid=(B, nq),
        in_specs=[
            pl.BlockSpec((tq, ATTN_WIDTH), qrow),
            pl.BlockSpec((L, KV_WIDTH), seq),
            pl.BlockSpec((L, 2 * KV_WIDTH), seq),
            pl.BlockSpec((tq, IDX_HEADS * IDX_DIM), qrow),
            pl.BlockSpec((L, IDX_DIM), seq),
            pl.BlockSpec((tq, IDX_HEADS), qrow),
        ],
        out_specs=pl.BlockSpec((tq, ATTN_WIDTH), qrow),
        out_shape=jax.ShapeDtypeStruct((T, ATTN_WIDTH), BF16),
        scratch_shapes=[pltpu.VMEM((tq, L), jnp.int32), pltpu.VMEM((tq, L), F32)],
        compiler_params=pltpu.CompilerParams(
            dimension_semantics=("parallel", "arbitrary"), vmem_limit_bytes=VMEM_LIMIT),
    )(q, k, v, iq, ik, iw)


def _outproj_kernel(x_ref, u_ref, halo_ref, a_ref, gin_ref, bin_ref, wpool_ref, pscale_ref,
                    wout_ref, g1_ref, b1_ref, wr_hi_ref, wr_lo_ref, br_ref,
                    h1_ref, h1b_ref, ids_ref, gates_ref, *, tq, tiles_per_seq):
    i = pl.program_id(0)
    seq_tile = i % tiles_per_seq
    u = u_ref[...]
    halo = jnp.where(seq_tile == 0, 0.0, halo_ref[...])
    ext = jnp.concatenate([halo, u], axis=0)
    lane = lax.broadcasted_iota(jnp.int32, (tq, POOL_WIDTH), 1)
    grp = lane // POOL_GDIM
    win = jnp.zeros((tq, POOL_WIDTH), F32)
    s = ext
    for gi, w in enumerate(POOL_WINDOWS):
        s = s + pltpu.roll(s, w // 2, 0)
        win = jnp.where(grp == gi, s[POOL_HALO:, :], win)
    t_seq = seq_tile * tq + lax.broadcasted_iota(jnp.int32, (tq, POOL_WIDTH), 0)
    width = lax.shift_left(jnp.int32(2), grp)
    cnt = jnp.minimum(t_seq + 1, width).astype(F32)
    d = win / cnt - u
    y_pool = jnp.dot(d.astype(BF16), wpool_ref[...], preferred_element_type=F32) * pscale_ref[...]

    mix = jnp.dot(y_pool.astype(BF16), wout_ref[0:POOL_WIDTH, :], preferred_element_type=F32)
    mix = mix + jnp.dot(a_ref[...], wout_ref[POOL_WIDTH:, :], preferred_element_type=F32)
    h = _layer_norm(x_ref[...], gin_ref[...], bin_ref[...])
    h1 = _layer_norm(ALPHA_RES * h + mix, g1_ref[...], b1_ref[...])
    h1_ref[...] = h1
    h1_hi = h1.astype(BF16)
    h1b_ref[...] = h1_hi
    h1_lo = (h1 - h1_hi.astype(F32)).astype(BF16)
    wr_hi = wr_hi_ref[...]
    logits = (jnp.dot(h1_hi, wr_hi, preferred_element_type=F32)
              + jnp.dot(h1_lo, wr_hi, preferred_element_type=F32)
              + jnp.dot(h1_hi, wr_lo_ref[...], preferred_element_type=F32)) + br_ref[...]
    lane_e = lax.broadcasted_iota(jnp.int32, (tq, LANES), 1).astype(F32)
    lg = jnp.where(lane_e < N_EXPERTS, logits, -jnp.inf)
    ids = jnp.zeros((tq, LANES), F32)
    ex = jnp.zeros((tq, LANES), F32)
    top0 = None
    for kk in range(TOP_K):
        m = jnp.max(lg, axis=-1, keepdims=True)
        idx = jnp.min(jnp.where(lg == m, lane_e, float(LANES)), axis=-1, keepdims=True)
        if top0 is None:
            top0 = m
        ids = jnp.where(lane_e == kk, idx, ids)
        ex = jnp.where(lane_e == kk, jnp.exp(m - top0), ex)
        lg = jnp.where(lane_e == idx, -jnp.inf, lg)
    ids_ref[...] = ids.astype(jnp.int32)
    gates_ref[...] = ex / jnp.sum(ex, axis=-1, keepdims=True)


def _outproj(x2, u, attn, gin, bin_, wpool_bd, pscale, wout, g1, b1, wr_hi, wr_lo, br, L, tq):
    T = x2.shape[0]
    row = lambda i: (i, 0)
    fixed = lambda i: (0, 0)
    halo = lambda i: (jnp.maximum(i * (tq // POOL_HALO) - 1, 0), 0)
    out_shapes = (
        jax.ShapeDtypeStruct((T, D_MODEL), F32),
        jax.ShapeDtypeStruct((T, D_MODEL), BF16),
        jax.ShapeDtypeStruct((T, LANES), jnp.int32),
        jax.ShapeDtypeStruct((T, LANES), F32),
    )
    return pl.pallas_call(
        functools.partial(_outproj_kernel, tq=tq, tiles_per_seq=L // tq),
        grid=(T // tq,),
        in_specs=[
            pl.BlockSpec((tq, D_MODEL), row),
            pl.BlockSpec((tq, POOL_WIDTH), row),
            pl.BlockSpec((POOL_HALO, POOL_WIDTH), halo),
            pl.BlockSpec((tq, ATTN_WIDTH), row),
            pl.BlockSpec((1, D_MODEL), fixed),
            pl.BlockSpec((1, D_MODEL), fixed),
            pl.BlockSpec((POOL_WIDTH, POOL_WIDTH), fixed),
            pl.BlockSpec((1, POOL_WIDTH), fixed),
            pl.BlockSpec((D_MODEL, D_MODEL), fixed),
            pl.BlockSpec((1, D_MODEL), fixed),
            pl.BlockSpec((1, D_MODEL), fixed),
            pl.BlockSpec((D_MODEL, LANES), fixed),
            pl.BlockSpec((D_MODEL, LANES), fixed),
            pl.BlockSpec((1, LANES), fixed),
        ],
        out_specs=tuple(pl.BlockSpec((tq, s.shape[1]), row) for s in out_shapes),
        out_shape=out_shapes,
        compiler_params=pltpu.CompilerParams(
            dimension_semantics=("parallel",), vmem_limit_bytes=VMEM_LIMIT),
    )(x2, u, u, attn, gin, bin_, wpool_bd, pscale, wout, g1, b1, wr_hi, wr_lo, br)


ROUTE_TB = 512
ROW_ALIGN = 16
RUN_BITS = 6
STAGE_ROWS = 2560
STAGE_CHUNK = 512
COMBINE_CHUNK = 1280
FFN_TM = 512
TAIL_BITS = 5


def _route_kernel(ids_ref, slot_ref, slot_t_ref, meta_ref, carry_ref):
    c = pl.program_id(0)

    @pl.when(c == 0)
    def _():
        carry_ref[...] = jnp.zeros(carry_ref.shape, F32)

    tb = ids_ref.shape[0]
    ids = ids_ref[...]
    lane = lax.broadcasted_iota(jnp.int32, (tb, LANES), 1)
    onehot = [jnp.where(lane == ids[:, k:k + 1], 1.0, 0.0) for k in range(TOP_K)]
    member = onehot[0] + onehot[1] + onehot[2] + onehot[3]
    r = lax.broadcasted_iota(jnp.int32, (tb, tb), 0)
    cc = lax.broadcasted_iota(jnp.int32, (tb, tb), 1)
    before = jnp.where(cc < r, 1.0, 0.0).astype(BF16)
    lrank = jnp.dot(before, member.astype(BF16), preferred_element_type=F32)
    n = jnp.sum(member, axis=0, keepdims=True)
    units = jnp.ceil(n * (1.0 / ROW_ALIGN))
    er = lax.broadcasted_iota(jnp.int32, (LANES, LANES), 0)
    ec = lax.broadcasted_iota(jnp.int32, (LANES, LANES), 1)
    lower = jnp.where(er < ec, 1.0, 0.0).astype(BF16)
    off_units = jnp.dot(jnp.broadcast_to(units, (8, LANES)).astype(BF16), lower,
                        preferred_element_type=F32)[0:1, :]
    base = off_units * ROW_ALIGN + lrank
    slot = jnp.zeros((tb, LANES), jnp.int32)
    for k in range(TOP_K):
        sk = jnp.sum(onehot[k] * base, axis=-1, keepdims=True).astype(jnp.int32)
        slot = jnp.where(lane == k, sk, slot)
    slot_ref[...] = slot
    slot_t_ref[...] = slot.T[0:8, :]
    row = lax.broadcasted_iota(jnp.int32, (8, LANES), 0)
    meta = jnp.where(row == 0, units, jnp.where(row == 1, off_units, carry_ref[...]))
    meta_ref[...] = meta.astype(jnp.int32)
    carry_ref[...] = carry_ref[...] + units


def _route(ids):
    T = ids.shape[0]
    nblk = T // ROUTE_TB
    return pl.pallas_call(
        _route_kernel,
        grid=(nblk,),
        in_specs=[pl.BlockSpec((ROUTE_TB, LANES), lambda c: (c, 0))],
        out_specs=(pl.BlockSpec((ROUTE_TB, LANES), lambda c: (c, 0)),
                   pl.BlockSpec((8, ROUTE_TB), lambda c: (0, c)),
                   pl.BlockSpec((8, LANES), lambda c: (c, 0))),
        out_shape=(jax.ShapeDtypeStruct((T, LANES), jnp.int32),
                   jax.ShapeDtypeStruct((8, T), jnp.int32),
                   jax.ShapeDtypeStruct((nblk * 8, LANES), jnp.int32)),
        scratch_shapes=[pltpu.VMEM((1, LANES), F32)],
        compiler_params=pltpu.CompilerParams(
            dimension_semantics=("arbitrary",), vmem_limit_bytes=VMEM_LIMIT),
    )(ids)


def _pieces_of(m, so, do, bits, visit):
    for b in bits:
        done = m & ((1 << b) - 1)

        @pl.when(((m >> b) & 1) == 1)
        def _(b=b, done=done):
            visit(b, pl.multiple_of((so + done) * ROW_ALIGN, ROW_ALIGN),
                  pl.multiple_of((do + done) * ROW_ALIGN, ROW_ALIGN), ROW_ALIGN << b)


def _piece_lists(units, off_units, dst_units):
    bits = jnp.arange(RUN_BITS, dtype=jnp.int32)[None, :, None]
    m = units[:, None, :]
    has = ((m >> bits) & 1) == 1
    done = m & ((1 << bits) - 1)
    pos = jnp.cumsum(has, axis=-1) - 1
    at = has[..., None] & (pos[..., None] == jnp.arange(N_EXPERTS, dtype=jnp.int32))
    pick = lambda v: jnp.sum(jnp.where(at, v[..., None], 0), axis=2).reshape(-1).astype(jnp.int32)
    cnt = jnp.sum(has, axis=-1).reshape(-1).astype(jnp.int32)
    return cnt, pick(off_units[:, None, :] + done), pick(dst_units[:, None, :] + done)


def _run_pieces(cnt_ref, srow_ref, grow_ref, c, visit):
    for b in range(RUN_BITS):
        base = (c * RUN_BITS + b) * N_EXPERTS

        def body(j, carry, b=b, base=base):
            visit(b, j, pl.multiple_of(srow_ref[base + j] * ROW_ALIGN, ROW_ALIGN),
                  pl.multiple_of(grow_ref[base + j] * ROW_ALIGN, ROW_ALIGN), ROW_ALIGN << b)
            return carry

        lax.fori_loop(0, cnt_ref[c * RUN_BITS + b], body, 0)


def _dispatch_kernel(cnt_ref, srow_ref, grow_ref, tail_ref, tail_dst_ref, slot_t_ref, h_ref,
                     xs_ref, stage_ref, zero_ref, sems, *, nblk):
    c = pl.program_id(0)
    par = c % 2
    tb = h_ref.shape[0]
    for r0 in range(0, STAGE_ROWS, STAGE_CHUNK):
        srow = r0 + lax.broadcasted_iota(jnp.int32, (STAGE_CHUNK, tb), 0)
        hit = srow == slot_t_ref[0:1, :]
        for k in range(1, TOP_K):
            hit = hit | (srow == slot_t_ref[k:k + 1, :])
        perm = jnp.where(hit, 1.0, 0.0).astype(BF16)
        stage_ref[par, r0:r0 + STAGE_CHUNK, :] = jnp.dot(
            perm, h_ref[...], preferred_element_type=F32).astype(BF16)

    def piece(p):
        return lambda b, j, s_row, g_row, rows: pltpu.make_async_copy(
            stage_ref.at[p, pl.ds(s_row, rows), :], xs_ref.at[pl.ds(g_row, rows), :],
            sems.at[p, j, b])

    _run_pieces(cnt_ref, srow_ref, grow_ref, c, lambda *a: piece(par)(*a).start())

    @pl.when(c > 0)
    def _():
        _run_pieces(cnt_ref, srow_ref, grow_ref, c - 1, lambda *a: piece(1 - par)(*a).wait())

    @pl.when(c == nblk - 1)
    def _():
        _run_pieces(cnt_ref, srow_ref, grow_ref, c, lambda *a: piece(par)(*a).wait())
        zero_ref[...] = jnp.zeros(zero_ref.shape, BF16)

        def tail(e, b, z_row, g_row, rows):
            del z_row
            return pltpu.make_async_copy(zero_ref.at[pl.ds(0, rows), :],
                                         xs_ref.at[pl.ds(g_row, rows), :], sems.at[0, e, b])

        def tails(act):
            for e in range(N_EXPERTS):
                _pieces_of(tail_ref[e], 0, tail_dst_ref[e], range(TAIL_BITS),
                           lambda b, z, g, rows, e=e: act(tail(e, b, z, g, rows)))

        tails(lambda cp: cp.start())
        tails(lambda cp: cp.wait())

        zrows = zero_ref.shape[0]
        used = (tail_dst_ref[N_EXPERTS - 1] + tail_ref[N_EXPERTS - 1]) * ROW_ALIGN
        n_rest = (xs_ref.shape[0] - used) // zrows

        def rest(i):
            row0 = pl.multiple_of(used + i * zrows, ROW_ALIGN)
            return pltpu.make_async_copy(
                zero_ref, xs_ref.at[pl.ds(row0, zrows), :],
                sems.at[i // (N_EXPERTS * RUN_BITS), (i // RUN_BITS) % N_EXPERTS, i % RUN_BITS])

        lax.fori_loop(0, n_rest, lambda i, _: (rest(i).start(), 0)[1], 0)
        lax.fori_loop(0, n_rest, lambda i, _: (rest(i).wait(), 0)[1], 0)


def _dispatch(units, off, dst, tail, tail_dst, slot_t, h1b, n_rows):
    T = h1b.shape[0]
    nblk = T // ROUTE_TB
    grid_spec = pltpu.PrefetchScalarGridSpec(
        num_scalar_prefetch=5,
        grid=(nblk,),
        in_specs=[pl.BlockSpec((8, ROUTE_TB), lambda c, *_: (0, c)),
                  pl.BlockSpec((ROUTE_TB, D_MODEL), lambda c, *_: (c, 0))],
        out_specs=pl.BlockSpec(memory_space=pl.ANY),
        scratch_shapes=[pltpu.VMEM((2, STAGE_ROWS, D_MODEL), BF16),
                        pltpu.VMEM((ROW_ALIGN << (TAIL_BITS - 1), D_MODEL), BF16),
                        pltpu.SemaphoreType.DMA((2, N_EXPERTS, RUN_BITS))],
    )
    return pl.pallas_call(
        functools.partial(_dispatch_kernel, nblk=nblk),
        grid_spec=grid_spec,
        out_shape=jax.ShapeDtypeStruct((n_rows, D_MODEL), BF16),
        compiler_params=pltpu.CompilerParams(
            dimension_semantics=("arbitrary",), vmem_limit_bytes=VMEM_LIMIT),
    )(units, off, dst, tail, tail_dst, slot_t, h1b)


def _ffn_kernel(tile_e_ref, n_tiles_ref, xs_ref, wgu_ref, bgu_ref, wd_ref, bd_ref, y_ref,
                wgu_b, wd_b, *, n_chunk):
    j = pl.program_id(0)
    active = j < n_tiles_ref[0]
    new_expert = (j == 0) | (tile_e_ref[j] != tile_e_ref[jnp.maximum(j - 1, 0)])

    def tile(refresh):
        def weights(dst, src, rows, cols):
            if refresh:
                dst[rows, cols] = src[0, rows, cols].astype(BF16)
            return dst[rows, cols]

        xs = xs_ref[...]
        cw = D_FF // n_chunk
        full = slice(None)
        acc = jnp.zeros(y_ref.shape, F32)
        for c in range(n_chunk):
            cols = slice(c * cw, (c + 1) * cw)
            ucols = slice(D_FF + c * cw, D_FF + (c + 1) * cw)
            gate = jnp.dot(xs, weights(wgu_b, wgu_ref, full, cols), preferred_element_type=F32)
            gate = gate + bgu_ref[0, :, cols]
            up = jnp.dot(xs, weights(wgu_b, wgu_ref, full, ucols), preferred_element_type=F32)
            up = up + bgu_ref[0, :, ucols]
            gate = jnp.minimum(gate, SWIGLU_LIMIT)
            up = jnp.clip(up, -SWIGLU_LIMIT, SWIGLU_LIMIT)
            act = (up + 1.0) * gate * jax.nn.sigmoid(SWIGLU_ALPHA * gate)
            acc = acc + jnp.dot(act.astype(BF16), weights(wd_b, wd_ref, cols, full),
                                preferred_element_type=F32)
        y_ref[...] = (acc + bd_ref[0]).astype(BF16)

    pl.when(active & new_expert)(lambda: tile(True))
    pl.when(active & jnp.logical_not(new_expert))(lambda: tile(False))

    @pl.when(jnp.logical_not(active))
    def _():
        y_ref[...] = jnp.zeros(y_ref.shape, BF16)


def _ffn(tile_e, n_tiles, xs, wgu, bgu, wd, bd):
    n_rows = xs.shape[0]
    last = lambda j, nt: jnp.minimum(j, nt[0] - 1)
    row = lambda j, te, nt: (last(j, nt), 0)
    exp3 = lambda j, te, nt: (te[last(j, nt)], 0, 0)
    grid_spec = pltpu.PrefetchScalarGridSpec(
        num_scalar_prefetch=2,
        grid=(n_rows // FFN_TM,),
        in_specs=[
            pl.BlockSpec((FFN_TM, D_MODEL), row),
            pl.BlockSpec((1, D_MODEL, 2 * D_FF), exp3),
            pl.BlockSpec((1, 1, 2 * D_FF), exp3),
            pl.BlockSpec((1, D_FF, D_MODEL), exp3),
            pl.BlockSpec((1, 1, D_MODEL), exp3),
        ],
        out_specs=pl.BlockSpec((FFN_TM, D_MODEL), lambda j, te, nt: (j, 0)),
        scratch_shapes=[pltpu.VMEM((D_MODEL, 2 * D_FF), BF16), pltpu.VMEM((D_FF, D_MODEL), BF16)],
    )
    return pl.pallas_call(
        functools.partial(_ffn_kernel, n_chunk=2),
        grid_spec=grid_spec,
        out_shape=jax.ShapeDtypeStruct((n_rows, D_MODEL), BF16),
        compiler_params=pltpu.CompilerParams(
            dimension_semantics=("arbitrary",), vmem_limit_bytes=FFN_VMEM_LIMIT),
    )(tile_e, n_tiles, xs, wgu, bgu, wd, bd)


def _combine_kernel(cnt_ref, srow_ref, grow_ref, slot_ref, gates_ref, h1_ref, ys_ref,
                    g_ref, b_ref, o_ref, stage_ref, sems, *, nblk):
    c = pl.program_id(0)
    par = c % 2
    tb = h1_ref.shape[0]

    def piece(p):
        return lambda b, j, s_row, g_row, rows: pltpu.make_async_copy(
            ys_ref.at[pl.ds(g_row, rows), :], stage_ref.at[p, pl.ds(s_row, rows), :],
            sems.at[p, j, b])

    @pl.when(c == 0)
    def _():
        stage_ref[...] = jnp.zeros(stage_ref.shape, BF16)
        _run_pieces(cnt_ref, srow_ref, grow_ref, c, lambda *a: piece(0)(*a).start())

    _run_pieces(cnt_ref, srow_ref, grow_ref, c, lambda *a: piece(par)(*a).wait())

    @pl.when(c + 1 < nblk)
    def _():
        _run_pieces(cnt_ref, srow_ref, grow_ref, c + 1, lambda *a: piece(1 - par)(*a).start())

    slot = slot_ref[...]
    gates = gates_ref[...]
    ffn = jnp.zeros((tb, D_MODEL), F32)
    for r0 in range(0, STAGE_ROWS, COMBINE_CHUNK):
        scol = r0 + lax.broadcasted_iota(jnp.int32, (tb, COMBINE_CHUNK), 1)
        w = jnp.zeros((tb, COMBINE_CHUNK), F32)
        for k in range(TOP_K):
            w = w + jnp.where(scol == slot[:, k:k + 1], gates[:, k:k + 1], 0.0)
        ffn = ffn + jnp.dot(w.astype(BF16), stage_ref[par, r0:r0 + COMBINE_CHUNK, :],
                            preferred_element_type=F32)
    o_ref[...] = _layer_norm(ALPHA_RES * h1_ref[...] + ffn, g_ref[...], b_ref[...])


def _combine(units, off, dst, slot, gates, h1, ys, g, b):
    T = h1.shape[0]
    nblk = T // ROUTE_TB
    blk = lambda c, *_: (c, 0)
    fixed = lambda c, *_: (0, 0)
    grid_spec = pltpu.PrefetchScalarGridSpec(
        num_scalar_prefetch=3,
        grid=(nblk,),
        in_specs=[pl.BlockSpec((ROUTE_TB, LANES), blk),
                  pl.BlockSpec((ROUTE_TB, LANES), blk),
                  pl.BlockSpec((ROUTE_TB, D_MODEL), blk),
                  pl.BlockSpec(memory_space=pl.ANY),
                  pl.BlockSpec((1, D_MODEL), fixed),
                  pl.BlockSpec((1, D_MODEL), fixed)],
        out_specs=pl.BlockSpec((ROUTE_TB, D_MODEL), blk),
        scratch_shapes=[pltpu.VMEM((2, STAGE_ROWS, D_MODEL), BF16),
                        pltpu.SemaphoreType.DMA((2, N_EXPERTS, RUN_BITS))],
    )
    return pl.pallas_call(
        functools.partial(_combine_kernel, nblk=nblk),
        grid_spec=grid_spec,
        out_shape=jax.ShapeDtypeStruct((T, D_MODEL), F32),
        compiler_params=pltpu.CompilerParams(
            dimension_semantics=("arbitrary",), vmem_limit_bytes=VMEM_LIMIT),
    )(units, off, dst, slot, gates, h1, ys, g, b)


def kernel(x, positions, ln_in_g, ln_in_b, w_in, w_pool, pool_scale, w_out, ln1_g, ln1_b,
           w_router, b_router, w_gate_up, b_gate_up, w_down, b_down, ln2_g, ln2_b):
    B, L, D = x.shape
    T = B * L
    assert T % ROUTE_TB == 0 and D == D_MODEL
    tq = min(256, L)
    x2 = x.reshape(T, D)
    pos2 = positions.reshape(1, T)
    gin = ln_in_g.reshape(1, D)
    bin_ = ln_in_b.reshape(1, D)

    w_pad = jnp.pad(w_in[0], ((0, 0), (0, IN_WIDTH_PAD - IN_WIDTH))).astype(BF16)
    u, q, k, v, iq, ik, iw = _inproj(x2, pos2, gin, bin_, w_pad, tq)

    attn = _attn(q, k, v, iq, ik, iw, B, L, min(256, L // CAUSAL_BANDS))

    wpool_bd = jnp.zeros((POOL_WIDTH, POOL_WIDTH), F32)
    for gi in range(POOL_GROUPS):
        sl = slice(gi * POOL_GDIM, (gi + 1) * POOL_GDIM)
        wpool_bd = wpool_bd.at[sl, sl].set(w_pool[0, gi])
    wr = jnp.pad(w_router[0], ((0, 0), (0, LANES - N_EXPERTS)))
    wr_hi = wr.astype(BF16)
    wr_lo = (wr - wr_hi.astype(F32)).astype(BF16)
    br = jnp.pad(b_router[0], (0, LANES - N_EXPERTS)).reshape(1, LANES)
    h1, h1b, ids, gates = _outproj(
        x2, u, attn, gin, bin_, wpool_bd.astype(BF16), pool_scale[0].reshape(1, POOL_WIDTH),
        w_out[0].astype(BF16), ln1_g[0].reshape(1, D), ln1_b[0].reshape(1, D),
        wr_hi, wr_lo, br, L, tq)

    nblk = T // ROUTE_TB
    slot, slot_t, meta = _route(ids)
    meta = meta.reshape(nblk, 8, LANES)[:, :, :N_EXPERTS]
    units, off_units, base_units = meta[:, 0], meta[:, 1], meta[:, 2]
    tile_units = FFN_TM // ROW_ALIGN
    total_units = base_units[-1] + units[-1]
    region_units = ((total_units + tile_units - 1) // tile_units) * tile_units
    region_end = jnp.cumsum(region_units)
    dst_units = (region_end - region_units)[None, :] + base_units
    max_rows = T * TOP_K + nblk * N_EXPERTS * (ROW_ALIGN - 1) + N_EXPERTS * (FFN_TM - 1)
    n_rows = -(-max_rows // FFN_TM) * FFN_TM
    tile_start = jnp.arange(n_rows // FFN_TM, dtype=jnp.int32) * tile_units
    tile_e = jnp.minimum(jnp.sum(tile_start[:, None] >= region_end[None, :], axis=1),
                         N_EXPERTS - 1).astype(jnp.int32)
    n_tiles = (region_end[-1:] // tile_units).astype(jnp.int32)
    cnt, srow, grow = _piece_lists(units, off_units, dst_units)
    tail_units = (region_units - total_units).astype(jnp.int32)
    tail_dst = (region_end - tail_units).astype(jnp.int32)

    xs = _dispatch(cnt, srow, grow, tail_units, tail_dst, slot_t, h1b, n_rows)
    ys = _ffn(tile_e, n_tiles, xs, w_gate_up[0], b_gate_up[0].reshape(N_EXPERTS, 1, 2 * D_FF),
              w_down[0], b_down[0].reshape(N_EXPERTS, 1, D))
    out = _combine(cnt, srow, grow, slot, gates, h1, ys,
                   ln2_g[0].reshape(1, D), ln2_b[0].reshape(1, D))
    return out.reshape(B, L, D)
```

```python
import functools

import jax
import jax.numpy as jnp
from jax import lax
from jax.experimental import pallas as pl
from jax.experimental.pallas import tpu as pltpu

F32 = jnp.float32
BF16 = jnp.bfloat16

D_MODEL = 1024
POOL_WIDTH = 256
POOL_GROUPS = 4
POOL_GDIM = 64
POOL_WINDOWS = (2, 4, 8, 16)
POOL_HALO = 16
HEAD_DIM = 128
ATTN_WIDTH = 768
N_HEADS = 6
N_KV_HEADS = 2
KV_WIDTH = 256
IDX_HEADS = 8
IDX_DIM = 64
TOPK_MAX = 256
ROPE_THETA = 500000.0
ROPE_DIM = 32
IDX_ROPE_DIM = 16
N_EXPERTS = 32
TOP_K = 4
D_FF = 1024
SWIGLU_LIMIT = 7.0
SWIGLU_ALPHA = 1.702
DEPTH = 1
ALPHA_RES = (2.0 * DEPTH) ** 0.25
LN_EPS = 1e-5
NEG_INF = -1e30
OFF_Q = 256
OFF_K = 1024
OFF_V = 1280
OFF_IQ = 1536
OFF_IK = 2048
IN_WIDTH = 2120
IN_WIDTH_PAD = 2176

LANES = 128
VMEM_LIMIT = 48 * 1024 * 1024
FFN_VMEM_LIMIT = 56 * 1024 * 1024
INT_MIN = -2 ** 31
F32_TINY = 2.0 ** -126
TIE_BLOCK = 256
CAUSAL_BANDS = 4
SEARCH_UNROLL = 8


def _layer_norm(x, g, b):
    mu = jnp.mean(x, axis=-1, keepdims=True)
    xc = x - mu
    var = jnp.mean(xc * xc, axis=-1, keepdims=True)
    return xc * lax.rsqrt(var + LN_EPS) * g + b


def _rope(xh, cos, sin, first_half, half):
    partner = jnp.where(first_half, pltpu.roll(xh, LANES - half, 1), pltpu.roll(xh, half, 1))
    return xh * cos + partner * sin


def _inproj_kernel(x_ref, pos_ref, g_ref, b_ref, w_ref, freq_ref,
                   u_ref, q_ref, k_ref, v_ref, iq_ref, ik_ref, iw_ref):
    tq = x_ref.shape[0]
    h = _layer_norm(x_ref[...], g_ref[...], b_ref[...])
    proj = jnp.dot(h.astype(BF16), w_ref[...], preferred_element_type=F32)

    pos = pos_ref[...].astype(F32)
    ang_q = freq_ref[0:ROPE_DIM // 2, :] * pos
    ang_i = freq_ref[ROPE_DIM // 2:ROPE_DIM // 2 + IDX_ROPE_DIM // 2, :] * pos
    cq, sq = jnp.cos(ang_q), jnp.sin(ang_q)
    ci, si = jnp.cos(ang_i), jnp.sin(ang_i)
    rot = jnp.concatenate(
        [cq, cq, -sq, sq, ci, ci, -si, si, jnp.zeros((LANES - 96, tq), F32)], axis=0)
    c = rot.T
    lane = lax.broadcasted_iota(jnp.int32, (tq, LANES), 1)
    in_q = lane < ROPE_DIM
    cos_q = jnp.where(in_q, c, 1.0)
    sin_q = jnp.where(in_q, pltpu.roll(c, LANES - 32, 1), 0.0)
    first_i = lane < IDX_ROPE_DIM
    second_i = (lane >= IDX_DIM) & (lane < IDX_DIM + IDX_ROPE_DIM)
    cos_i = jnp.where(first_i, pltpu.roll(c, LANES - 64, 1),
                      jnp.where(second_i, c, 1.0))
    sin_i = jnp.where(first_i, pltpu.roll(c, LANES - 80, 1),
                      jnp.where(second_i, pltpu.roll(c, LANES - 16, 1), 0.0))
    half_q = lane < ROPE_DIM // 2
    half_i = (lane % IDX_DIM) < IDX_ROPE_DIM // 2

    u_ref[...] = proj[:, 0:OFF_Q]
    scale = HEAD_DIM ** -0.5
    for hh in range(N_HEADS):
        c0 = OFF_Q + hh * HEAD_DIM
        r = _rope(proj[:, c0:c0 + HEAD_DIM], cos_q, sin_q, half_q, ROPE_DIM // 2)
        q_ref[:, hh * HEAD_DIM:(hh + 1) * HEAD_DIM] = (r * scale).astype(BF16)
    for hh in range(N_KV_HEADS):
        c0 = OFF_K + hh * HEAD_DIM
        r = _rope(proj[:, c0:c0 + HEAD_DIM], cos_q, sin_q, half_q, ROPE_DIM // 2)
        k_ref[:, hh * HEAD_DIM:(hh + 1) * HEAD_DIM] = r.astype(BF16)
    ones = jnp.ones((tq, HEAD_DIM), BF16)
    for hh in range(N_KV_HEADS):
        c0 = OFF_V + hh * HEAD_DIM
        v_ref[:, 2 * hh * HEAD_DIM:(2 * hh + 1) * HEAD_DIM] = proj[:, c0:c0 + HEAD_DIM].astype(BF16)
        v_ref[:, (2 * hh + 1) * HEAD_DIM:(2 * hh + 2) * HEAD_DIM] = ones
    for t in range(IDX_HEADS * IDX_DIM // LANES):
        c0 = OFF_IQ + t * LANES
        r = _rope(proj[:, c0:c0 + LANES], cos_i, sin_i, half_i, IDX_ROPE_DIM // 2)
        iq_ref[:, t * LANES:(t + 1) * LANES] = r.astype(BF16)
    tail = proj[:, OFF_IK:OFF_IK + LANES]
    r = _rope(tail, cos_i, sin_i, half_i, IDX_ROPE_DIM // 2)
    ik_ref[...] = r[:, 0:IDX_DIM].astype(BF16)
    iw_ref[...] = tail[:, IDX_DIM:IDX_DIM + IDX_HEADS]


def _rope_freqs():
    f_q = ROPE_THETA ** (-jnp.arange(0, ROPE_DIM, 2, dtype=F32) / ROPE_DIM)
    f_i = ROPE_THETA ** (-jnp.arange(0, IDX_ROPE_DIM, 2, dtype=F32) / IDX_ROPE_DIM)
    return jnp.concatenate([f_q, f_i]).reshape(-1, 1)


def _inproj(x2, pos2, g, b, w_pad, tq):
    T = x2.shape[0]
    row = lambda i: (i, 0)
    fixed = lambda i: (0, 0)
    out_shapes = (
        jax.ShapeDtypeStruct((T, POOL_WIDTH), F32),
        jax.ShapeDtypeStruct((T, ATTN_WIDTH), BF16),
        jax.ShapeDtypeStruct((T, KV_WIDTH), BF16),
        jax.ShapeDtypeStruct((T, 2 * KV_WIDTH), BF16),
        jax.ShapeDtypeStruct((T, IDX_HEADS * IDX_DIM), BF16),
        jax.ShapeDtypeStruct((T, IDX_DIM), BF16),
        jax.ShapeDtypeStruct((T, IDX_HEADS), F32),
    )
    return pl.pallas_call(
        _inproj_kernel,
        grid=(T // tq,),
        in_specs=[
            pl.BlockSpec((tq, D_MODEL), row),
            pl.BlockSpec((1, tq), lambda i: (0, i)),
            pl.BlockSpec((1, D_MODEL), fixed),
            pl.BlockSpec((1, D_MODEL), fixed),
            pl.BlockSpec((D_MODEL, IN_WIDTH_PAD), fixed),
            pl.BlockSpec((ROPE_DIM // 2 + IDX_ROPE_DIM // 2, 1), fixed),
        ],
        out_specs=tuple(pl.BlockSpec((tq, s.shape[1]), row) for s in out_shapes),
        out_shape=out_shapes,
        compiler_params=pltpu.CompilerParams(
            dimension_semantics=("parallel",), vmem_limit_bytes=VMEM_LIMIT),
    )(x2, pos2, g, b, w_pad, _rope_freqs())


def _row_count(mask):
    return jnp.sum(jnp.where(mask, 1.0, 0.0), axis=-1, keepdims=True)


def _select_bias(iq_ref, ik_ref, iw_ref, key_ref, bias_ref, hk_ref, *, i, tq, S, n_sel):
    q_pos = i * tq + lax.broadcasted_iota(jnp.int32, (tq, S), 0)
    k_pos = lax.broadcasted_iota(jnp.int32, (tq, S), 1)
    causal = k_pos <= q_pos
    if S <= n_sel:
        bias_ref[:, 0:S] = jnp.where(causal, 0.0, NEG_INF)
        return

    w_scale = (IDX_HEADS ** -0.5) * (IDX_DIM ** -0.5)
    iw = iw_ref[...] * w_scale
    ik = ik_ref[0:S, :]
    nt = (((1,), (1,)), ((), ()))
    sc = jnp.zeros((tq, S), F32)
    for hh in range(IDX_HEADS):
        s = lax.dot_general(iq_ref[:, hh * IDX_DIM:(hh + 1) * IDX_DIM], ik, nt,
                            preferred_element_type=F32)
        sc = sc + jnp.maximum(s, 0.0) * iw[:, hh:hh + 1]
    sc = jnp.where(jnp.abs(sc) < F32_TINY, 0.0, sc)
    sc = jnp.where(causal, sc, NEG_INF)
    bits = pltpu.bitcast(sc, jnp.int32)
    key_ref[:, 0:S] = bits ^ ((bits >> 31) & jnp.int32(0x7FFFFFFF))
    hi_mask = jnp.int32(-65536)
    hk_ref[:, 0:S] = pltpu.bitcast(bits & hi_mask, F32).astype(BF16)

    one = jnp.ones((tq, LANES), BF16)
    zero = jnp.zeros((tq, LANES), BF16)

    def coarse_step(b, t_u):
        cand = t_u | lax.shift_left(jnp.int32(1), 31 - b)
        fbits = jnp.where(cand < 0, cand ^ jnp.int32(INT_MIN), ~cand) & hi_mask
        cand_f = jnp.where((fbits & jnp.int32(0x7F800000)) == 0,
                           jnp.where(fbits > 0, F32_TINY, 0.0), pltpu.bitcast(fbits, F32))
        cand_f = jnp.broadcast_to(cand_f, (tq, LANES)).astype(BF16)
        acc = zero
        for c0 in range(0, S, LANES):
            acc = acc + jnp.where(hk_ref[:, c0:c0 + LANES] >= cand_f, one, zero)
        cnt = jnp.sum(acc.astype(F32), axis=-1, keepdims=True)
        return jnp.where(cnt >= n_sel, cand, t_u)

    def value_step(b, t_u):
        cand = t_u | lax.shift_left(jnp.int32(1), 31 - b)
        cnt = _row_count(key_ref[:, 0:S] >= (cand ^ jnp.int32(INT_MIN)))
        return jnp.where(cnt >= n_sel, cand, t_u)

    def unrolled(step):
        def steps(bb, t_u):
            for r in range(SEARCH_UNROLL):
                t_u = step(bb * SEARCH_UNROLL + r, t_u)
            return t_u
        return steps

    t_u = lax.fori_loop(0, 16 // SEARCH_UNROLL, unrolled(coarse_step),
                        jnp.zeros((tq, 1), jnp.int32))
    t_u = lax.fori_loop(16 // SEARCH_UNROLL, 32 // SEARCH_UNROLL, unrolled(value_step), t_u)
    thr = t_u ^ jnp.int32(INT_MIN)
    key = key_ref[:, 0:S]
    need = n_sel - _row_count(key > thr)
    tied = key == thr
    bias_ref[:, 0:S] = jnp.where((key >= thr) & causal, 0.0, NEG_INF)
    excess = _row_count(tied & causal) > need
    any_excess = jnp.max(jnp.where(excess, 1.0, 0.0)) > 0.0

    @pl.when(any_excess)
    def _():
        r = lax.broadcasted_iota(jnp.int32, (TIE_BLOCK, TIE_BLOCK), 0)
        cc = lax.broadcasted_iota(jnp.int32, (TIE_BLOCK, TIE_BLOCK), 1)
        upto = jnp.where(r <= cc, 1.0, 0.0).astype(BF16)
        qp = i * tq + lax.broadcasted_iota(jnp.int32, (tq, TIE_BLOCK), 0)
        carry = jnp.zeros((tq, 1), F32)
        for c0 in range(0, S, TIE_BLOCK):
            kb = key_ref[:, c0:c0 + TIE_BLOCK]
            cz = (c0 + lax.broadcasted_iota(jnp.int32, (tq, TIE_BLOCK), 1)) <= qp
            tb = (kb == thr) & cz
            seen = jnp.dot(jnp.where(tb, 1.0, 0.0).astype(BF16), upto,
                           preferred_element_type=F32) + carry
            admit = ((kb > thr) & cz) | (tb & (seen <= need))
            bias_ref[:, c0:c0 + TIE_BLOCK] = jnp.where(admit, 0.0, NEG_INF)
            carry = seen[:, TIE_BLOCK - 1:TIE_BLOCK]


def _attn_tile(q_ref, k_ref, v_ref, iq_ref, ik_ref, iw_ref, o_ref, key_ref, bias_ref, hk_ref,
               *, i, tq, S, n_sel):
    _select_bias(iq_ref, ik_ref, iw_ref, key_ref, bias_ref, hk_ref,
                 i=i, tq=tq, S=S, n_sel=n_sel)
    nt = (((1,), (1,)), ((), ()))
    for hh in range(N_HEADS):
        g = hh // (N_HEADS // N_KV_HEADS)
        kg = k_ref[0:S, g * HEAD_DIM:(g + 1) * HEAD_DIM]
        vg = v_ref[0:S, 2 * g * HEAD_DIM:2 * (g + 1) * HEAD_DIM]
        logits = lax.dot_general(q_ref[:, hh * HEAD_DIM:(hh + 1) * HEAD_DIM], kg, nt,
                                 preferred_element_type=F32) + bias_ref[:, 0:S]
        m = jnp.max(logits, axis=-1, keepdims=True)
        p = jnp.exp((logits - m).astype(BF16))
        ol = jnp.dot(p, vg, preferred_element_type=F32)
        o = ol[:, 0:HEAD_DIM] / ol[:, HEAD_DIM:HEAD_DIM + 1]
        o_ref[:, hh * HEAD_DIM:(hh + 1) * HEAD_DIM] = o.astype(BF16)


def _attn_kernel(*refs, tq, L, n_sel):
    i = pl.program_id(1)
    band = L // CAUSAL_BANDS
    for v in range(CAUSAL_BANDS):
        @pl.when((i * tq) // band == v)
        def _(v=v):
            _attn_tile(*refs, i=i, tq=tq, S=(v + 1) * band, n_sel=n_sel)


def _attn(q, k, v, iq, ik, iw, B, L, tq):
    T = B * L
    nq = L // tq
    n_sel = min(TOPK_MAX, L // 4)
    assert L % (CAUSAL_BANDS * tq) == 0
    qrow = lambda b, i: (b * nq + i, 0)
    seq = lambda b, i: (b, 0)
    return pl.pallas_call(
        functools.partial(_attn_kernel, tq=tq, L=L, n_sel=n_sel),
        grid=(B, nq),
        in_specs=[
            pl.BlockSpec((tq, ATTN_WIDTH), qrow),
            pl.BlockSpec((L, KV_WIDTH), seq),
            pl.BlockSpec((L, 2 * KV_WIDTH), seq),
            pl.BlockSpec((tq, IDX_HEADS * IDX_DIM), qrow),
            pl.BlockSpec((L, IDX_DIM), seq),
            pl.BlockSpec((tq, IDX_HEADS), qrow),
        ],
        out_specs=pl.BlockSpec((tq, ATTN_WIDTH), qrow),
        out_shape=jax.ShapeDtypeStruct((T, ATTN_WIDTH), BF16),
        scratch_shapes=[pltpu.VMEM((tq, L), jnp.int32), pltpu.VMEM((tq, L), F32),
                        pltpu.VMEM((tq, L), BF16)],
        compiler_params=pltpu.CompilerParams(
            dimension_semantics=("parallel", "arbitrary"), vmem_limit_bytes=VMEM_LIMIT),
    )(q, k, v, iq, ik, iw)


def _outproj_kernel(x_ref, u_ref, halo_ref, a_ref, gin_ref, bin_ref, wpool_ref, pscale_ref,
                    wout_ref, g1_ref, b1_ref, wr_hi_ref, wr_lo_ref, br_ref,
                    h1_ref, h1b_ref, ids_ref, gates_ref, *, tq, tiles_per_seq):
    i = pl.program_id(0)
    seq_tile = i % tiles_per_seq
    u = u_ref[...]
    halo = jnp.where(seq_tile == 0, 0.0, halo_ref[...])
    ext = jnp.concatenate([halo, u], axis=0)
    lane = lax.broadcasted_iota(jnp.int32, (tq, POOL_WIDTH), 1)
    grp = lane // POOL_GDIM
    win = jnp.zeros((tq, POOL_WIDTH), F32)
    s = ext
    for gi, w in enumerate(POOL_WINDOWS):
        s = s + pltpu.roll(s, w // 2, 0)
        win = jnp.where(grp == gi, s[POOL_HALO:, :], win)
    t_seq = seq_tile * tq + lax.broadcasted_iota(jnp.int32, (tq, POOL_WIDTH), 0)
    width = lax.shift_left(jnp.int32(2), grp)
    cnt = jnp.minimum(t_seq + 1, width).astype(F32)
    d = win / cnt - u
    y_pool = jnp.dot(d.astype(BF16), wpool_ref[...], preferred_element_type=F32) * pscale_ref[...]

    mix = jnp.dot(y_pool.astype(BF16), wout_ref[0:POOL_WIDTH, :], preferred_element_type=F32)
    mix = mix + jnp.dot(a_ref[...], wout_ref[POOL_WIDTH:, :], preferred_element_type=F32)
    h = _layer_norm(x_ref[...], gin_ref[...], bin_ref[...])
    h1 = _layer_norm(ALPHA_RES * h + mix, g1_ref[...], b1_ref[...])
    h1_ref[...] = h1
    h1_hi = h1.astype(BF16)
    h1b_ref[...] = h1_hi
    h1_lo = (h1 - h1_hi.astype(F32)).astype(BF16)
    wr_hi = wr_hi_ref[...]
    logits = (jnp.dot(h1_hi, wr_hi, preferred_element_type=F32)
              + jnp.dot(h1_lo, wr_hi, preferred_element_type=F32)
              + jnp.dot(h1_hi, wr_lo_ref[...], preferred_element_type=F32)) + br_ref[...]
    lane_e = lax.broadcasted_iota(jnp.int32, (tq, LANES), 1).astype(F32)
    lg = jnp.where(lane_e < N_EXPERTS, logits, -jnp.inf)
    ids = jnp.zeros((tq, LANES), F32)
    ex = jnp.zeros((tq, LANES), F32)
    top0 = None
    for kk in range(TOP_K):
        m = jnp.max(lg, axis=-1, keepdims=True)
        idx = jnp.min(jnp.where(lg == m, lane_e, float(LANES)), axis=-1, keepdims=True)
        if top0 is None:
            top0 = m
        ids = jnp.where(lane_e == kk, idx, ids)
        ex = jnp.where(lane_e == kk, jnp.exp(m - top0), ex)
        lg = jnp.where(lane_e == idx, -jnp.inf, lg)
    ids_ref[...] = ids.astype(jnp.int32)
    gates_ref[...] = ex / jnp.sum(ex, axis=-1, keepdims=True)


def _outproj(x2, u, attn, gin, bin_, wpool_bd, pscale, wout, g1, b1, wr_hi, wr_lo, br, L, tq):
    T = x2.shape[0]
    row = lambda i: (i, 0)
    fixed = lambda i: (0, 0)
    halo = lambda i: (jnp.maximum(i * (tq // POOL_HALO) - 1, 0), 0)
    out_shapes = (
        jax.ShapeDtypeStruct((T, D_MODEL), F32),
        jax.ShapeDtypeStruct((T, D_MODEL), BF16),
        jax.ShapeDtypeStruct((T, LANES), jnp.int32),
        jax.ShapeDtypeStruct((T, LANES), F32),
    )
    return pl.pallas_call(
        functools.partial(_outproj_kernel, tq=tq, tiles_per_seq=L // tq),
        grid=(T // tq,),
        in_specs=[
            pl.BlockSpec((tq, D_MODEL), row),
            pl.BlockSpec((tq, POOL_WIDTH), row),
            pl.BlockSpec((POOL_HALO, POOL_WIDTH), halo),
            pl.BlockSpec((tq, ATTN_WIDTH), row),
            pl.BlockSpec((1, D_MODEL), fixed),
            pl.BlockSpec((1, D_MODEL), fixed),
            pl.BlockSpec((POOL_WIDTH, POOL_WIDTH), fixed),
            pl.BlockSpec((1, POOL_WIDTH), fixed),
            pl.BlockSpec((D_MODEL, D_MODEL), fixed),
            pl.BlockSpec((1, D_MODEL), fixed),
            pl.BlockSpec((1, D_MODEL), fixed),
            pl.BlockSpec((D_MODEL, LANES), fixed),
            pl.BlockSpec((D_MODEL, LANES), fixed),
            pl.BlockSpec((1, LANES), fixed),
        ],
        out_specs=tuple(pl.BlockSpec((tq, s.shape[1]), row) for s in out_shapes),
        out_shape=out_shapes,
        compiler_params=pltpu.CompilerParams(
            dimension_semantics=("parallel",), vmem_limit_bytes=VMEM_LIMIT),
    )(x2, u, u, attn, gin, bin_, wpool_bd, pscale, wout, g1, b1, wr_hi, wr_lo, br)


ROUTE_TB = 512
ROW_ALIGN = 16
RUN_BITS = 6
STAGE_ROWS = 2560
STAGE_CHUNK = 512
COMBINE_CHUNK = 1280
FFN_TM = 512
TAIL_BITS = 5


def _route_kernel(ids_ref, slot_ref, slot_t_ref, meta_ref, carry_ref):
    c = pl.program_id(0)

    @pl.when(c == 0)
    def _():
        carry_ref[...] = jnp.zeros(carry_ref.shape, F32)

    tb = ids_ref.shape[0]
    ids = ids_ref[...]
    lane = lax.broadcasted_iota(jnp.int32, (tb, LANES), 1)
    onehot = [jnp.where(lane == ids[:, k:k + 1], 1.0, 0.0) for k in range(TOP_K)]
    member = onehot[0] + onehot[1] + onehot[2] + onehot[3]
    r = lax.broadcasted_iota(jnp.int32, (tb, tb), 0)
    cc = lax.broadcasted_iota(jnp.int32, (tb, tb), 1)
    before = jnp.where(cc < r, 1.0, 0.0).astype(BF16)
    lrank = jnp.dot(before, member.astype(BF16), preferred_element_type=F32)
    n = jnp.sum(member, axis=0, keepdims=True)
    units = jnp.ceil(n * (1.0 / ROW_ALIGN))
    er = lax.broadcasted_iota(jnp.int32, (LANES, LANES), 0)
    ec = lax.broadcasted_iota(jnp.int32, (LANES, LANES), 1)
    lower = jnp.where(er < ec, 1.0, 0.0).astype(BF16)
    off_units = jnp.dot(jnp.broadcast_to(units, (8, LANES)).astype(BF16), lower,
                        preferred_element_type=F32)[0:1, :]
    base = off_units * ROW_ALIGN + lrank
    slot = jnp.zeros((tb, LANES), jnp.int32)
    for k in range(TOP_K):
        sk = jnp.sum(onehot[k] * base, axis=-1, keepdims=True).astype(jnp.int32)
        slot = jnp.where(lane == k, sk, slot)
    slot_ref[...] = slot
    slot_t_ref[...] = slot.T[0:8, :]
    row = lax.broadcasted_iota(jnp.int32, (8, LANES), 0)
    meta = jnp.where(row == 0, units, jnp.where(row == 1, off_units, carry_ref[...]))
    meta_ref[...] = meta.astype(jnp.int32)
    carry_ref[...] = carry_ref[...] + units


def _route(ids):
    T = ids.shape[0]
    nblk = T // ROUTE_TB
    return pl.pallas_call(
        _route_kernel,
        grid=(nblk,),
        in_specs=[pl.BlockSpec((ROUTE_TB, LANES), lambda c: (c, 0))],
        out_specs=(pl.BlockSpec((ROUTE_TB, LANES), lambda c: (c, 0)),
                   pl.BlockSpec((8, ROUTE_TB), lambda c: (0, c)),
                   pl.BlockSpec((8, LANES), lambda c: (c, 0))),
        out_shape=(jax.ShapeDtypeStruct((T, LANES), jnp.int32),
                   jax.ShapeDtypeStruct((8, T), jnp.int32),
                   jax.ShapeDtypeStruct((nblk * 8, LANES), jnp.int32)),
        scratch_shapes=[pltpu.VMEM((1, LANES), F32)],
        compiler_params=pltpu.CompilerParams(
            dimension_semantics=("arbitrary",), vmem_limit_bytes=VMEM_LIMIT),
    )(ids)


def _pieces_of(m, so, do, bits, visit):
    for b in bits:
        done = m & ((1 << b) - 1)

        @pl.when(((m >> b) & 1) == 1)
        def _(b=b, done=done):
            visit(b, pl.multiple_of((so + done) * ROW_ALIGN, ROW_ALIGN),
                  pl.multiple_of((do + done) * ROW_ALIGN, ROW_ALIGN), ROW_ALIGN << b)


def _piece_lists(units, off_units, dst_units):
    bits = jnp.arange(RUN_BITS, dtype=jnp.int32)[None, :, None]
    m = units[:, None, :]
    has = ((m >> bits) & 1) == 1
    done = m & ((1 << bits) - 1)
    pos = jnp.cumsum(has, axis=-1) - 1
    at = has[..., None] & (pos[..., None] == jnp.arange(N_EXPERTS, dtype=jnp.int32))
    pick = lambda v: jnp.sum(jnp.where(at, v[..., None], 0), axis=2).reshape(-1).astype(jnp.int32)
    cnt = jnp.sum(has, axis=-1).reshape(-1).astype(jnp.int32)
    return cnt, pick(off_units[:, None, :] + done), pick(dst_units[:, None, :] + done)


def _run_pieces(cnt_ref, srow_ref, grow_ref, c, visit):
    for b in range(RUN_BITS):
        base = (c * RUN_BITS + b) * N_EXPERTS

        def body(j, carry, b=b, base=base):
            visit(b, j, pl.multiple_of(srow_ref[base + j] * ROW_ALIGN, ROW_ALIGN),
                  pl.multiple_of(grow_ref[base + j] * ROW_ALIGN, ROW_ALIGN), ROW_ALIGN << b)
            return carry

        lax.fori_loop(0, cnt_ref[c * RUN_BITS + b], body, 0)


def _dispatch_kernel(cnt_ref, srow_ref, grow_ref, tail_ref, tail_dst_ref, slot_t_ref, h_ref,
                     xs_ref, stage_ref, zero_ref, sems, *, nblk):
    c = pl.program_id(0)
    par = c % 2
    tb = h_ref.shape[0]
    for r0 in range(0, STAGE_ROWS, STAGE_CHUNK):
        srow = r0 + lax.broadcasted_iota(jnp.int32, (STAGE_CHUNK, tb), 0)
        hit = srow == slot_t_ref[0:1, :]
        for k in range(1, TOP_K):
            hit = hit | (srow == slot_t_ref[k:k + 1, :])
        perm = jnp.where(hit, 1.0, 0.0).astype(BF16)
        stage_ref[par, r0:r0 + STAGE_CHUNK, :] = jnp.dot(
            perm, h_ref[...], preferred_element_type=F32).astype(BF16)

    def piece(p):
        return lambda b, j, s_row, g_row, rows: pltpu.make_async_copy(
            stage_ref.at[p, pl.ds(s_row, rows), :], xs_ref.at[pl.ds(g_row, rows), :],
            sems.at[p, j, b])

    _run_pieces(cnt_ref, srow_ref, grow_ref, c, lambda *a: piece(par)(*a).start())

    @pl.when(c > 0)
    def _():
        _run_pieces(cnt_ref, srow_ref, grow_ref, c - 1, lambda *a: piece(1 - par)(*a).wait())

    @pl.when(c == nblk - 1)
    def _():
        _run_pieces(cnt_ref, srow_ref, grow_ref, c, lambda *a: piece(par)(*a).wait())
        zero_ref[...] = jnp.zeros(zero_ref.shape, BF16)

        def tail(e, b, z_row, g_row, rows):
            del z_row
            return pltpu.make_async_copy(zero_ref.at[pl.ds(0, rows), :],
                                         xs_ref.at[pl.ds(g_row, rows), :], sems.at[0, e, b])

        def tails(act):
            for e in range(N_EXPERTS):
                _pieces_of(tail_ref[e], 0, tail_dst_ref[e], range(TAIL_BITS),
                           lambda b, z, g, rows, e=e: act(tail(e, b, z, g, rows)))

        tails(lambda cp: cp.start())
        tails(lambda cp: cp.wait())

        zrows = zero_ref.shape[0]
        used = (tail_dst_ref[N_EXPERTS - 1] + tail_ref[N_EXPERTS - 1]) * ROW_ALIGN
        n_rest = (xs_ref.shape[0] - used) // zrows

        def rest(i):
            row0 = pl.multiple_of(used + i * zrows, ROW_ALIGN)
            return pltpu.make_async_copy(
                zero_ref, xs_ref.at[pl.ds(row0, zrows), :],
                sems.at[i // (N_EXPERTS * RUN_BITS), (i // RUN_BITS) % N_EXPERTS, i % RUN_BITS])

        lax.fori_loop(0, n_rest, lambda i, _: (rest(i).start(), 0)[1], 0)
        lax.fori_loop(0, n_rest, lambda i, _: (rest(i).wait(), 0)[1], 0)


def _dispatch(units, off, dst, tail, tail_dst, slot_t, h1b, n_rows):
    T = h1b.shape[0]
    nblk = T // ROUTE_TB
    grid_spec = pltpu.PrefetchScalarGridSpec(
        num_scalar_prefetch=5,
        grid=(nblk,),
        in_specs=[pl.BlockSpec((8, ROUTE_TB), lambda c, *_: (0, c)),
                  pl.BlockSpec((ROUTE_TB, D_MODEL), lambda c, *_: (c, 0))],
        out_specs=pl.BlockSpec(memory_space=pl.ANY),
        scratch_shapes=[pltpu.VMEM((2, STAGE_ROWS, D_MODEL), BF16),
                        pltpu.VMEM((ROW_ALIGN << (TAIL_BITS - 1), D_MODEL), BF16),
                        pltpu.SemaphoreType.DMA((2, N_EXPERTS, RUN_BITS))],
    )
    return pl.pallas_call(
        functools.partial(_dispatch_kernel, nblk=nblk),
        grid_spec=grid_spec,
        out_shape=jax.ShapeDtypeStruct((n_rows, D_MODEL), BF16),
        compiler_params=pltpu.CompilerParams(
            dimension_semantics=("arbitrary",), vmem_limit_bytes=VMEM_LIMIT),
    )(units, off, dst, tail, tail_dst, slot_t, h1b)


def _ffn_kernel(tile_e_ref, n_tiles_ref, xs_ref, wgu_ref, bgu_ref, wd_ref, bd_ref, y_ref,
                wgu_b, wd_b, *, n_chunk):
    j = pl.program_id(0)
    active = j < n_tiles_ref[0]
    new_expert = (j == 0) | (tile_e_ref[j] != tile_e_ref[jnp.maximum(j - 1, 0)])

    def tile(refresh):
        def weights(dst, src, rows, cols):
            if refresh:
                dst[rows, cols] = src[0, rows, cols].astype(BF16)
            return dst[rows, cols]

        xs = xs_ref[...]
        cw = D_FF // n_chunk
        full = slice(None)
        acc = jnp.zeros(y_ref.shape, F32)
        for c in range(n_chunk):
            cols = slice(c * cw, (c + 1) * cw)
            ucols = slice(D_FF + c * cw, D_FF + (c + 1) * cw)
            gate = jnp.dot(xs, weights(wgu_b, wgu_ref, full, cols), preferred_element_type=F32)
            gate = gate + bgu_ref[0, :, cols]
            up = jnp.dot(xs, weights(wgu_b, wgu_ref, full, ucols), preferred_element_type=F32)
            up = up + bgu_ref[0, :, ucols]
            gate = jnp.minimum(gate, SWIGLU_LIMIT)
            up = jnp.clip(up, -SWIGLU_LIMIT, SWIGLU_LIMIT)
            act = (up + 1.0) * gate * jax.nn.sigmoid(SWIGLU_ALPHA * gate)
            acc = acc + jnp.dot(act.astype(BF16), weights(wd_b, wd_ref, cols, full),
                                preferred_element_type=F32)
        y_ref[...] = (acc + bd_ref[0]).astype(BF16)

    pl.when(active & new_expert)(lambda: tile(True))
    pl.when(active & jnp.logical_not(new_expert))(lambda: tile(False))

    @pl.when(jnp.logical_not(active))
    def _():
        y_ref[...] = jnp.zeros(y_ref.shape, BF16)


def _ffn(tile_e, n_tiles, xs, wgu, bgu, wd, bd):
    n_rows = xs.shape[0]
    last = lambda j, nt: jnp.minimum(j, nt[0] - 1)
    row = lambda j, te, nt: (last(j, nt), 0)
    exp3 = lambda j, te, nt: (te[last(j, nt)], 0, 0)
    grid_spec = pltpu.PrefetchScalarGridSpec(
        num_scalar_prefetch=2,
        grid=(n_rows // FFN_TM,),
        in_specs=[
            pl.BlockSpec((FFN_TM, D_MODEL), row),
            pl.BlockSpec((1, D_MODEL, 2 * D_FF), exp3),
            pl.BlockSpec((1, 1, 2 * D_FF), exp3),
            pl.BlockSpec((1, D_FF, D_MODEL), exp3),
            pl.BlockSpec((1, 1, D_MODEL), exp3),
        ],
        out_specs=pl.BlockSpec((FFN_TM, D_MODEL), lambda j, te, nt: (j, 0)),
        scratch_shapes=[pltpu.VMEM((D_MODEL, 2 * D_FF), BF16), pltpu.VMEM((D_FF, D_MODEL), BF16)],
    )
    return pl.pallas_call(
        functools.partial(_ffn_kernel, n_chunk=2),
        grid_spec=grid_spec,
        out_shape=jax.ShapeDtypeStruct((n_rows, D_MODEL), BF16),
        compiler_params=pltpu.CompilerParams(
            dimension_semantics=("arbitrary",), vmem_limit_bytes=FFN_VMEM_LIMIT),
    )(tile_e, n_tiles, xs, wgu, bgu, wd, bd)


def _combine_kernel(cnt_ref, srow_ref, grow_ref, slot_ref, gates_ref, h1_ref, ys_ref,
                    g_ref, b_ref, o_ref, stage_ref, sems, *, nblk):
    c = pl.program_id(0)
    par = c % 2
    tb = h1_ref.shape[0]

    def piece(p):
        return lambda b, j, s_row, g_row, rows: pltpu.make_async_copy(
            ys_ref.at[pl.ds(g_row, rows), :], stage_ref.at[p, pl.ds(s_row, rows), :],
            sems.at[p, j, b])

    @pl.when(c == 0)
    def _():
        stage_ref[...] = jnp.zeros(stage_ref.shape, BF16)
        _run_pieces(cnt_ref, srow_ref, grow_ref, c, lambda *a: piece(0)(*a).start())

    _run_pieces(cnt_ref, srow_ref, grow_ref, c, lambda *a: piece(par)(*a).wait())

    @pl.when(c + 1 < nblk)
    def _():
        _run_pieces(cnt_ref, srow_ref, grow_ref, c + 1, lambda *a: piece(1 - par)(*a).start())

    slot = slot_ref[...]
    gates = gates_ref[...]
    ffn = jnp.zeros((tb, D_MODEL), F32)
    for r0 in range(0, STAGE_ROWS, COMBINE_CHUNK):
        scol = r0 + lax.broadcasted_iota(jnp.int32, (tb, COMBINE_CHUNK), 1)
        w = jnp.zeros((tb, COMBINE_CHUNK), F32)
        for k in range(TOP_K):
            w = w + jnp.where(scol == slot[:, k:k + 1], gates[:, k:k + 1], 0.0)
        ffn = ffn + jnp.dot(w.astype(BF16), stage_ref[par, r0:r0 + COMBINE_CHUNK, :],
                            preferred_element_type=F32)
    o_ref[...] = _layer_norm(ALPHA_RES * h1_ref[...] + ffn, g_ref[...], b_ref[...])


def _combine(units, off, dst, slot, gates, h1, ys, g, b):
    T = h1.shape[0]
    nblk = T // ROUTE_TB
    blk = lambda c, *_: (c, 0)
    fixed = lambda c, *_: (0, 0)
    grid_spec = pltpu.PrefetchScalarGridSpec(
        num_scalar_prefetch=3,
        grid=(nblk,),
        in_specs=[pl.BlockSpec((ROUTE_TB, LANES), blk),
                  pl.BlockSpec((ROUTE_TB, LANES), blk),
                  pl.BlockSpec((ROUTE_TB, D_MODEL), blk),
                  pl.BlockSpec(memory_space=pl.ANY),
                  pl.BlockSpec((1, D_MODEL), fixed),
                  pl.BlockSpec((1, D_MODEL), fixed)],
        out_specs=pl.BlockSpec((ROUTE_TB, D_MODEL), blk),
        scratch_shapes=[pltpu.VMEM((2, STAGE_ROWS, D_MODEL), BF16),
                        pltpu.SemaphoreType.DMA((2, N_EXPERTS, RUN_BITS))],
    )
    return pl.pallas_call(
        functools.partial(_combine_kernel, nblk=nblk),
        grid_spec=grid_spec,
        out_shape=jax.ShapeDtypeStruct((T, D_MODEL), F32),
        compiler_params=pltpu.CompilerParams(
            dimension_semantics=("arbitrary",), vmem_limit_bytes=VMEM_LIMIT),
    )(units, off, dst, slot, gates, h1, ys, g, b)


def kernel(x, positions, ln_in_g, ln_in_b, w_in, w_pool, pool_scale, w_out, ln1_g, ln1_b,
           w_router, b_router, w_gate_up, b_gate_up, w_down, b_down, ln2_g, ln2_b):
    B, L, D = x.shape
    T = B * L
    assert T % ROUTE_TB == 0 and D == D_MODEL
    tq = min(256, L)
    x2 = x.reshape(T, D)
    pos2 = positions.reshape(1, T)
    gin = ln_in_g.reshape(1, D)
    bin_ = ln_in_b.reshape(1, D)

    w_pad = jnp.pad(w_in[0], ((0, 0), (0, IN_WIDTH_PAD - IN_WIDTH))).astype(BF16)
    u, q, k, v, iq, ik, iw = _inproj(x2, pos2, gin, bin_, w_pad, tq)

    attn = _attn(q, k, v, iq, ik, iw, B, L, min(256, L // CAUSAL_BANDS))

    wpool_bd = jnp.zeros((POOL_WIDTH, POOL_WIDTH), F32)
    for gi in range(POOL_GROUPS):
        sl = slice(gi * POOL_GDIM, (gi + 1) * POOL_GDIM)
        wpool_bd = wpool_bd.at[sl, sl].set(w_pool[0, gi])
    wr = jnp.pad(w_router[0], ((0, 0), (0, LANES - N_EXPERTS)))
    wr_hi = wr.astype(BF16)
    wr_lo = (wr - wr_hi.astype(F32)).astype(BF16)
    br = jnp.pad(b_router[0], (0, LANES - N_EXPERTS)).reshape(1, LANES)
    h1, h1b, ids, gates = _outproj(
        x2, u, attn, gin, bin_, wpool_bd.astype(BF16), pool_scale[0].reshape(1, POOL_WIDTH),
        w_out[0].astype(BF16), ln1_g[0].reshape(1, D), ln1_b[0].reshape(1, D),
        wr_hi, wr_lo, br, L, tq)

    nblk = T // ROUTE_TB
    slot, slot_t, meta = _route(ids)
    meta = meta.reshape(nblk, 8, LANES)[:, :, :N_EXPERTS]
    units, off_units, base_units = meta[:, 0], meta[:, 1], meta[:, 2]
    tile_units = FFN_TM // ROW_ALIGN
    total_units = base_units[-1] + units[-1]
    region_units = ((total_units + tile_units - 1) // tile_units) * tile_units
    region_end = jnp.cumsum(region_units)
    dst_units = (region_end - region_units)[None, :] + base_units
    max_rows = T * TOP_K + nblk * N_EXPERTS * (ROW_ALIGN - 1) + N_EXPERTS * (FFN_TM - 1)
    n_rows = -(-max_rows // FFN_TM) * FFN_TM
    tile_start = jnp.arange(n_rows // FFN_TM, dtype=jnp.int32) * tile_units
    tile_e = jnp.minimum(jnp.sum(tile_start[:, None] >= region_end[None, :], axis=1),
                         N_EXPERTS - 1).astype(jnp.int32)
    n_tiles = (region_end[-1:] // tile_units).astype(jnp.int32)
    cnt, srow, grow = _piece_lists(units, off_units, dst_units)
    tail_units = (region_units - total_units).astype(jnp.int32)
    tail_dst = (region_end - tail_units).astype(jnp.int32)

    xs = _dispatch(cnt, srow, grow, tail_units, tail_dst, slot_t, h1b, n_rows)
    ys = _ffn(tile_e, n_tiles, xs, w_gate_up[0], b_gate_up[0].reshape(N_EXPERTS, 1, 2 * D_FF),
              w_down[0], b_down[0].reshape(N_EXPERTS, 1, D))
    out = _combine(cnt, srow, grow, slot, gates, h1, ys,
                   ln2_g[0].reshape(1, D), ln2_b[0].reshape(1, D))
    return out.reshape(B, L, D)
```

```python
import functools

import jax
import jax.numpy as jnp
from jax import lax
from jax.experimental import pallas as pl
from jax.experimental.pallas import tpu as pltpu

F32 = jnp.float32
BF16 = jnp.bfloat16

D_MODEL = 1024
POOL_WIDTH = 256
POOL_GROUPS = 4
POOL_GDIM = 64
POOL_WINDOWS = (2, 4, 8, 16)
POOL_HALO = 16
HEAD_DIM = 128
ATTN_WIDTH = 768
N_HEADS = 6
N_KV_HEADS = 2
KV_WIDTH = 256
IDX_HEADS = 8
IDX_DIM = 64
TOPK_MAX = 256
ROPE_THETA = 500000.0
ROPE_DIM = 32
IDX_ROPE_DIM = 16
N_EXPERTS = 32
TOP_K = 4
D_FF = 1024
SWIGLU_LIMIT = 7.0
SWIGLU_ALPHA = 1.702
DEPTH = 1
ALPHA_RES = (2.0 * DEPTH) ** 0.25
LN_EPS = 1e-5
NEG_INF = -1e30
OFF_Q = 256
OFF_K = 1024
OFF_V = 1280
OFF_IQ = 1536
OFF_IK = 2048
IN_WIDTH = 2120
IN_WIDTH_PAD = 2176

LANES = 128
VMEM_LIMIT = 48 * 1024 * 1024
FFN_VMEM_LIMIT = 56 * 1024 * 1024
INT_MIN = -2 ** 31
F32_TINY = 2.0 ** -126
TIE_BLOCK = 256
CAUSAL_BANDS = 4
SEARCH_UNROLL = 4


def _layer_norm(x, g, b):
    mu = jnp.mean(x, axis=-1, keepdims=True)
    xc = x - mu
    var = jnp.mean(xc * xc, axis=-1, keepdims=True)
    return xc * lax.rsqrt(var + LN_EPS) * g + b


def _rope(xh, cos, sin, first_half, half):
    partner = jnp.where(first_half, pltpu.roll(xh, LANES - half, 1), pltpu.roll(xh, half, 1))
    return xh * cos + partner * sin


def _inproj_kernel(x_ref, pos_ref, g_ref, b_ref, w_ref, freq_ref,
                   u_ref, q_ref, k_ref, v_ref, iq_ref, ik_ref, iw_ref):
    tq = x_ref.shape[0]
    h = _layer_norm(x_ref[...], g_ref[...], b_ref[...])
    proj = jnp.dot(h.astype(BF16), w_ref[...], preferred_element_type=F32)

    pos = pos_ref[...].astype(F32)
    ang_q = freq_ref[0:ROPE_DIM // 2, :] * pos
    ang_i = freq_ref[ROPE_DIM // 2:ROPE_DIM // 2 + IDX_ROPE_DIM // 2, :] * pos
    cq, sq = jnp.cos(ang_q), jnp.sin(ang_q)
    ci, si = jnp.cos(ang_i), jnp.sin(ang_i)
    rot = jnp.concatenate(
        [cq, cq, -sq, sq, ci, ci, -si, si, jnp.zeros((LANES - 96, tq), F32)], axis=0)
    c = rot.T
    lane = lax.broadcasted_iota(jnp.int32, (tq, LANES), 1)
    in_q = lane < ROPE_DIM
    cos_q = jnp.where(in_q, c, 1.0)
    sin_q = jnp.where(in_q, pltpu.roll(c, LANES - 32, 1), 0.0)
    first_i = lane < IDX_ROPE_DIM
    second_i = (lane >= IDX_DIM) & (lane < IDX_DIM + IDX_ROPE_DIM)
    cos_i = jnp.where(first_i, pltpu.roll(c, LANES - 64, 1),
                      jnp.where(second_i, c, 1.0))
    sin_i = jnp.where(first_i, pltpu.roll(c, LANES - 80, 1),
                      jnp.where(second_i, pltpu.roll(c, LANES - 16, 1), 0.0))
    half_q = lane < ROPE_DIM // 2
    half_i = (lane % IDX_DIM) < IDX_ROPE_DIM // 2

    u_ref[...] = proj[:, 0:OFF_Q]
    scale = HEAD_DIM ** -0.5
    for hh in range(N_HEADS):
        c0 = OFF_Q + hh * HEAD_DIM
        r = _rope(proj[:, c0:c0 + HEAD_DIM], cos_q, sin_q, half_q, ROPE_DIM // 2)
        q_ref[:, hh * HEAD_DIM:(hh + 1) * HEAD_DIM] = (r * scale).astype(BF16)
    for hh in range(N_KV_HEADS):
        c0 = OFF_K + hh * HEAD_DIM
        r = _rope(proj[:, c0:c0 + HEAD_DIM], cos_q, sin_q, half_q, ROPE_DIM // 2)
        k_ref[:, hh * HEAD_DIM:(hh + 1) * HEAD_DIM] = r.astype(BF16)
    ones = jnp.ones((tq, HEAD_DIM), BF16)
    for hh in range(N_KV_HEADS):
        c0 = OFF_V + hh * HEAD_DIM
        v_ref[:, 2 * hh * HEAD_DIM:(2 * hh + 1) * HEAD_DIM] = proj[:, c0:c0 + HEAD_DIM].astype(BF16)
        v_ref[:, (2 * hh + 1) * HEAD_DIM:(2 * hh + 2) * HEAD_DIM] = ones
    for t in range(IDX_HEADS * IDX_DIM // LANES):
        c0 = OFF_IQ + t * LANES
        r = _rope(proj[:, c0:c0 + LANES], cos_i, sin_i, half_i, IDX_ROPE_DIM // 2)
        iq_ref[:, t * LANES:(t + 1) * LANES] = r.astype(BF16)
    tail = proj[:, OFF_IK:OFF_IK + LANES]
    r = _rope(tail, cos_i, sin_i, half_i, IDX_ROPE_DIM // 2)
    ik_ref[...] = r[:, 0:IDX_DIM].astype(BF16)
    iw_ref[...] = tail[:, IDX_DIM:IDX_DIM + IDX_HEADS]


def _rope_freqs():
    f_q = ROPE_THETA ** (-jnp.arange(0, ROPE_DIM, 2, dtype=F32) / ROPE_DIM)
    f_i = ROPE_THETA ** (-jnp.arange(0, IDX_ROPE_DIM, 2, dtype=F32) / IDX_ROPE_DIM)
    return jnp.concatenate([f_q, f_i]).reshape(-1, 1)


def _inproj(x2, pos2, g, b, w_pad, tq):
    T = x2.shape[0]
    row = lambda i: (i, 0)
    fixed = lambda i: (0, 0)
    out_shapes = (
        jax.ShapeDtypeStruct((T, POOL_WIDTH), F32),
        jax.ShapeDtypeStruct((T, ATTN_WIDTH), BF16),
        jax.ShapeDtypeStruct((T, KV_WIDTH), BF16),
        jax.ShapeDtypeStruct((T, 2 * KV_WIDTH), BF16),
        jax.ShapeDtypeStruct((T, IDX_HEADS * IDX_DIM), BF16),
        jax.ShapeDtypeStruct((T, IDX_DIM), BF16),
        jax.ShapeDtypeStruct((T, IDX_HEADS), F32),
    )
    return pl.pallas_call(
        _inproj_kernel,
        grid=(T // tq,),
        in_specs=[
            pl.BlockSpec((tq, D_MODEL), row),
            pl.BlockSpec((1, tq), lambda i: (0, i)),
            pl.BlockSpec((1, D_MODEL), fixed),
            pl.BlockSpec((1, D_MODEL), fixed),
            pl.BlockSpec((D_MODEL, IN_WIDTH_PAD), fixed),
            pl.BlockSpec((ROPE_DIM // 2 + IDX_ROPE_DIM // 2, 1), fixed),
        ],
        out_specs=tuple(pl.BlockSpec((tq, s.shape[1]), row) for s in out_shapes),
        out_shape=out_shapes,
        compiler_params=pltpu.CompilerParams(
            dimension_semantics=("parallel",), vmem_limit_bytes=VMEM_LIMIT),
    )(x2, pos2, g, b, w_pad, _rope_freqs())


def _row_count(mask):
    return jnp.sum(jnp.where(mask, 1.0, 0.0), axis=-1, keepdims=True)


def _select_bias(iq_ref, ik_ref, iw_ref, key_ref, bias_ref, hk_ref, *, i, tq, S, n_sel):
    q_pos = i * tq + lax.broadcasted_iota(jnp.int32, (tq, S), 0)
    k_pos = lax.broadcasted_iota(jnp.int32, (tq, S), 1)
    causal = k_pos <= q_pos
    if S <= n_sel:
        bias_ref[:, 0:S] = jnp.where(causal, 0.0, NEG_INF)
        return

    w_scale = (IDX_HEADS ** -0.5) * (IDX_DIM ** -0.5)
    iw = iw_ref[...] * w_scale
    ik = ik_ref[0:S, :]
    nt = (((1,), (1,)), ((), ()))
    sc = jnp.zeros((tq, S), F32)
    for hh in range(IDX_HEADS):
        s = lax.dot_general(iq_ref[:, hh * IDX_DIM:(hh + 1) * IDX_DIM], ik, nt,
                            preferred_element_type=F32)
        sc = sc + jnp.maximum(s, 0.0) * iw[:, hh:hh + 1]
    sc = jnp.where(jnp.abs(sc) < F32_TINY, 0.0, sc)
    sc = jnp.where(causal, sc, NEG_INF)
    bits = pltpu.bitcast(sc, jnp.int32)
    key_ref[:, 0:S] = bits ^ ((bits >> 31) & jnp.int32(0x7FFFFFFF))
    hi_mask = jnp.int32(-65536)
    hk_ref[:, 0:S] = pltpu.bitcast(bits & hi_mask, F32).astype(BF16)

    one = jnp.ones((tq, LANES), BF16)
    zero = jnp.zeros((tq, LANES), BF16)

    def coarse_step(b, t_u):
        cand = t_u | lax.shift_left(jnp.int32(1), 31 - b)
        fbits = jnp.where(cand < 0, cand ^ jnp.int32(INT_MIN), ~cand) & hi_mask
        cand_f = jnp.where((fbits & jnp.int32(0x7F800000)) == 0,
                           jnp.where(fbits > 0, F32_TINY, 0.0), pltpu.bitcast(fbits, F32))
        cand_f = jnp.broadcast_to(cand_f, (tq, LANES)).astype(BF16)
        acc = zero
        for c0 in range(0, S, LANES):
            acc = acc + jnp.where(hk_ref[:, c0:c0 + LANES] >= cand_f, one, zero)
        cnt = jnp.sum(acc.astype(F32), axis=-1, keepdims=True)
        return jnp.where(cnt >= n_sel, cand, t_u)

    def value_step(b, t_u):
        cand = t_u | lax.shift_left(jnp.int32(1), 31 - b)
        cnt = _row_count(key_ref[:, 0:S] >= (cand ^ jnp.int32(INT_MIN)))
        return jnp.where(cnt >= n_sel, cand, t_u)

    def unrolled(step):
        def steps(bb, t_u):
            for r in range(SEARCH_UNROLL):
                t_u = step(bb * SEARCH_UNROLL + r, t_u)
            return t_u
        return steps

    t_u = lax.fori_loop(0, 16 // SEARCH_UNROLL, unrolled(coarse_step),
                        jnp.zeros((tq, 1), jnp.int32))
    t_u = lax.fori_loop(16 // SEARCH_UNROLL, 32 // SEARCH_UNROLL, unrolled(value_step), t_u)
    thr = t_u ^ jnp.int32(INT_MIN)
    key = key_ref[:, 0:S]
    need = n_sel - _row_count(key > thr)
    tied = key == thr
    bias_ref[:, 0:S] = jnp.where((key >= thr) & causal, 0.0, NEG_INF)
    excess = _row_count(tied & causal) > need
    any_excess = jnp.max(jnp.where(excess, 1.0, 0.0)) > 0.0

    @pl.when(any_excess)
    def _():
        r = lax.broadcasted_iota(jnp.int32, (TIE_BLOCK, TIE_BLOCK), 0)
        cc = lax.broadcasted_iota(jnp.int32, (TIE_BLOCK, TIE_BLOCK), 1)
        upto = jnp.where(r <= cc, 1.0, 0.0).astype(BF16)
        qp = i * tq + lax.broadcasted_iota(jnp.int32, (tq, TIE_BLOCK), 0)
        carry = jnp.zeros((tq, 1), F32)
        for c0 in range(0, S, TIE_BLOCK):
            kb = key_ref[:, c0:c0 + TIE_BLOCK]
            cz = (c0 + lax.broadcasted_iota(jnp.int32, (tq, TIE_BLOCK), 1)) <= qp
            tb = (kb == thr) & cz
            seen = jnp.dot(jnp.where(tb, 1.0, 0.0).astype(BF16), upto,
                           preferred_element_type=F32) + carry
            admit = ((kb > thr) & cz) | (tb & (seen <= need))
            bias_ref[:, c0:c0 + TIE_BLOCK] = jnp.where(admit, 0.0, NEG_INF)
            carry = seen[:, TIE_BLOCK - 1:TIE_BLOCK]


def _attn_tile(q_ref, k_ref, v_ref, iq_ref, ik_ref, iw_ref, o_ref, key_ref, bias_ref, hk_ref,
               *, i, tq, S, n_sel):
    _select_bias(iq_ref, ik_ref, iw_ref, key_ref, bias_ref, hk_ref,
                 i=i, tq=tq, S=S, n_sel=n_sel)
    nt = (((1,), (1,)), ((), ()))
    for hh in range(N_HEADS):
        g = hh // (N_HEADS // N_KV_HEADS)
        kg = k_ref[0:S, g * HEAD_DIM:(g + 1) * HEAD_DIM]
        vg = v_ref[0:S, 2 * g * HEAD_DIM:2 * (g + 1) * HEAD_DIM]
        logits = lax.dot_general(q_ref[:, hh * HEAD_DIM:(hh + 1) * HEAD_DIM], kg, nt,
                                 preferred_element_type=F32) + bias_ref[:, 0:S]
        m = jnp.max(logits, axis=-1, keepdims=True)
        p = jnp.exp((logits - m).astype(BF16))
        ol = jnp.dot(p, vg, preferred_element_type=F32)
        o = ol[:, 0:HEAD_DIM] / ol[:, HEAD_DIM:HEAD_DIM + 1]
        o_ref[:, hh * HEAD_DIM:(hh + 1) * HEAD_DIM] = o.astype(BF16)


def _attn_kernel(*refs, tq, L, n_sel):
    i = pl.program_id(1)
    band = L // CAUSAL_BANDS
    for v in range(CAUSAL_BANDS):
        @pl.when((i * tq) // band == v)
        def _(v=v):
            _attn_tile(*refs, i=i, tq=tq, S=(v + 1) * band, n_sel=n_sel)


def _attn(q, k, v, iq, ik, iw, B, L, tq):
    T = B * L
    nq = L // tq
    n_sel = min(TOPK_MAX, L // 4)
    assert L % (CAUSAL_BANDS * tq) == 0
    qrow = lambda b, i: (b * nq + i, 0)
    seq = lambda b, i: (b, 0)
    return pl.pallas_call(
        functools.partial(_attn_kernel, tq=tq, L=L, n_sel=n_sel),
        grid=(B, nq),
        in_specs=[
            pl.BlockSpec((tq, ATTN_WIDTH), qrow),
            pl.BlockSpec((L, KV_WIDTH), seq),
            pl.BlockSpec((L, 2 * KV_WIDTH), seq),
            pl.BlockSpec((tq, IDX_HEADS * IDX_DIM), qrow),
            pl.BlockSpec((L, IDX_DIM), seq),
            pl.BlockSpec((tq, IDX_HEADS), qrow),
        ],
        out_specs=pl.BlockSpec((tq, ATTN_WIDTH), qrow),
        out_shape=jax.ShapeDtypeStruct((T, ATTN_WIDTH), BF16),
        scratch_shapes=[pltpu.VMEM((tq, L), jnp.int32), pltpu.VMEM((tq, L), F32),
                        pltpu.VMEM((tq, L), BF16)],
        compiler_params=pltpu.CompilerParams(
            dimension_semantics=("parallel", "arbitrary"), vmem_limit_bytes=VMEM_LIMIT),
    )(q, k, v, iq, ik, iw)


def _outproj_kernel(x_ref, u_ref, halo_ref, a_ref, gin_ref, bin_ref, wpool_ref, pscale_ref,
                    wout_ref, g1_ref, b1_ref, wr_hi_ref, wr_lo_ref, br_ref,
                    h1_ref, h1b_ref, ids_ref, gates_ref, *, tq, tiles_per_seq):
    i = pl.program_id(0)
    seq_tile = i % tiles_per_seq
    u = u_ref[...]
    halo = jnp.where(seq_tile == 0, 0.0, halo_ref[...])
    ext = jnp.concatenate([halo, u], axis=0)
    lane = lax.broadcasted_iota(jnp.int32, (tq, POOL_WIDTH), 1)
    grp = lane // POOL_GDIM
    win = jnp.zeros((tq, POOL_WIDTH), F32)
    s = ext
    for gi, w in enumerate(POOL_WINDOWS):
        s = s + pltpu.roll(s, w // 2, 0)
        win = jnp.where(grp == gi, s[POOL_HALO:, :], win)
    t_seq = seq_tile * tq + lax.broadcasted_iota(jnp.int32, (tq, POOL_WIDTH), 0)
    width = lax.shift_left(jnp.int32(2), grp)
    cnt = jnp.minimum(t_seq + 1, width).astype(F32)
    d = win / cnt - u
    y_pool = jnp.dot(d.astype(BF16), wpool_ref[...], preferred_element_type=F32) * pscale_ref[...]

    mix = jnp.dot(y_pool.astype(BF16), wout_ref[0:POOL_WIDTH, :], preferred_element_type=F32)
    mix = mix + jnp.dot(a_ref[...], wout_ref[POOL_WIDTH:, :], preferred_element_type=F32)
    h = _layer_norm(x_ref[...], gin_ref[...], bin_ref[...])
    h1 = _layer_norm(ALPHA_RES * h + mix, g1_ref[...], b1_ref[...])
    h1_ref[...] = h1
    h1_hi = h1.astype(BF16)
    h1b_ref[...] = h1_hi
    h1_lo = (h1 - h1_hi.astype(F32)).astype(BF16)
    wr_hi = wr_hi_ref[...]
    logits = (jnp.dot(h1_hi, wr_hi, preferred_element_type=F32)
              + jnp.dot(h1_lo, wr_hi, preferred_element_type=F32)
              + jnp.dot(h1_hi, wr_lo_ref[...], preferred_element_type=F32)) + br_ref[...]
    lane_e = lax.broadcasted_iota(jnp.int32, (tq, LANES), 1).astype(F32)
    lg = jnp.where(lane_e < N_EXPERTS, logits, -jnp.inf)
    ids = jnp.zeros((tq, LANES), F32)
    ex = jnp.zeros((tq, LANES), F32)
    top0 = None
    for kk in range(TOP_K):
        m = jnp.max(lg, axis=-1, keepdims=True)
        idx = jnp.min(jnp.where(lg == m, lane_e, float(LANES)), axis=-1, keepdims=True)
        if top0 is None:
            top0 = m
        ids = jnp.where(lane_e == kk, idx, ids)
        ex = jnp.where(lane_e == kk, jnp.exp(m - top0), ex)
        lg = jnp.where(lane_e == idx, -jnp.inf, lg)
    ids_ref[...] = ids.astype(jnp.int32)
    gates_ref[...] = ex / jnp.sum(ex, axis=-1, keepdims=True)


def _outproj(x2, u, attn, gin, bin_, wpool_bd, pscale, wout, g1, b1, wr_hi, wr_lo, br, L, tq):
    T = x2.shape[0]
    row = lambda i: (i, 0)
    fixed = lambda i: (0, 0)
    halo = lambda i: (jnp.maximum(i * (tq // POOL_HALO) - 1, 0), 0)
    out_shapes = (
        jax.ShapeDtypeStruct((T, D_MODEL), F32),
        jax.ShapeDtypeStruct((T, D_MODEL), BF16),
        jax.ShapeDtypeStruct((T, LANES), jnp.int32),
        jax.ShapeDtypeStruct((T, LANES), F32),
    )
    return pl.pallas_call(
        functools.partial(_outproj_kernel, tq=tq, tiles_per_seq=L // tq),
        grid=(T // tq,),
        in_specs=[
            pl.BlockSpec((tq, D_MODEL), row),
            pl.BlockSpec((tq, POOL_WIDTH), row),
            pl.BlockSpec((POOL_HALO, POOL_WIDTH), halo),
            pl.BlockSpec((tq, ATTN_WIDTH), row),
            pl.BlockSpec((1, D_MODEL), fixed),
            pl.BlockSpec((1, D_MODEL), fixed),
            pl.BlockSpec((POOL_WIDTH, POOL_WIDTH), fixed),
            pl.BlockSpec((1, POOL_WIDTH), fixed),
            pl.BlockSpec((D_MODEL, D_MODEL), fixed),
            pl.BlockSpec((1, D_MODEL), fixed),
            pl.BlockSpec((1, D_MODEL), fixed),
            pl.BlockSpec((D_MODEL, LANES), fixed),
            pl.BlockSpec((D_MODEL, LANES), fixed),
            pl.BlockSpec((1, LANES), fixed),
        ],
        out_specs=tuple(pl.BlockSpec((tq, s.shape[1]), row) for s in out_shapes),
        out_shape=out_shapes,
        compiler_params=pltpu.CompilerParams(
            dimension_semantics=("parallel",), vmem_limit_bytes=VMEM_LIMIT),
    )(x2, u, u, attn, gin, bin_, wpool_bd, pscale, wout, g1, b1, wr_hi, wr_lo, br)


ROUTE_TB = 512
ROW_ALIGN = 16
RUN_BITS = 6
STAGE_ROWS = 2560
STAGE_CHUNK = 512
COMBINE_CHUNK = 1280
FFN_TM = 512
TAIL_BITS = 5


def _route_kernel(ids_ref, slot_ref, slot_t_ref, meta_ref, carry_ref):
    c = pl.program_id(0)

    @pl.when(c == 0)
    def _():
        carry_ref[...] = jnp.zeros(carry_ref.shape, F32)

    tb = ids_ref.shape[0]
    ids = ids_ref[...]
    lane = lax.broadcasted_iota(jnp.int32, (tb, LANES), 1)
    onehot = [jnp.where(lane == ids[:, k:k + 1], 1.0, 0.0) for k in range(TOP_K)]
    member = onehot[0] + onehot[1] + onehot[2] + onehot[3]
    r = lax.broadcasted_iota(jnp.int32, (tb, tb), 0)
    cc = lax.broadcasted_iota(jnp.int32, (tb, tb), 1)
    before = jnp.where(cc < r, 1.0, 0.0).astype(BF16)
    lrank = jnp.dot(before, member.astype(BF16), preferred_element_type=F32)
    n = jnp.sum(member, axis=0, keepdims=True)
    units = jnp.ceil(n * (1.0 / ROW_ALIGN))
    er = lax.broadcasted_iota(jnp.int32, (LANES, LANES), 0)
    ec = lax.broadcasted_iota(jnp.int32, (LANES, LANES), 1)
    lower = jnp.where(er < ec, 1.0, 0.0).astype(BF16)
    off_units = jnp.dot(jnp.broadcast_to(units, (8, LANES)).astype(BF16), lower,
                        preferred_element_type=F32)[0:1, :]
    base = off_units * ROW_ALIGN + lrank
    slot = jnp.zeros((tb, LANES), jnp.int32)
    for k in range(TOP_K):
        sk = jnp.sum(onehot[k] * base, axis=-1, keepdims=True).astype(jnp.int32)
        slot = jnp.where(lane == k, sk, slot)
    slot_ref[...] = slot
    slot_t_ref[...] = slot.T[0:8, :]
    row = lax.broadcasted_iota(jnp.int32, (8, LANES), 0)
    meta = jnp.where(row == 0, units, jnp.where(row == 1, off_units, carry_ref[...]))
    meta_ref[...] = meta.astype(jnp.int32)
    carry_ref[...] = carry_ref[...] + units


def _route(ids):
    T = ids.shape[0]
    nblk = T // ROUTE_TB
    return pl.pallas_call(
        _route_kernel,
        grid=(nblk,),
        in_specs=[pl.BlockSpec((ROUTE_TB, LANES), lambda c: (c, 0))],
        out_specs=(pl.BlockSpec((ROUTE_TB, LANES), lambda c: (c, 0)),
                   pl.BlockSpec((8, ROUTE_TB), lambda c: (0, c)),
                   pl.BlockSpec((8, LANES), lambda c: (c, 0))),
        out_shape=(jax.ShapeDtypeStruct((T, LANES), jnp.int32),
                   jax.ShapeDtypeStruct((8, T), jnp.int32),
                   jax.ShapeDtypeStruct((nblk * 8, LANES), jnp.int32)),
        scratch_shapes=[pltpu.VMEM((1, LANES), F32)],
        compiler_params=pltpu.CompilerParams(
            dimension_semantics=("arbitrary",), vmem_limit_bytes=VMEM_LIMIT),
    )(ids)


def _pieces_of(m, so, do, bits, visit):
    for b in bits:
        done = m & ((1 << b) - 1)

        @pl.when(((m >> b) & 1) == 1)
        def _(b=b, done=done):
            visit(b, pl.multiple_of((so + done) * ROW_ALIGN, ROW_ALIGN),
                  pl.multiple_of((do + done) * ROW_ALIGN, ROW_ALIGN), ROW_ALIGN << b)


def _piece_lists(units, off_units, dst_units):
    bits = jnp.arange(RUN_BITS, dtype=jnp.int32)[None, :, None]
    m = units[:, None, :]
    has = ((m >> bits) & 1) == 1
    done = m & ((1 << bits) - 1)
    pos = jnp.cumsum(has, axis=-1) - 1
    at = has[..., None] & (pos[..., None] == jnp.arange(N_EXPERTS, dtype=jnp.int32))
    pick = lambda v: jnp.sum(jnp.where(at, v[..., None], 0), axis=2).reshape(-1).astype(jnp.int32)
    cnt = jnp.sum(has, axis=-1).reshape(-1).astype(jnp.int32)
    return cnt, pick(off_units[:, None, :] + done), pick(dst_units[:, None, :] + done)


def _run_pieces(cnt_ref, srow_ref, grow_ref, c, visit):
    for b in range(RUN_BITS):
        base = (c * RUN_BITS + b) * N_EXPERTS

        def body(j, carry, b=b, base=base):
            visit(b, j, pl.multiple_of(srow_ref[base + j] * ROW_ALIGN, ROW_ALIGN),
                  pl.multiple_of(grow_ref[base + j] * ROW_ALIGN, ROW_ALIGN), ROW_ALIGN << b)
            return carry

        lax.fori_loop(0, cnt_ref[c * RUN_BITS + b], body, 0)


def _dispatch_kernel(cnt_ref, srow_ref, grow_ref, tail_ref, tail_dst_ref, slot_t_ref, h_ref,
                     xs_ref, stage_ref, zero_ref, sems, *, nblk):
    c = pl.program_id(0)
    par = c % 2
    tb = h_ref.shape[0]
    for r0 in range(0, STAGE_ROWS, STAGE_CHUNK):
        srow = r0 + lax.broadcasted_iota(jnp.int32, (STAGE_CHUNK, tb), 0)
        hit = srow == slot_t_ref[0:1, :]
        for k in range(1, TOP_K):
            hit = hit | (srow == slot_t_ref[k:k + 1, :])
        perm = jnp.where(hit, 1.0, 0.0).astype(BF16)
        stage_ref[par, r0:r0 + STAGE_CHUNK, :] = jnp.dot(
            perm, h_ref[...], preferred_element_type=F32).astype(BF16)

    def piece(p):
        return lambda b, j, s_row, g_row, rows: pltpu.make_async_copy(
            stage_ref.at[p, pl.ds(s_row, rows), :], xs_ref.at[pl.ds(g_row, rows), :],
            sems.at[p, j, b])

    _run_pieces(cnt_ref, srow_ref, grow_ref, c, lambda *a: piece(par)(*a).start())

    @pl.when(c > 0)
    def _():
        _run_pieces(cnt_ref, srow_ref, grow_ref, c - 1, lambda *a: piece(1 - par)(*a).wait())

    @pl.when(c == nblk - 1)
    def _():
        _run_pieces(cnt_ref, srow_ref, grow_ref, c, lambda *a: piece(par)(*a).wait())
        zero_ref[...] = jnp.zeros(zero_ref.shape, BF16)

        def tail(e, b, z_row, g_row, rows):
            del z_row
            return pltpu.make_async_copy(zero_ref.at[pl.ds(0, rows), :],
                                         xs_ref.at[pl.ds(g_row, rows), :], sems.at[0, e, b])

        def tails(act):
            for e in range(N_EXPERTS):
                _pieces_of(tail_ref[e], 0, tail_dst_ref[e], range(TAIL_BITS),
                           lambda b, z, g, rows, e=e: act(tail(e, b, z, g, rows)))

        tails(lambda cp: cp.start())
        tails(lambda cp: cp.wait())

        zrows = zero_ref.shape[0]
        used = (tail_dst_ref[N_EXPERTS - 1] + tail_ref[N_EXPERTS - 1]) * ROW_ALIGN
        n_rest = (xs_ref.shape[0] - used) // zrows

        def rest(i):
            row0 = pl.multiple_of(used + i * zrows, ROW_ALIGN)
            return pltpu.make_async_copy(
                zero_ref, xs_ref.at[pl.ds(row0, zrows), :],
                sems.at[i // (N_EXPERTS * RUN_BITS), (i // RUN_BITS) % N_EXPERTS, i % RUN_BITS])

        lax.fori_loop(0, n_rest, lambda i, _: (rest(i).start(), 0)[1], 0)
        lax.fori_loop(0, n_rest, lambda i, _: (rest(i).wait(), 0)[1], 0)


def _dispatch(units, off, dst, tail, tail_dst, slot_t, h1b, n_rows):
    T = h1b.shape[0]
    nblk = T // ROUTE_TB
    grid_spec = pltpu.PrefetchScalarGridSpec(
        num_scalar_prefetch=5,
        grid=(nblk,),
        in_specs=[pl.BlockSpec((8, ROUTE_TB), lambda c, *_: (0, c)),
                  pl.BlockSpec((ROUTE_TB, D_MODEL), lambda c, *_: (c, 0))],
        out_specs=pl.BlockSpec(memory_space=pl.ANY),
        scratch_shapes=[pltpu.VMEM((2, STAGE_ROWS, D_MODEL), BF16),
                        pltpu.VMEM((ROW_ALIGN << (TAIL_BITS - 1), D_MODEL), BF16),
                        pltpu.SemaphoreType.DMA((2, N_EXPERTS, RUN_BITS))],
    )
    return pl.pallas_call(
        functools.partial(_dispatch_kernel, nblk=nblk),
        grid_spec=grid_spec,
        out_shape=jax.ShapeDtypeStruct((n_rows, D_MODEL), BF16),
        compiler_params=pltpu.CompilerParams(
            dimension_semantics=("arbitrary",), vmem_limit_bytes=VMEM_LIMIT),
    )(units, off, dst, tail, tail_dst, slot_t, h1b)


def _ffn_kernel(tile_e_ref, n_tiles_ref, xs_ref, wgu_ref, bgu_ref, wd_ref, bd_ref, y_ref,
                wgu_b, wd_b, *, n_chunk):
    j = pl.program_id(0)
    active = j < n_tiles_ref[0]
    new_expert = (j == 0) | (tile_e_ref[j] != tile_e_ref[jnp.maximum(j - 1, 0)])

    def tile(refresh):
        def weights(dst, src, rows, cols):
            if refresh:
                dst[rows, cols] = src[0, rows, cols].astype(BF16)
            return dst[rows, cols]

        xs = xs_ref[...]
        cw = D_FF // n_chunk
        full = slice(None)
        acc = jnp.zeros(y_ref.shape, F32)
        for c in range(n_chunk):
            cols = slice(c * cw, (c + 1) * cw)
            ucols = slice(D_FF + c * cw, D_FF + (c + 1) * cw)
            gate = jnp.dot(xs, weights(wgu_b, wgu_ref, full, cols), preferred_element_type=F32)
            gate = gate + bgu_ref[0, :, cols]
            up = jnp.dot(xs, weights(wgu_b, wgu_ref, full, ucols), preferred_element_type=F32)
            up = up + bgu_ref[0, :, ucols]
            gate = jnp.minimum(gate, SWIGLU_LIMIT)
            up = jnp.clip(up, -SWIGLU_LIMIT, SWIGLU_LIMIT)
            act = (up + 1.0) * gate * jax.nn.sigmoid(SWIGLU_ALPHA * gate)
            acc = acc + jnp.dot(act.astype(BF16), weights(wd_b, wd_ref, cols, full),
                                preferred_element_type=F32)
        y_ref[...] = (acc + bd_ref[0]).astype(BF16)

    pl.when(active & new_expert)(lambda: tile(True))
    pl.when(active & jnp.logical_not(new_expert))(lambda: tile(False))

    @pl.when(jnp.logical_not(active))
    def _():
        y_ref[...] = jnp.zeros(y_ref.shape, BF16)


def _ffn(tile_e, n_tiles, xs, wgu, bgu, wd, bd):
    n_rows = xs.shape[0]
    last = lambda j, nt: jnp.minimum(j, nt[0] - 1)
    row = lambda j, te, nt: (last(j, nt), 0)
    exp3 = lambda j, te, nt: (te[last(j, nt)], 0, 0)
    grid_spec = pltpu.PrefetchScalarGridSpec(
        num_scalar_prefetch=2,
        grid=(n_rows // FFN_TM,),
        in_specs=[
            pl.BlockSpec((FFN_TM, D_MODEL), row),
            pl.BlockSpec((1, D_MODEL, 2 * D_FF), exp3),
            pl.BlockSpec((1, 1, 2 * D_FF), exp3),
            pl.BlockSpec((1, D_FF, D_MODEL), exp3),
            pl.BlockSpec((1, 1, D_MODEL), exp3),
        ],
        out_specs=pl.BlockSpec((FFN_TM, D_MODEL), lambda j, te, nt: (j, 0)),
        scratch_shapes=[pltpu.VMEM((D_MODEL, 2 * D_FF), BF16), pltpu.VMEM((D_FF, D_MODEL), BF16)],
    )
    return pl.pallas_call(
        functools.partial(_ffn_kernel, n_chunk=2),
        grid_spec=grid_spec,
        out_shape=jax.ShapeDtypeStruct((n_rows, D_MODEL), BF16),
        compiler_params=pltpu.CompilerParams(
            dimension_semantics=("arbitrary",), vmem_limit_bytes=FFN_VMEM_LIMIT),
    )(tile_e, n_tiles, xs, wgu, bgu, wd, bd)


def _combine_kernel(cnt_ref, srow_ref, grow_ref, slot_ref, gates_ref, h1_ref, ys_ref,
                    g_ref, b_ref, o_ref, stage_ref, sems, *, nblk):
    c = pl.program_id(0)
    par = c % 2
    tb = h1_ref.shape[0]

    def piece(p):
        return lambda b, j, s_row, g_row, rows: pltpu.make_async_copy(
            ys_ref.at[pl.ds(g_row, rows), :], stage_ref.at[p, pl.ds(s_row, rows), :],
            sems.at[p, j, b])

    @pl.when(c == 0)
    def _():
        stage_ref[...] = jnp.zeros(stage_ref.shape, BF16)
        _run_pieces(cnt_ref, srow_ref, grow_ref, c, lambda *a: piece(0)(*a).start())

    _run_pieces(cnt_ref, srow_ref, grow_ref, c, lambda *a: piece(par)(*a).wait())

    @pl.when(c + 1 < nblk)
    def _():
        _run_pieces(cnt_ref, srow_ref, grow_ref, c + 1, lambda *a: piece(1 - par)(*a).start())

    slot = slot_ref[...]
    gates = gates_ref[...]
    ffn = jnp.zeros((tb, D_MODEL), F32)
    for r0 in range(0, STAGE_ROWS, COMBINE_CHUNK):
        scol = r0 + lax.broadcasted_iota(jnp.int32, (tb, COMBINE_CHUNK), 1)
        w = jnp.zeros((tb, COMBINE_CHUNK), F32)
        for k in range(TOP_K):
            w = w + jnp.where(scol == slot[:, k:k + 1], gates[:, k:k + 1], 0.0)
        ffn = ffn + jnp.dot(w.astype(BF16), stage_ref[par, r0:r0 + COMBINE_CHUNK, :],
                            preferred_element_type=F32)
    o_ref[...] = _layer_norm(ALPHA_RES * h1_ref[...] + ffn, g_ref[...], b_ref[...])


def _combine(units, off, dst, slot, gates, h1, ys, g, b):
    T = h1.shape[0]
    nblk = T // ROUTE_TB
    blk = lambda c, *_: (c, 0)
    fixed = lambda c, *_: (0, 0)
    grid_spec = pltpu.PrefetchScalarGridSpec(
        num_scalar_prefetch=3,
        grid=(nblk,),
        in_specs=[pl.BlockSpec((ROUTE_TB, LANES), blk),
                  pl.BlockSpec((ROUTE_TB, LANES), blk),
                  pl.BlockSpec((ROUTE_TB, D_MODEL), blk),
                  pl.BlockSpec(memory_space=pl.ANY),
                  pl.BlockSpec((1, D_MODEL), fixed),
                  pl.BlockSpec((1, D_MODEL), fixed)],
        out_specs=pl.BlockSpec((ROUTE_TB, D_MODEL), blk),
        scratch_shapes=[pltpu.VMEM((2, STAGE_ROWS, D_MODEL), BF16),
                        pltpu.SemaphoreType.DMA((2, N_EXPERTS, RUN_BITS))],
    )
    return pl.pallas_call(
        functools.partial(_combine_kernel, nblk=nblk),
        grid_spec=grid_spec,
        out_shape=jax.ShapeDtypeStruct((T, D_MODEL), F32),
        compiler_params=pltpu.CompilerParams(
            dimension_semantics=("arbitrary",), vmem_limit_bytes=VMEM_LIMIT),
    )(units, off, dst, slot, gates, h1, ys, g, b)


def kernel(x, positions, ln_in_g, ln_in_b, w_in, w_pool, pool_scale, w_out, ln1_g, ln1_b,
           w_router, b_router, w_gate_up, b_gate_up, w_down, b_down, ln2_g, ln2_b):
    B, L, D = x.shape
    T = B * L
    assert T % ROUTE_TB == 0 and D == D_MODEL
    tq = min(256, L)
    x2 = x.reshape(T, D)
    pos2 = positions.reshape(1, T)
    gin = ln_in_g.reshape(1, D)
    bin_ = ln_in_b.reshape(1, D)

    w_pad = jnp.pad(w_in[0], ((0, 0), (0, IN_WIDTH_PAD - IN_WIDTH))).astype(BF16)
    u, q, k, v, iq, ik, iw = _inproj(x2, pos2, gin, bin_, w_pad, tq)

    attn = _attn(q, k, v, iq, ik, iw, B, L, min(256, L // CAUSAL_BANDS))

    wpool_bd = jnp.zeros((POOL_WIDTH, POOL_WIDTH), F32)
    for gi in range(POOL_GROUPS):
        sl = slice(gi * POOL_GDIM, (gi + 1) * POOL_GDIM)
        wpool_bd = wpool_bd.at[sl, sl].set(w_pool[0, gi])
    wr = jnp.pad(w_router[0], ((0, 0), (0, LANES - N_EXPERTS)))
    wr_hi = wr.astype(BF16)
    wr_lo = (wr - wr_hi.astype(F32)).astype(BF16)
    br = jnp.pad(b_router[0], (0, LANES - N_EXPERTS)).reshape(1, LANES)
    h1, h1b, ids, gates = _outproj(
        x2, u, attn, gin, bin_, wpool_bd.astype(BF16), pool_scale[0].reshape(1, POOL_WIDTH),
        w_out[0].astype(BF16), ln1_g[0].reshape(1, D), ln1_b[0].reshape(1, D),
        wr_hi, wr_lo, br, L, tq)

    nblk = T // ROUTE_TB
    slot, slot_t, meta = _route(ids)
    meta = meta.reshape(nblk, 8, LANES)[:, :, :N_EXPERTS]
    units, off_units, base_units = meta[:, 0], meta[:, 1], meta[:, 2]
    tile_units = FFN_TM // ROW_ALIGN
    total_units = base_units[-1] + units[-1]
    region_units = ((total_units + tile_units - 1) // tile_units) * tile_units
    region_end = jnp.cumsum(region_units)
    dst_units = (region_end - region_units)[None, :] + base_units
    max_rows = T * TOP_K + nblk * N_EXPERTS * (ROW_ALIGN - 1) + N_EXPERTS * (FFN_TM - 1)
    n_rows = -(-max_rows // FFN_TM) * FFN_TM
    tile_start = jnp.arange(n_rows // FFN_TM, dtype=jnp.int32) * tile_units
    tile_e = jnp.minimum(jnp.sum(tile_start[:, None] >= region_end[None, :], axis=1),
                         N_EXPERTS - 1).astype(jnp.int32)
    n_tiles = (region_end[-1:] // tile_units).astype(jnp.int32)
    cnt, srow, grow = _piece_lists(units, off_units, dst_units)
    tail_units = (region_units - total_units).astype(jnp.int32)
    tail_dst = (region_end - tail_units).astype(jnp.int32)

    xs = _dispatch(cnt, srow, grow, tail_units, tail_dst, slot_t, h1b, n_rows)
    ys = _ffn(tile_e, n_tiles, xs, w_gate_up[0], b_gate_up[0].reshape(N_EXPERTS, 1, 2 * D_FF),
              w_down[0], b_down[0].reshape(N_EXPERTS, 1, D))
    out = _combine(cnt, srow, grow, slot, gates, h1, ys,
                   ln2_g[0].reshape(1, D), ln2_b[0].reshape(1, D))
    return out.reshape(B, L, D)
```

```python
import functools

import jax
import jax.numpy as jnp
from jax import lax
from jax.experimental import pallas as pl
from jax.experimental.pallas import tpu as pltpu

F32 = jnp.float32
BF16 = jnp.bfloat16

D_MODEL = 1024
POOL_WIDTH = 256
POOL_GROUPS = 4
POOL_GDIM = 64
POOL_WINDOWS = (2, 4, 8, 16)
POOL_HALO = 16
HEAD_DIM = 128
ATTN_WIDTH = 768
N_HEADS = 6
N_KV_HEADS = 2
KV_WIDTH = 256
IDX_HEADS = 8
IDX_DIM = 64
TOPK_MAX = 256
ROPE_THETA = 500000.0
ROPE_DIM = 32
IDX_ROPE_DIM = 16
N_EXPERTS = 32
TOP_K = 4
D_FF = 1024
SWIGLU_LIMIT = 7.0
SWIGLU_ALPHA = 1.702
DEPTH = 1
ALPHA_RES = (2.0 * DEPTH) ** 0.25
LN_EPS = 1e-5
NEG_INF = -1e30
OFF_Q = 256
OFF_K = 1024
OFF_V = 1280
OFF_IQ = 1536
OFF_IK = 2048
IN_WIDTH = 2120
IN_WIDTH_PAD = 2176

LANES = 128
VMEM_LIMIT = 48 * 1024 * 1024
FFN_VMEM_LIMIT = 56 * 1024 * 1024
INT_MIN = -2 ** 31
TIE_BLOCK = 256
CAUSAL_BANDS = 4
SEARCH_UNROLL = 8


def _layer_norm(x, g, b):
    mu = jnp.mean(x, axis=-1, keepdims=True)
    xc = x - mu
    var = jnp.mean(xc * xc, axis=-1, keepdims=True)
    return xc * lax.rsqrt(var + LN_EPS) * g + b


def _rope(xh, cos, sin, first_half, half):
    partner = jnp.where(first_half, pltpu.roll(xh, LANES - half, 1), pltpu.roll(xh, half, 1))
    return xh * cos + partner * sin


def _inproj_kernel(x_ref, pos_ref, g_ref, b_ref, w_ref, freq_ref,
                   u_ref, q_ref, k_ref, v_ref, iq_ref, ik_ref, iw_ref):
    tq = x_ref.shape[0]
    h = _layer_norm(x_ref[...], g_ref[...], b_ref[...])
    proj = jnp.dot(h.astype(BF16), w_ref[...], preferred_element_type=F32)

    pos = pos_ref[...].astype(F32)
    ang_q = freq_ref[0:ROPE_DIM // 2, :] * pos
    ang_i = freq_ref[ROPE_DIM // 2:ROPE_DIM // 2 + IDX_ROPE_DIM // 2, :] * pos
    cq, sq = jnp.cos(ang_q), jnp.sin(ang_q)
    ci, si = jnp.cos(ang_i), jnp.sin(ang_i)
    rot = jnp.concatenate(
        [cq, cq, -sq, sq, ci, ci, -si, si, jnp.zeros((LANES - 96, tq), F32)], axis=0)
    c = rot.T
    lane = lax.broadcasted_iota(jnp.int32, (tq, LANES), 1)
    in_q = lane < ROPE_DIM
    cos_q = jnp.where(in_q, c, 1.0)
    sin_q = jnp.where(in_q, pltpu.roll(c, LANES - 32, 1), 0.0)
    first_i = lane < IDX_ROPE_DIM
    second_i = (lane >= IDX_DIM) & (lane < IDX_DIM + IDX_ROPE_DIM)
    cos_i = jnp.where(first_i, pltpu.roll(c, LANES - 64, 1),
                      jnp.where(second_i, c, 1.0))
    sin_i = jnp.where(first_i, pltpu.roll(c, LANES - 80, 1),
                      jnp.where(second_i, pltpu.roll(c, LANES - 16, 1), 0.0))
    half_q = lane < ROPE_DIM // 2
    half_i = (lane % IDX_DIM) < IDX_ROPE_DIM // 2

    u_ref[...] = proj[:, 0:OFF_Q]
    scale = HEAD_DIM ** -0.5
    for hh in range(N_HEADS):
        c0 = OFF_Q + hh * HEAD_DIM
        r = _rope(proj[:, c0:c0 + HEAD_DIM], cos_q, sin_q, half_q, ROPE_DIM // 2)
        q_ref[:, hh * HEAD_DIM:(hh + 1) * HEAD_DIM] = (r * scale).astype(BF16)
    for hh in range(N_KV_HEADS):
        c0 = OFF_K + hh * HEAD_DIM
        r = _rope(proj[:, c0:c0 + HEAD_DIM], cos_q, sin_q, half_q, ROPE_DIM // 2)
        k_ref[:, hh * HEAD_DIM:(hh + 1) * HEAD_DIM] = r.astype(BF16)
    ones = jnp.ones((tq, HEAD_DIM), BF16)
    for hh in range(N_KV_HEADS):
        c0 = OFF_V + hh * HEAD_DIM
        v_ref[:, 2 * hh * HEAD_DIM:(2 * hh + 1) * HEAD_DIM] = proj[:, c0:c0 + HEAD_DIM].astype(BF16)
        v_ref[:, (2 * hh + 1) * HEAD_DIM:(2 * hh + 2) * HEAD_DIM] = ones
    for t in range(IDX_HEADS * IDX_DIM // LANES):
        c0 = OFF_IQ + t * LANES
        r = _rope(proj[:, c0:c0 + LANES], cos_i, sin_i, half_i, IDX_ROPE_DIM // 2)
        iq_ref[:, t * LANES:(t + 1) * LANES] = r.astype(BF16)
    tail = proj[:, OFF_IK:OFF_IK + LANES]
    r = _rope(tail, cos_i, sin_i, half_i, IDX_ROPE_DIM // 2)
    ik_ref[...] = r[:, 0:IDX_DIM].astype(BF16)
    iw_ref[...] = tail[:, IDX_DIM:IDX_DIM + IDX_HEADS]


def _rope_freqs():
    f_q = ROPE_THETA ** (-jnp.arange(0, ROPE_DIM, 2, dtype=F32) / ROPE_DIM)
    f_i = ROPE_THETA ** (-jnp.arange(0, IDX_ROPE_DIM, 2, dtype=F32) / IDX_ROPE_DIM)
    return jnp.concatenate([f_q, f_i]).reshape(-1, 1)


def _inproj(x2, pos2, g, b, w_pad, tq):
    T = x2.shape[0]
    row = lambda i: (i, 0)
    fixed = lambda i: (0, 0)
    out_shapes = (
        jax.ShapeDtypeStruct((T, POOL_WIDTH), F32),
        jax.ShapeDtypeStruct((T, ATTN_WIDTH), BF16),
        jax.ShapeDtypeStruct((T, KV_WIDTH), BF16),
        jax.ShapeDtypeStruct((T, 2 * KV_WIDTH), BF16),
        jax.ShapeDtypeStruct((T, IDX_HEADS * IDX_DIM), BF16),
        jax.ShapeDtypeStruct((T, IDX_DIM), BF16),
        jax.ShapeDtypeStruct((T, IDX_HEADS), F32),
    )
    return pl.pallas_call(
        _inproj_kernel,
        grid=(T // tq,),
        in_specs=[
            pl.BlockSpec((tq, D_MODEL), row),
            pl.BlockSpec((1, tq), lambda i: (0, i)),
            pl.BlockSpec((1, D_MODEL), fixed),
            pl.BlockSpec((1, D_MODEL), fixed),
            pl.BlockSpec((D_MODEL, IN_WIDTH_PAD), fixed),
            pl.BlockSpec((ROPE_DIM // 2 + IDX_ROPE_DIM // 2, 1), fixed),
        ],
        out_specs=tuple(pl.BlockSpec((tq, s.shape[1]), row) for s in out_shapes),
        out_shape=out_shapes,
        compiler_params=pltpu.CompilerParams(
            dimension_semantics=("parallel",), vmem_limit_bytes=VMEM_LIMIT),
    )(x2, pos2, g, b, w_pad, _rope_freqs())


def _row_count(mask):
    return jnp.sum(jnp.where(mask, 1.0, 0.0), axis=-1, keepdims=True)


def _select_bias(iq_ref, ik_ref, iw_ref, key_ref, bias_ref, *, i, tq, S, n_sel):
    q_pos = i * tq + lax.broadcasted_iota(jnp.int32, (tq, S), 0)
    k_pos = lax.broadcasted_iota(jnp.int32, (tq, S), 1)
    causal = k_pos <= q_pos
    if S <= n_sel:
        bias_ref[:, 0:S] = jnp.where(causal, 0.0, NEG_INF)
        return

    w_scale = (IDX_HEADS ** -0.5) * (IDX_DIM ** -0.5)
    iw = iw_ref[...] * w_scale
    ik = ik_ref[0:S, :]
    nt = (((1,), (1,)), ((), ()))
    sc = jnp.zeros((tq, S), F32)
    for hh in range(IDX_HEADS):
        s = lax.dot_general(iq_ref[:, hh * IDX_DIM:(hh + 1) * IDX_DIM], ik, nt,
                            preferred_element_type=F32)
        sc = sc + jnp.maximum(s, 0.0) * iw[:, hh:hh + 1]
    sc = jnp.where(sc == 0.0, 0.0, sc)
    sc = jnp.where(causal, sc, NEG_INF)
    bits = pltpu.bitcast(sc, jnp.int32)
    key_ref[:, 0:S] = bits ^ ((bits >> 31) & jnp.int32(0x7FFFFFFF))

    def value_step(b, t_u):
        cand = t_u | lax.shift_left(jnp.int32(1), 31 - b)
        cnt = _row_count(key_ref[:, 0:S] >= (cand ^ jnp.int32(INT_MIN)))
        return jnp.where(cnt >= n_sel, cand, t_u)

    def value_steps(bb, t_u):
        for r in range(SEARCH_UNROLL):
            t_u = value_step(bb * SEARCH_UNROLL + r, t_u)
        return t_u

    t_u = lax.fori_loop(0, 32 // SEARCH_UNROLL, value_steps, jnp.zeros((tq, 1), jnp.int32))
    thr = t_u ^ jnp.int32(INT_MIN)
    key = key_ref[:, 0:S]
    need = n_sel - _row_count(key > thr)
    tied = key == thr
    bias_ref[:, 0:S] = jnp.where((key >= thr) & causal, 0.0, NEG_INF)
    excess = _row_count(tied & causal) > need
    any_excess = jnp.max(jnp.where(excess, 1.0, 0.0)) > 0.0

    @pl.when(any_excess)
    def _():
        r = lax.broadcasted_iota(jnp.int32, (TIE_BLOCK, TIE_BLOCK), 0)
        cc = lax.broadcasted_iota(jnp.int32, (TIE_BLOCK, TIE_BLOCK), 1)
        upto = jnp.where(r <= cc, 1.0, 0.0).astype(BF16)
        qp = i * tq + lax.broadcasted_iota(jnp.int32, (tq, TIE_BLOCK), 0)
        carry = jnp.zeros((tq, 1), F32)
        for c0 in range(0, S, TIE_BLOCK):
            kb = key_ref[:, c0:c0 + TIE_BLOCK]
            cz = (c0 + lax.broadcasted_iota(jnp.int32, (tq, TIE_BLOCK), 1)) <= qp
            tb = (kb == thr) & cz
            seen = jnp.dot(jnp.where(tb, 1.0, 0.0).astype(BF16), upto,
                           preferred_element_type=F32) + carry
            admit = ((kb > thr) & cz) | (tb & (seen <= need))
            bias_ref[:, c0:c0 + TIE_BLOCK] = jnp.where(admit, 0.0, NEG_INF)
            carry = seen[:, TIE_BLOCK - 1:TIE_BLOCK]


def _attn_tile(q_ref, k_ref, v_ref, iq_ref, ik_ref, iw_ref, o_ref, key_ref, bias_ref,
               *, i, tq, S, n_sel):
    _select_bias(iq_ref, ik_ref, iw_ref, key_ref, bias_ref, i=i, tq=tq, S=S, n_sel=n_sel)
    nt = (((1,), (1,)), ((), ()))
    for hh in range(N_HEADS):
        g = hh // (N_HEADS // N_KV_HEADS)
        kg = k_ref[0:S, g * HEAD_DIM:(g + 1) * HEAD_DIM]
        vg = v_ref[0:S, 2 * g * HEAD_DIM:2 * (g + 1) * HEAD_DIM]
        logits = lax.dot_general(q_ref[:, hh * HEAD_DIM:(hh + 1) * HEAD_DIM], kg, nt,
                                 preferred_element_type=F32) + bias_ref[:, 0:S]
        m = jnp.max(logits, axis=-1, keepdims=True)
        p = jnp.exp((logits - m).astype(BF16))
        ol = jnp.dot(p, vg, preferred_element_type=F32)
        o = ol[:, 0:HEAD_DIM] / ol[:, HEAD_DIM:HEAD_DIM + 1]
        o_ref[:, hh * HEAD_DIM:(hh + 1) * HEAD_DIM] = o.astype(BF16)


def _attn_kernel(*refs, tq, L, n_sel):
    i = pl.program_id(1)
    band = L // CAUSAL_BANDS
    for v in range(CAUSAL_BANDS):
        @pl.when((i * tq) // band == v)
        def _(v=v):
            _attn_tile(*refs, i=i, tq=tq, S=(v + 1) * band, n_sel=n_sel)


def _attn(q, k, v, iq, ik, iw, B, L, tq):
    T = B * L
    nq = L // tq
    n_sel = min(TOPK_MAX, L // 4)
    assert L % (CAUSAL_BANDS * tq) == 0
    qrow = lambda b, i: (b * nq + i, 0)
    seq = lambda b, i: (b, 0)
    return pl.pallas_call(
        functools.partial(_attn_kernel, tq=tq, L=L, n_sel=n_sel),
        grid=(B, nq),
        in_specs=[
            pl.BlockSpec((tq, ATTN_WIDTH), qrow),
            pl.BlockSpec((L, KV_WIDTH), seq),
            pl.BlockSpec((L, 2 * KV_WIDTH), seq),
            pl.BlockSpec((tq, IDX_HEADS * IDX_DIM), qrow),
            pl.BlockSpec((L, IDX_DIM), seq),
            pl.BlockSpec((tq, IDX_HEADS), qrow),
        ],
        out_specs=pl.BlockSpec((tq, ATTN_WIDTH), qrow),
        out_shape=jax.ShapeDtypeStruct((T, ATTN_WIDTH), BF16),
        scratch_shapes=[pltpu.VMEM((tq, L), jnp.int32), pltpu.VMEM((tq, L), F32)],
        compiler_params=pltpu.CompilerParams(
            dimension_semantics=("parallel", "arbitrary"), vmem_limit_bytes=VMEM_LIMIT),
    )(q, k, v, iq, ik, iw)


def _outproj_kernel(x_ref, u_ref, halo_ref, a_ref, gin_ref, bin_ref, wpool_ref, pscale_ref,
                    wout_ref, g1_ref, b1_ref, wr_hi_ref, wr_lo_ref, br_ref,
                    h1_ref, h1b_ref, gates_ref, slot_ref, slot_t_ref, meta_ref, carry_ref,
                    *, tq, tiles_per_seq):
    i = pl.program_id(0)
    seq_tile = i % tiles_per_seq
    u = u_ref[...]
    halo = jnp.where(seq_tile == 0, 0.0, halo_ref[...])
    ext = jnp.concatenate([halo, u], axis=0)
    lane = lax.broadcasted_iota(jnp.int32, (tq, POOL_WIDTH), 1)
    grp = lane // POOL_GDIM
    win = jnp.zeros((tq, POOL_WIDTH), F32)
    s = ext
    for gi, w in enumerate(POOL_WINDOWS):
        s = s + pltpu.roll(s, w // 2, 0)
        win = jnp.where(grp == gi, s[POOL_HALO:, :], win)
    t_seq = seq_tile * tq + lax.broadcasted_iota(jnp.int32, (tq, POOL_WIDTH), 0)
    width = lax.shift_left(jnp.int32(2), grp)
    cnt = jnp.minimum(t_seq + 1, width).astype(F32)
    d = win / cnt - u
    y_pool = jnp.dot(d.astype(BF16), wpool_ref[...], preferred_element_type=F32) * pscale_ref[...]

    mix = jnp.dot(y_pool.astype(BF16), wout_ref[0:POOL_WIDTH, :], preferred_element_type=F32)
    mix = mix + jnp.dot(a_ref[...], wout_ref[POOL_WIDTH:, :], preferred_element_type=F32)
    h = _layer_norm(x_ref[...], gin_ref[...], bin_ref[...])
    h1 = _layer_norm(ALPHA_RES * h + mix, g1_ref[...], b1_ref[...])
    h1_ref[...] = h1
    h1_hi = h1.astype(BF16)
    h1b_ref[...] = h1_hi
    h1_lo = (h1 - h1_hi.astype(F32)).astype(BF16)
    wr_hi = wr_hi_ref[...]
    logits = (jnp.dot(h1_hi, wr_hi, preferred_element_type=F32)
              + jnp.dot(h1_lo, wr_hi, preferred_element_type=F32)
              + jnp.dot(h1_hi, wr_lo_ref[...], preferred_element_type=F32)) + br_ref[...]
    lane_e = lax.broadcasted_iota(jnp.int32, (tq, LANES), 1).astype(F32)
    lg = jnp.where(lane_e < N_EXPERTS, logits, -jnp.inf)
    ids = jnp.zeros((tq, LANES), F32)
    ex = jnp.zeros((tq, LANES), F32)
    top0 = None
    for kk in range(TOP_K):
        m = jnp.max(lg, axis=-1, keepdims=True)
        idx = jnp.min(jnp.where(lg == m, lane_e, float(LANES)), axis=-1, keepdims=True)
        if top0 is None:
            top0 = m
        ids = jnp.where(lane_e == kk, idx, ids)
        ex = jnp.where(lane_e == kk, jnp.exp(m - top0), ex)
        lg = jnp.where(lane_e == idx, -jnp.inf, lg)
    gates_ref[...] = ex / jnp.sum(ex, axis=-1, keepdims=True)
    _route_block(ids.astype(jnp.int32), slot_ref, slot_t_ref, meta_ref, carry_ref)


def _outproj(x2, u, attn, gin, bin_, wpool_bd, pscale, wout, g1, b1, wr_hi, wr_lo, br, L, tq):
    T = x2.shape[0]
    assert tq == ROUTE_TB
    row = lambda i: (i, 0)
    fixed = lambda i: (0, 0)
    halo = lambda i: (jnp.maximum(i * (tq // POOL_HALO) - 1, 0), 0)
    out_shapes = (
        jax.ShapeDtypeStruct((T, D_MODEL), F32),
        jax.ShapeDtypeStruct((T, D_MODEL), BF16),
        jax.ShapeDtypeStruct((T, LANES), F32),
        jax.ShapeDtypeStruct((T, LANES), jnp.int32),
        jax.ShapeDtypeStruct((8, T), jnp.int32),
        jax.ShapeDtypeStruct((T // tq * 8, LANES), jnp.int32),
    )
    out_specs = (
        pl.BlockSpec((tq, D_MODEL), row), pl.BlockSpec((tq, D_MODEL), row),
        pl.BlockSpec((tq, LANES), row), pl.BlockSpec((tq, LANES), row),
        pl.BlockSpec((8, tq), lambda i: (0, i)), pl.BlockSpec((8, LANES), row),
    )
    return pl.pallas_call(
        functools.partial(_outproj_kernel, tq=tq, tiles_per_seq=L // tq),
        grid=(T // tq,),
        in_specs=[
            pl.BlockSpec((tq, D_MODEL), row),
            pl.BlockSpec((tq, POOL_WIDTH), row),
            pl.BlockSpec((POOL_HALO, POOL_WIDTH), halo),
            pl.BlockSpec((tq, ATTN_WIDTH), row),
            pl.BlockSpec((1, D_MODEL), fixed),
            pl.BlockSpec((1, D_MODEL), fixed),
            pl.BlockSpec((POOL_WIDTH, POOL_WIDTH), fixed),
            pl.BlockSpec((1, POOL_WIDTH), fixed),
            pl.BlockSpec((D_MODEL, D_MODEL), fixed),
            pl.BlockSpec((1, D_MODEL), fixed),
            pl.BlockSpec((1, D_MODEL), fixed),
            pl.BlockSpec((D_MODEL, LANES), fixed),
            pl.BlockSpec((D_MODEL, LANES), fixed),
            pl.BlockSpec((1, LANES), fixed),
        ],
        out_specs=out_specs,
        out_shape=out_shapes,
        scratch_shapes=[pltpu.VMEM((1, LANES), F32)],
        compiler_params=pltpu.CompilerParams(
            dimension_semantics=("arbitrary",), vmem_limit_bytes=VMEM_LIMIT),
    )(x2, u, u, attn, gin, bin_, wpool_bd, pscale, wout, g1, b1, wr_hi, wr_lo, br)


ROUTE_TB = 512
ROW_ALIGN = 16
RUN_BITS = 6
STAGE_ROWS = 2560
STAGE_CHUNK = 512
COMBINE_CHUNK = 1280
FFN_TM = 512
TAIL_BITS = 5


def _route_block(ids, slot_ref, slot_t_ref, meta_ref, carry_ref):
    c = pl.program_id(0)

    @pl.when(c == 0)
    def _():
        carry_ref[...] = jnp.zeros(carry_ref.shape, F32)

    tb = ids.shape[0]
    lane = lax.broadcasted_iota(jnp.int32, (tb, LANES), 1)
    onehot = [jnp.where(lane == ids[:, k:k + 1], 1.0, 0.0) for k in range(TOP_K)]
    member = onehot[0] + onehot[1] + onehot[2] + onehot[3]
    r = lax.broadcasted_iota(jnp.int32, (tb, tb), 0)
    cc = lax.broadcasted_iota(jnp.int32, (tb, tb), 1)
    before = jnp.where(cc < r, 1.0, 0.0).astype(BF16)
    lrank = jnp.dot(before, member.astype(BF16), preferred_element_type=F32)
    n = jnp.sum(member, axis=0, keepdims=True)
    units = jnp.ceil(n * (1.0 / ROW_ALIGN))
    er = lax.broadcasted_iota(jnp.int32, (LANES, LANES), 0)
    ec = lax.broadcasted_iota(jnp.int32, (LANES, LANES), 1)
    lower = jnp.where(er < ec, 1.0, 0.0).astype(BF16)
    off_units = jnp.dot(jnp.broadcast_to(units, (8, LANES)).astype(BF16), lower,
                        preferred_element_type=F32)[0:1, :]
    base = off_units * ROW_ALIGN + lrank
    slot = jnp.zeros((tb, LANES), jnp.int32)
    for k in range(TOP_K):
        sk = jnp.sum(onehot[k] * base, axis=-1, keepdims=True).astype(jnp.int32)
        slot = jnp.where(lane == k, sk, slot)
    slot_ref[...] = slot
    slot_t_ref[...] = slot.T[0:8, :]
    row = lax.broadcasted_iota(jnp.int32, (8, LANES), 0)
    meta = jnp.where(row == 0, units, jnp.where(row == 1, off_units, carry_ref[...]))
    meta_ref[...] = meta.astype(jnp.int32)
    carry_ref[...] = carry_ref[...] + units


def _pieces_of(m, so, do, bits, visit):
    for b in bits:
        done = m & ((1 << b) - 1)

        @pl.when(((m >> b) & 1) == 1)
        def _(b=b, done=done):
            visit(b, pl.multiple_of((so + done) * ROW_ALIGN, ROW_ALIGN),
                  pl.multiple_of((do + done) * ROW_ALIGN, ROW_ALIGN), ROW_ALIGN << b)


def _piece_lists(units, off_units, dst_units):
    bits = jnp.arange(RUN_BITS, dtype=jnp.int32)[None, :, None]
    m = units[:, None, :]
    has = ((m >> bits) & 1) == 1
    done = m & ((1 << bits) - 1)
    pos = jnp.cumsum(has, axis=-1) - 1
    at = has[..., None] & (pos[..., None] == jnp.arange(N_EXPERTS, dtype=jnp.int32))
    pick = lambda v: jnp.sum(jnp.where(at, v[..., None], 0), axis=2).reshape(-1).astype(jnp.int32)
    cnt = jnp.sum(has, axis=-1).reshape(-1).astype(jnp.int32)
    return cnt, pick(off_units[:, None, :] + done), pick(dst_units[:, None, :] + done)


def _run_pieces(cnt_ref, srow_ref, grow_ref, c, visit):
    for b in range(RUN_BITS):
        base = (c * RUN_BITS + b) * N_EXPERTS

        def body(j, carry, b=b, base=base):
            visit(b, j, pl.multiple_of(srow_ref[base + j] * ROW_ALIGN, ROW_ALIGN),
                  pl.multiple_of(grow_ref[base + j] * ROW_ALIGN, ROW_ALIGN), ROW_ALIGN << b)
            return carry

        lax.fori_loop(0, cnt_ref[c * RUN_BITS + b], body, 0)


def _dispatch_kernel(cnt_ref, srow_ref, grow_ref, tail_ref, tail_dst_ref, slot_t_ref, h_ref,
                     xs_ref, stage_ref, zero_ref, sems, *, nblk):
    c = pl.program_id(0)
    par = c % 2
    tb = h_ref.shape[0]
    for r0 in range(0, STAGE_ROWS, STAGE_CHUNK):
        srow = r0 + lax.broadcasted_iota(jnp.int32, (STAGE_CHUNK, tb), 0)
        hit = srow == slot_t_ref[0:1, :]
        for k in range(1, TOP_K):
            hit = hit | (srow == slot_t_ref[k:k + 1, :])
        perm = jnp.where(hit, 1.0, 0.0).astype(BF16)
        stage_ref[par, r0:r0 + STAGE_CHUNK, :] = jnp.dot(
            perm, h_ref[...], preferred_element_type=F32).astype(BF16)

    def piece(p):
        return lambda b, j, s_row, g_row, rows: pltpu.make_async_copy(
            stage_ref.at[p, pl.ds(s_row, rows), :], xs_ref.at[pl.ds(g_row, rows), :],
            sems.at[p, j, b])

    _run_pieces(cnt_ref, srow_ref, grow_ref, c, lambda *a: piece(par)(*a).start())

    @pl.when(c > 0)
    def _():
        _run_pieces(cnt_ref, srow_ref, grow_ref, c - 1, lambda *a: piece(1 - par)(*a).wait())

    @pl.when(c == nblk - 1)
    def _():
        _run_pieces(cnt_ref, srow_ref, grow_ref, c, lambda *a: piece(par)(*a).wait())
        zero_ref[...] = jnp.zeros(zero_ref.shape, BF16)

        def tail(e, b, z_row, g_row, rows):
            del z_row
            return pltpu.make_async_copy(zero_ref.at[pl.ds(0, rows), :],
                                         xs_ref.at[pl.ds(g_row, rows), :], sems.at[0, e, b])

        def tails(act):
            for e in range(N_EXPERTS):
                _pieces_of(tail_ref[e], 0, tail_dst_ref[e], range(TAIL_BITS),
                           lambda b, z, g, rows, e=e: act(tail(e, b, z, g, rows)))

        tails(lambda cp: cp.start())
        tails(lambda cp: cp.wait())

        zrows = zero_ref.shape[0]
        used = (tail_dst_ref[N_EXPERTS - 1] + tail_ref[N_EXPERTS - 1]) * ROW_ALIGN
        n_rest = (xs_ref.shape[0] - used) // zrows

        def rest(i):
            row0 = pl.multiple_of(used + i * zrows, ROW_ALIGN)
            return pltpu.make_async_copy(
                zero_ref, xs_ref.at[pl.ds(row0, zrows), :],
                sems.at[i // (N_EXPERTS * RUN_BITS), (i // RUN_BITS) % N_EXPERTS, i % RUN_BITS])

        lax.fori_loop(0, n_rest, lambda i, _: (rest(i).start(), 0)[1], 0)
        lax.fori_loop(0, n_rest, lambda i, _: (rest(i).wait(), 0)[1], 0)


def _dispatch(units, off, dst, tail, tail_dst, slot_t, h1b, n_rows):
    T = h1b.shape[0]
    nblk = T // ROUTE_TB
    grid_spec = pltpu.PrefetchScalarGridSpec(
        num_scalar_prefetch=5,
        grid=(nblk,),
        in_specs=[pl.BlockSpec((8, ROUTE_TB), lambda c, *_: (0, c)),
                  pl.BlockSpec((ROUTE_TB, D_MODEL), lambda c, *_: (c, 0))],
        out_specs=pl.BlockSpec(memory_space=pl.ANY),
        scratch_shapes=[pltpu.VMEM((2, STAGE_ROWS, D_MODEL), BF16),
                        pltpu.VMEM((ROW_ALIGN << (TAIL_BITS - 1), D_MODEL), BF16),
                        pltpu.SemaphoreType.DMA((2, N_EXPERTS, RUN_BITS))],
    )
    return pl.pallas_call(
        functools.partial(_dispatch_kernel, nblk=nblk),
        grid_spec=grid_spec,
        out_shape=jax.ShapeDtypeStruct((n_rows, D_MODEL), BF16),
        compiler_params=pltpu.CompilerParams(
            dimension_semantics=("arbitrary",), vmem_limit_bytes=VMEM_LIMIT),
    )(units, off, dst, tail, tail_dst, slot_t, h1b)


def _ffn_kernel(tile_e_ref, n_tiles_ref, xs_ref, wgu_ref, bgu_ref, wd_ref, bd_ref, y_ref,
                wgu_b, wd_b, *, n_chunk):
    j = pl.program_id(0)
    active = j < n_tiles_ref[0]
    new_expert = (j == 0) | (tile_e_ref[j] != tile_e_ref[jnp.maximum(j - 1, 0)])

    def tile(refresh):
        def weights(dst, src, rows, cols):
            if refresh:
                dst[rows, cols] = src[0, rows, cols].astype(BF16)
            return dst[rows, cols]

        xs = xs_ref[...]
        cw = D_FF // n_chunk
        full = slice(None)
        acc = jnp.zeros(y_ref.shape, F32)
        for c in range(n_chunk):
            cols = slice(c * cw, (c + 1) * cw)
            ucols = slice(D_FF + c * cw, D_FF + (c + 1) * cw)
            gate = jnp.dot(xs, weights(wgu_b, wgu_ref, full, cols), preferred_element_type=F32)
            gate = gate + bgu_ref[0, :, cols]
            up = jnp.dot(xs, weights(wgu_b, wgu_ref, full, ucols), preferred_element_type=F32)
            up = up + bgu_ref[0, :, ucols]
            gate = jnp.minimum(gate, SWIGLU_LIMIT)
            up = jnp.clip(up, -SWIGLU_LIMIT, SWIGLU_LIMIT)
            act = (up + 1.0) * gate * jax.nn.sigmoid(SWIGLU_ALPHA * gate)
            acc = acc + jnp.dot(act.astype(BF16), weights(wd_b, wd_ref, cols, full),
                                preferred_element_type=F32)
        y_ref[...] = (acc + bd_ref[0]).astype(BF16)

    pl.when(active & new_expert)(lambda: tile(True))
    pl.when(active & jnp.logical_not(new_expert))(lambda: tile(False))

    @pl.when(jnp.logical_not(active))
    def _():
        y_ref[...] = jnp.zeros(y_ref.shape, BF16)


def _ffn(tile_e, n_tiles, xs, wgu, bgu, wd, bd):
    n_rows = xs.shape[0]
    last = lambda j, nt: jnp.minimum(j, nt[0] - 1)
    row = lambda j, te, nt: (last(j, nt), 0)
    exp3 = lambda j, te, nt: (te[last(j, nt)], 0, 0)
    grid_spec = pltpu.PrefetchScalarGridSpec(
        num_scalar_prefetch=2,
        grid=(n_rows // FFN_TM,),
        in_specs=[
            pl.BlockSpec((FFN_TM, D_MODEL), row),
            pl.BlockSpec((1, D_MODEL, 2 * D_FF), exp3),
            pl.BlockSpec((1, 1, 2 * D_FF), exp3),
            pl.BlockSpec((1, D_FF, D_MODEL), exp3),
            pl.BlockSpec((1, 1, D_MODEL), exp3),
        ],
        out_specs=pl.BlockSpec((FFN_TM, D_MODEL), lambda j, te, nt: (j, 0)),
        scratch_shapes=[pltpu.VMEM((D_MODEL, 2 * D_FF), BF16), pltpu.VMEM((D_FF, D_MODEL), BF16)],
    )
    return pl.pallas_call(
        functools.partial(_ffn_kernel, n_chunk=2),
        grid_spec=grid_spec,
        out_shape=jax.ShapeDtypeStruct((n_rows, D_MODEL), BF16),
        compiler_params=pltpu.CompilerParams(
            dimension_semantics=("arbitrary",), vmem_limit_bytes=FFN_VMEM_LIMIT),
    )(tile_e, n_tiles, xs, wgu, bgu, wd, bd)


def _combine_kernel(cnt_ref, srow_ref, grow_ref, slot_ref, gates_ref, h1_ref, ys_ref,
                    g_ref, b_ref, o_ref, stage_ref, sems, *, nblk):
    c = pl.program_id(0)
    par = c % 2
    tb = h1_ref.shape[0]

    def piece(p):
        return lambda b, j, s_row, g_row, rows: pltpu.make_async_copy(
            ys_ref.at[pl.ds(g_row, rows), :], stage_ref.at[p, pl.ds(s_row, rows), :],
            sems.at[p, j, b])

    @pl.when(c == 0)
    def _():
        stage_ref[...] = jnp.zeros(stage_ref.shape, BF16)
        _run_pieces(cnt_ref, srow_ref, grow_ref, c, lambda *a: piece(0)(*a).start())

    _run_pieces(cnt_ref, srow_ref, grow_ref, c, lambda *a: piece(par)(*a).wait())

    @pl.when(c + 1 < nblk)
    def _():
        _run_pieces(cnt_ref, srow_ref, grow_ref, c + 1, lambda *a: piece(1 - par)(*a).start())

    slot = slot_ref[...]
    gates = gates_ref[...]
    ffn = jnp.zeros((tb, D_MODEL), F32)
    for r0 in range(0, STAGE_ROWS, COMBINE_CHUNK):
        scol = r0 + lax.broadcasted_iota(jnp.int32, (tb, COMBINE_CHUNK), 1)
        w = jnp.zeros((tb, COMBINE_CHUNK), F32)
        for k in range(TOP_K):
            w = w + jnp.where(scol == slot[:, k:k + 1], gates[:, k:k + 1], 0.0)
        ffn = ffn + jnp.dot(w.astype(BF16), stage_ref[par, r0:r0 + COMBINE_CHUNK, :],
                            preferred_element_type=F32)
    o_ref[...] = _layer_norm(ALPHA_RES * h1_ref[...] + ffn, g_ref[...], b_ref[...])


def _combine(units, off, dst, slot, gates, h1, ys, g, b):
    T = h1.shape[0]
    nblk = T // ROUTE_TB
    blk = lambda c, *_: (c, 0)
    fixed = lambda c, *_: (0, 0)
    grid_spec = pltpu.PrefetchScalarGridSpec(
        num_scalar_prefetch=3,
        grid=(nblk,),
        in_specs=[pl.BlockSpec((ROUTE_TB, LANES), blk),
                  pl.BlockSpec((ROUTE_TB, LANES), blk),
                  pl.BlockSpec((ROUTE_TB, D_MODEL), blk),
                  pl.BlockSpec(memory_space=pl.ANY),
                  pl.BlockSpec((1, D_MODEL), fixed),
                  pl.BlockSpec((1, D_MODEL), fixed)],
        out_specs=pl.BlockSpec((ROUTE_TB, D_MODEL), blk),
        scratch_shapes=[pltpu.VMEM((2, STAGE_ROWS, D_MODEL), BF16),
                        pltpu.SemaphoreType.DMA((2, N_EXPERTS, RUN_BITS))],
    )
    return pl.pallas_call(
        functools.partial(_combine_kernel, nblk=nblk),
        grid_spec=grid_spec,
        out_shape=jax.ShapeDtypeStruct((T, D_MODEL), F32),
        compiler_params=pltpu.CompilerParams(
            dimension_semantics=("arbitrary",), vmem_limit_bytes=VMEM_LIMIT),
    )(units, off, dst, slot, gates, h1, ys, g, b)


def kernel(x, positions, ln_in_g, ln_in_b, w_in, w_pool, pool_scale, w_out, ln1_g, ln1_b,
           w_router, b_router, w_gate_up, b_gate_up, w_down, b_down, ln2_g, ln2_b):
    B, L, D = x.shape
    T = B * L
    assert T % ROUTE_TB == 0 and D == D_MODEL
    tq = min(512, L)
    x2 = x.reshape(T, D)
    pos2 = positions.reshape(1, T)
    gin = ln_in_g.reshape(1, D)
    bin_ = ln_in_b.reshape(1, D)

    w_pad = jnp.pad(w_in[0], ((0, 0), (0, IN_WIDTH_PAD - IN_WIDTH))).astype(BF16)
    u, q, k, v, iq, ik, iw = _inproj(x2, pos2, gin, bin_, w_pad, tq)

    attn = _attn(q, k, v, iq, ik, iw, B, L, min(256, L // CAUSAL_BANDS))

    wpool_bd = jnp.zeros((POOL_WIDTH, POOL_WIDTH), F32)
    for gi in range(POOL_GROUPS):
        sl = slice(gi * POOL_GDIM, (gi + 1) * POOL_GDIM)
        wpool_bd = wpool_bd.at[sl, sl].set(w_pool[0, gi])
    wr = jnp.pad(w_router[0], ((0, 0), (0, LANES - N_EXPERTS)))
    wr_hi = wr.astype(BF16)
    wr_lo = (wr - wr_hi.astype(F32)).astype(BF16)
    br = jnp.pad(b_router[0], (0, LANES - N_EXPERTS)).reshape(1, LANES)
    h1, h1b, gates, slot, slot_t, meta = _outproj(
        x2, u, attn, gin, bin_, wpool_bd.astype(BF16), pool_scale[0].reshape(1, POOL_WIDTH),
        w_out[0].astype(BF16), ln1_g[0].reshape(1, D), ln1_b[0].reshape(1, D),
        wr_hi, wr_lo, br, L, tq)

    nblk = T // ROUTE_TB
    meta = meta.reshape(nblk, 8, LANES)[:, :, :N_EXPERTS]
    units, off_units, base_units = meta[:, 0], meta[:, 1], meta[:, 2]
    tile_units = FFN_TM // ROW_ALIGN
    total_units = base_units[-1] + units[-1]
    region_units = ((total_units + tile_units - 1) // tile_units) * tile_units
    region_end = jnp.cumsum(region_units)
    dst_units = (region_end - region_units)[None, :] + base_units
    max_rows = T * TOP_K + nblk * N_EXPERTS * (ROW_ALIGN - 1) + N_EXPERTS * (FFN_TM - 1)
    n_rows = -(-max_rows // FFN_TM) * FFN_TM
    tile_start = jnp.arange(n_rows // FFN_TM, dtype=jnp.int32) * tile_units
    tile_e = jnp.minimum(jnp.sum(tile_start[:, None] >= region_end[None, :], axis=1),
                         N_EXPERTS - 1).astype(jnp.int32)
    n_tiles = (region_end[-1:] // tile_units).astype(jnp.int32)
    cnt, srow, grow = _piece_lists(units, off_units, dst_units)
    tail_units = (region_units - total_units).astype(jnp.int32)
    tail_dst = (region_end - tail_units).astype(jnp.int32)

    xs = _dispatch(cnt, srow, grow, tail_units, tail_dst, slot_t, h1b, n_rows)
    ys = _ffn(tile_e, n_tiles, xs, w_gate_up[0], b_gate_up[0].reshape(N_EXPERTS, 1, 2 * D_FF),
              w_down[0], b_down[0].reshape(N_EXPERTS, 1, D))
    out = _combine(cnt, srow, grow, slot, gates, h1, ys,
                   ln2_g[0].reshape(1, D), ln2_b[0].reshape(1, D))
    return out.reshape(B, L, D)
```

```python
import functools

import jax
import jax.numpy as jnp
from jax import lax
from jax.experimental import pallas as pl
from jax.experimental.pallas import tpu as pltpu

F32 = jnp.float32
BF16 = jnp.bfloat16

D_MODEL = 1024
POOL_WIDTH = 256
POOL_GROUPS = 4
POOL_GDIM = 64
POOL_WINDOWS = (2, 4, 8, 16)
POOL_HALO = 16
HEAD_DIM = 128
ATTN_WIDTH = 768
N_HEADS = 6
N_KV_HEADS = 2
KV_WIDTH = 256
IDX_HEADS = 8
IDX_DIM = 64
TOPK_MAX = 256
ROPE_THETA = 500000.0
ROPE_DIM = 32
IDX_ROPE_DIM = 16
N_EXPERTS = 32
TOP_K = 4
D_FF = 1024
SWIGLU_LIMIT = 7.0
SWIGLU_ALPHA = 1.702
DEPTH = 1
ALPHA_RES = (2.0 * DEPTH) ** 0.25
LN_EPS = 1e-5
NEG_INF = -1e30
OFF_Q = 256
OFF_K = 1024
OFF_V = 1280
OFF_IQ = 1536
OFF_IK = 2048
IN_WIDTH = 2120
IN_WIDTH_PAD = 2176

LANES = 128
VMEM_LIMIT = 48 * 1024 * 1024
FFN_VMEM_LIMIT = 56 * 1024 * 1024
INT_MIN = -2 ** 31
TIE_BLOCK = 256
CAUSAL_BANDS = 4
SEARCH_UNROLL = 8


def _layer_norm(x, g, b):
    mu = jnp.mean(x, axis=-1, keepdims=True)
    xc = x - mu
    var = jnp.mean(xc * xc, axis=-1, keepdims=True)
    return xc * lax.rsqrt(var + LN_EPS) * g + b


def _rope(xh, cos, sin, first_half, half):
    partner = jnp.where(first_half, pltpu.roll(xh, LANES - half, 1), pltpu.roll(xh, half, 1))
    return xh * cos + partner * sin


def _inproj_kernel(x_ref, pos_ref, g_ref, b_ref, w_ref, freq_ref,
                   u_ref, q_ref, k_ref, v_ref, iq_ref, ik_ref, iw_ref):
    tq = x_ref.shape[0]
    h = _layer_norm(x_ref[...], g_ref[...], b_ref[...])
    proj = jnp.dot(h.astype(BF16), w_ref[...], preferred_element_type=F32)

    pos = pos_ref[...].astype(F32)
    ang_q = freq_ref[0:ROPE_DIM // 2, :] * pos
    ang_i = freq_ref[ROPE_DIM // 2:ROPE_DIM // 2 + IDX_ROPE_DIM // 2, :] * pos
    cq, sq = jnp.cos(ang_q), jnp.sin(ang_q)
    ci, si = jnp.cos(ang_i), jnp.sin(ang_i)
    rot = jnp.concatenate(
        [cq, cq, -sq, sq, ci, ci, -si, si, jnp.zeros((LANES - 96, tq), F32)], axis=0)
    c = rot.T
    lane = lax.broadcasted_iota(jnp.int32, (tq, LANES), 1)
    in_q = lane < ROPE_DIM
    cos_q = jnp.where(in_q, c, 1.0)
    sin_q = jnp.where(in_q, pltpu.roll(c, LANES - 32, 1), 0.0)
    first_i = lane < IDX_ROPE_DIM
    second_i = (lane >= IDX_DIM) & (lane < IDX_DIM + IDX_ROPE_DIM)
    cos_i = jnp.where(first_i, pltpu.roll(c, LANES - 64, 1),
                      jnp.where(second_i, c, 1.0))
    sin_i = jnp.where(first_i, pltpu.roll(c, LANES - 80, 1),
                      jnp.where(second_i, pltpu.roll(c, LANES - 16, 1), 0.0))
    half_q = lane < ROPE_DIM // 2
    half_i = (lane % IDX_DIM) < IDX_ROPE_DIM // 2

    u_ref[...] = proj[:, 0:OFF_Q]
    scale = HEAD_DIM ** -0.5
    for hh in range(N_HEADS):
        c0 = OFF_Q + hh * HEAD_DIM
        r = _rope(proj[:, c0:c0 + HEAD_DIM], cos_q, sin_q, half_q, ROPE_DIM // 2)
        q_ref[:, hh * HEAD_DIM:(hh + 1) * HEAD_DIM] = (r * scale).astype(BF16)
    for hh in range(N_KV_HEADS):
        c0 = OFF_K + hh * HEAD_DIM
        r = _rope(proj[:, c0:c0 + HEAD_DIM], cos_q, sin_q, half_q, ROPE_DIM // 2)
        k_ref[:, hh * HEAD_DIM:(hh + 1) * HEAD_DIM] = r.astype(BF16)
    ones = jnp.ones((tq, HEAD_DIM), BF16)
    for hh in range(N_KV_HEADS):
        c0 = OFF_V + hh * HEAD_DIM
        v_ref[:, 2 * hh * HEAD_DIM:(2 * hh + 1) * HEAD_DIM] = proj[:, c0:c0 + HEAD_DIM].astype(BF16)
        v_ref[:, (2 * hh + 1) * HEAD_DIM:(2 * hh + 2) * HEAD_DIM] = ones
    for t in range(IDX_HEADS * IDX_DIM // LANES):
        c0 = OFF_IQ + t * LANES
        r = _rope(proj[:, c0:c0 + LANES], cos_i, sin_i, half_i, IDX_ROPE_DIM // 2)
        iq_ref[:, t * LANES:(t + 1) * LANES] = r.astype(BF16)
    tail = proj[:, OFF_IK:OFF_IK + LANES]
    r = _rope(tail, cos_i, sin_i, half_i, IDX_ROPE_DIM // 2)
    ik_ref[...] = r[:, 0:IDX_DIM].astype(BF16)
    iw_ref[...] = tail[:, IDX_DIM:IDX_DIM + IDX_HEADS]


def _rope_freqs():
    f_q = ROPE_THETA ** (-jnp.arange(0, ROPE_DIM, 2, dtype=F32) / ROPE_DIM)
    f_i = ROPE_THETA ** (-jnp.arange(0, IDX_ROPE_DIM, 2, dtype=F32) / IDX_ROPE_DIM)
    return jnp.concatenate([f_q, f_i]).reshape(-1, 1)


def _inproj(x2, pos2, g, b, w_pad, tq):
    T = x2.shape[0]
    row = lambda i: (i, 0)
    fixed = lambda i: (0, 0)
    out_shapes = (
        jax.ShapeDtypeStruct((T, POOL_WIDTH), F32),
        jax.ShapeDtypeStruct((T, ATTN_WIDTH), BF16),
        jax.ShapeDtypeStruct((T, KV_WIDTH), BF16),
        jax.ShapeDtypeStruct((T, 2 * KV_WIDTH), BF16),
        jax.ShapeDtypeStruct((T, IDX_HEADS * IDX_DIM), BF16),
        jax.ShapeDtypeStruct((T, IDX_DIM), BF16),
        jax.ShapeDtypeStruct((T, IDX_HEADS), F32),
    )
    return pl.pallas_call(
        _inproj_kernel,
        grid=(T // tq,),
        in_specs=[
            pl.BlockSpec((tq, D_MODEL), row),
            pl.BlockSpec((1, tq), lambda i: (0, i)),
            pl.BlockSpec((1, D_MODEL), fixed),
            pl.BlockSpec((1, D_MODEL), fixed),
            pl.BlockSpec((D_MODEL, IN_WIDTH_PAD), fixed),
            pl.BlockSpec((ROPE_DIM // 2 + IDX_ROPE_DIM // 2, 1), fixed),
        ],
        out_specs=tuple(pl.BlockSpec((tq, s.shape[1]), row) for s in out_shapes),
        out_shape=out_shapes,
        compiler_params=pltpu.CompilerParams(
            dimension_semantics=("parallel",), vmem_limit_bytes=VMEM_LIMIT),
    )(x2, pos2, g, b, w_pad, _rope_freqs())


def _row_count(mask):
    return jnp.sum(jnp.where(mask, 1.0, 0.0), axis=-1, keepdims=True)


def _select_bias(iq_ref, ik_ref, iw_ref, key_ref, bias_ref, *, i, tq, S, n_sel):
    q_pos = i * tq + lax.broadcasted_iota(jnp.int32, (tq, S), 0)
    k_pos = lax.broadcasted_iota(jnp.int32, (tq, S), 1)
    causal = k_pos <= q_pos
    if S <= n_sel:
        bias_ref[:, 0:S] = jnp.where(causal, 0.0, NEG_INF)
        return

    w_scale = (IDX_HEADS ** -0.5) * (IDX_DIM ** -0.5)
    iw = iw_ref[...] * w_scale
    ik = ik_ref[0:S, :]
    nt = (((1,), (1,)), ((), ()))
    sc = jnp.zeros((tq, S), F32)
    for hh in range(IDX_HEADS):
        s = lax.dot_general(iq_ref[:, hh * IDX_DIM:(hh + 1) * IDX_DIM], ik, nt,
                            preferred_element_type=F32)
        sc = sc + jnp.maximum(s, 0.0) * iw[:, hh:hh + 1]
    sc = jnp.where(sc == 0.0, 0.0, sc)
    sc = jnp.where(causal, sc, NEG_INF)
    bits = pltpu.bitcast(sc, jnp.int32)
    key_ref[:, 0:S] = bits ^ ((bits >> 31) & jnp.int32(0x7FFFFFFF))

    def value_step(b, t_u):
        cand = t_u | lax.shift_left(jnp.int32(1), 31 - b)
        cnt = _row_count(key_ref[:, 0:S] >= (cand ^ jnp.int32(INT_MIN)))
        return jnp.where(cnt >= n_sel, cand, t_u)

    def value_steps(bb, t_u):
        for r in range(SEARCH_UNROLL):
            t_u = value_step(bb * SEARCH_UNROLL + r, t_u)
        return t_u

    t_u = lax.fori_loop(0, 32 // SEARCH_UNROLL, value_steps, jnp.zeros((tq, 1), jnp.int32))
    thr = t_u ^ jnp.int32(INT_MIN)
    key = key_ref[:, 0:S]
    need = n_sel - _row_count(key > thr)
    bias_ref[:, 0:S] = jnp.where(causal, jnp.where(key >= thr, 0.0, NEG_INF), NEG_INF)
    causal_ties = jnp.sum(jnp.where(causal, jnp.where(key == thr, 1.0, 0.0), 0.0),
                          axis=-1, keepdims=True)
    any_excess = jnp.max(jnp.where(causal_ties > need, 1.0, 0.0)) > 0.0

    @pl.when(any_excess)
    def _():
        r = lax.broadcasted_iota(jnp.int32, (TIE_BLOCK, TIE_BLOCK), 0)
        cc = lax.broadcasted_iota(jnp.int32, (TIE_BLOCK, TIE_BLOCK), 1)
        upto = jnp.where(r <= cc, 1.0, 0.0).astype(BF16)
        qp = i * tq + lax.broadcasted_iota(jnp.int32, (tq, TIE_BLOCK), 0)
        carry = jnp.zeros((tq, 1), F32)
        for c0 in range(0, S, TIE_BLOCK):
            kb = key_ref[:, c0:c0 + TIE_BLOCK]
            cz = (c0 + lax.broadcasted_iota(jnp.int32, (tq, TIE_BLOCK), 1)) <= qp
            tb = jnp.where(cz, jnp.where(kb == thr, 1.0, 0.0), 0.0)
            seen = jnp.dot(tb.astype(BF16), upto, preferred_element_type=F32) + carry
            tie_bias = jnp.where(kb == thr, jnp.where(seen <= need, 0.0, NEG_INF), NEG_INF)
            bias_ref[:, c0:c0 + TIE_BLOCK] = jnp.where(
                cz, jnp.where(kb > thr, 0.0, tie_bias), NEG_INF)
            carry = seen[:, TIE_BLOCK - 1:TIE_BLOCK]


def _attn_tile(q_ref, k_ref, v_ref, iq_ref, ik_ref, iw_ref, o_ref, key_ref, bias_ref,
               *, i, tq, S, n_sel):
    _select_bias(iq_ref, ik_ref, iw_ref, key_ref, bias_ref, i=i, tq=tq, S=S, n_sel=n_sel)
    nt = (((1,), (1,)), ((), ()))
    for hh in range(N_HEADS):
        g = hh // (N_HEADS // N_KV_HEADS)
        kg = k_ref[0:S, g * HEAD_DIM:(g + 1) * HEAD_DIM]
        vg = v_ref[0:S, 2 * g * HEAD_DIM:2 * (g + 1) * HEAD_DIM]
        logits = lax.dot_general(q_ref[:, hh * HEAD_DIM:(hh + 1) * HEAD_DIM], kg, nt,
                                 preferred_element_type=F32) + bias_ref[:, 0:S]
        m = jnp.max(logits, axis=-1, keepdims=True)
        p = jnp.exp((logits - m).astype(BF16))
        ol = jnp.dot(p, vg, preferred_element_type=F32)
        o = ol[:, 0:HEAD_DIM] / ol[:, HEAD_DIM:HEAD_DIM + 1]
        o_ref[:, hh * HEAD_DIM:(hh + 1) * HEAD_DIM] = o.astype(BF16)


def _attn_kernel(*refs, tq, L, n_sel):
    i = pl.program_id(1)
    band = L // CAUSAL_BANDS
    n_full = n_sel // tq
    if n_full:
        @pl.when(i < n_full)
        def _():
            _attn_tile(*refs, i=i, tq=tq, S=n_full * tq, n_sel=n_sel)
    for v in range(CAUSAL_BANDS):
        @pl.when((i >= n_full) & ((i * tq) // band == v))
        def _(v=v):
            _attn_tile(*refs, i=i, tq=tq, S=(v + 1) * band, n_sel=n_sel)


def _attn(q, k, v, iq, ik, iw, B, L, tq):
    T = B * L
    nq = L // tq
    n_sel = min(TOPK_MAX, L // 4)
    assert L % (CAUSAL_BANDS * tq) == 0
    qrow = lambda b, i: (b * nq + i, 0)
    seq = lambda b, i: (b, 0)
    return pl.pallas_call(
        functools.partial(_attn_kernel, tq=tq, L=L, n_sel=n_sel),
        grid=(B, nq),
        in_specs=[
            pl.BlockSpec((tq, ATTN_WIDTH), qrow),
            pl.BlockSpec((L, KV_WIDTH), seq),
            pl.BlockSpec((L, 2 * KV_WIDTH), seq),
            pl.BlockSpec((tq, IDX_HEADS * IDX_DIM), qrow),
            pl.BlockSpec((L, IDX_DIM), seq),
            pl.BlockSpec((tq, IDX_HEADS), qrow),
        ],
        out_specs=pl.BlockSpec((tq, ATTN_WIDTH), qrow),
        out_shape=jax.ShapeDtypeStruct((T, ATTN_WIDTH), BF16),
        scratch_shapes=[pltpu.VMEM((tq, L), jnp.int32), pltpu.VMEM((tq, L), F32)],
        compiler_params=pltpu.CompilerParams(
            dimension_semantics=("parallel", "arbitrary"), vmem_limit_bytes=VMEM_LIMIT),
    )(q, k, v, iq, ik, iw)


def _outproj_kernel(x_ref, u_ref, halo_ref, a_ref, gin_ref, bin_ref, wpool_ref, pscale_ref,
                    wout_ref, g1_ref, b1_ref, wr_hi_ref, wr_lo_ref, br_ref,
                    h1_ref, h1b_ref, gates_ref, slot_ref, slot_t_ref, meta_ref, carry_ref,
                    *, tq, tiles_per_seq):
    i = pl.program_id(0)
    seq_tile = i % tiles_per_seq
    u = u_ref[...]
    halo = jnp.where(seq_tile == 0, 0.0, halo_ref[...])
    ext = jnp.concatenate([halo, u], axis=0)
    lane = lax.broadcasted_iota(jnp.int32, (tq, POOL_WIDTH), 1)
    grp = lane // POOL_GDIM
    win = jnp.zeros((tq, POOL_WIDTH), F32)
    s = ext
    for gi, w in enumerate(POOL_WINDOWS):
        s = s + pltpu.roll(s, w // 2, 0)
        win = jnp.where(grp == gi, s[POOL_HALO:, :], win)
    t_seq = seq_tile * tq + lax.broadcasted_iota(jnp.int32, (tq, POOL_WIDTH), 0)
    width = lax.shift_left(jnp.int32(2), grp)
    cnt = jnp.minimum(t_seq + 1, width).astype(F32)
    d = win / cnt - u
    y_pool = jnp.dot(d.astype(BF16), wpool_ref[...], preferred_element_type=F32) * pscale_ref[...]

    mix = jnp.dot(y_pool.astype(BF16), wout_ref[0:POOL_WIDTH, :], preferred_element_type=F32)
    mix = mix + jnp.dot(a_ref[...], wout_ref[POOL_WIDTH:, :], preferred_element_type=F32)
    h = _layer_norm(x_ref[...], gin_ref[...], bin_ref[...])
    h1 = _layer_norm(ALPHA_RES * h + mix, g1_ref[...], b1_ref[...])
    h1_ref[...] = h1
    h1_hi = h1.astype(BF16)
    h1b_ref[...] = h1_hi
    h1_lo = (h1 - h1_hi.astype(F32)).astype(BF16)
    wr_hi = wr_hi_ref[...]
    logits = (jnp.dot(h1_hi, wr_hi, preferred_element_type=F32)
              + jnp.dot(h1_lo, wr_hi, preferred_element_type=F32)
              + jnp.dot(h1_hi, wr_lo_ref[...], preferred_element_type=F32)) + br_ref[...]
    lane_e = lax.broadcasted_iota(jnp.int32, (tq, LANES), 1).astype(F32)
    lg = jnp.where(lane_e < N_EXPERTS, logits, -jnp.inf)
    ids = jnp.zeros((tq, LANES), F32)
    ex = jnp.zeros((tq, LANES), F32)
    top0 = None
    for kk in range(TOP_K):
        m = jnp.max(lg, axis=-1, keepdims=True)
        idx = jnp.min(jnp.where(lg == m, lane_e, float(LANES)), axis=-1, keepdims=True)
        if top0 is None:
            top0 = m
        ids = jnp.where(lane_e == kk, idx, ids)
        ex = jnp.where(lane_e == kk, jnp.exp(m - top0), ex)
        lg = jnp.where(lane_e == idx, -jnp.inf, lg)
    gates_ref[...] = ex / jnp.sum(ex, axis=-1, keepdims=True)
    _route_block(ids.astype(jnp.int32), slot_ref, slot_t_ref, meta_ref, carry_ref)


def _outproj(x2, u, attn, gin, bin_, wpool_bd, pscale, wout, g1, b1, wr_hi, wr_lo, br, L, tq):
    T = x2.shape[0]
    assert tq == ROUTE_TB
    row = lambda i: (i, 0)
    fixed = lambda i: (0, 0)
    halo = lambda i: (jnp.maximum(i * (tq // POOL_HALO) - 1, 0), 0)
    out_shapes = (
        jax.ShapeDtypeStruct((T, D_MODEL), F32),
        jax.ShapeDtypeStruct((T, D_MODEL), BF16),
        jax.ShapeDtypeStruct((T, LANES), F32),
        jax.ShapeDtypeStruct((T, LANES), jnp.int32),
        jax.ShapeDtypeStruct((8, T), jnp.int32),
        jax.ShapeDtypeStruct((T // tq * 8, LANES), jnp.int32),
    )
    out_specs = (
        pl.BlockSpec((tq, D_MODEL), row), pl.BlockSpec((tq, D_MODEL), row),
        pl.BlockSpec((tq, LANES), row), pl.BlockSpec((tq, LANES), row),
        pl.BlockSpec((8, tq), lambda i: (0, i)), pl.BlockSpec((8, LANES), row),
    )
    return pl.pallas_call(
        functools.partial(_outproj_kernel, tq=tq, tiles_per_seq=L // tq),
        grid=(T // tq,),
        in_specs=[
            pl.BlockSpec((tq, D_MODEL), row),
            pl.BlockSpec((tq, POOL_WIDTH), row),
            pl.BlockSpec((POOL_HALO, POOL_WIDTH), halo),
            pl.BlockSpec((tq, ATTN_WIDTH), row),
            pl.BlockSpec((1, D_MODEL), fixed),
            pl.BlockSpec((1, D_MODEL), fixed),
            pl.BlockSpec((POOL_WIDTH, POOL_WIDTH), fixed),
            pl.BlockSpec((1, POOL_WIDTH), fixed),
            pl.BlockSpec((D_MODEL, D_MODEL), fixed),
            pl.BlockSpec((1, D_MODEL), fixed),
            pl.BlockSpec((1, D_MODEL), fixed),
            pl.BlockSpec((D_MODEL, LANES), fixed),
            pl.BlockSpec((D_MODEL, LANES), fixed),
            pl.BlockSpec((1, LANES), fixed),
        ],
        out_specs=out_specs,
        out_shape=out_shapes,
        scratch_shapes=[pltpu.VMEM((1, LANES), F32)],
        compiler_params=pltpu.CompilerParams(
            dimension_semantics=("arbitrary",), vmem_limit_bytes=VMEM_LIMIT),
    )(x2, u, u, attn, gin, bin_, wpool_bd, pscale, wout, g1, b1, wr_hi, wr_lo, br)


ROUTE_TB = 512
ROW_ALIGN = 16
RUN_BITS = 6
STAGE_ROWS = 2560
STAGE_CHUNK = 512
COMBINE_CHUNK = 1280
FFN_TM = 512
TAIL_BITS = 5


def _route_block(ids, slot_ref, slot_t_ref, meta_ref, carry_ref):
    c = pl.program_id(0)

    @pl.when(c == 0)
    def _():
        carry_ref[...] = jnp.zeros(carry_ref.shape, F32)

    tb = ids.shape[0]
    lane = lax.broadcasted_iota(jnp.int32, (tb, LANES), 1)
    onehot = [jnp.where(lane == ids[:, k:k + 1], 1.0, 0.0) for k in range(TOP_K)]
    member = onehot[0] + onehot[1] + onehot[2] + onehot[3]
    r = lax.broadcasted_iota(jnp.int32, (tb, tb), 0)
    cc = lax.broadcasted_iota(jnp.int32, (tb, tb), 1)
    before = jnp.where(cc < r, 1.0, 0.0).astype(BF16)
    lrank = jnp.dot(before, member.astype(BF16), preferred_element_type=F32)
    n = jnp.sum(member, axis=0, keepdims=True)
    units = jnp.ceil(n * (1.0 / ROW_ALIGN))
    er = lax.broadcasted_iota(jnp.int32, (LANES, LANES), 0)
    ec = lax.broadcasted_iota(jnp.int32, (LANES, LANES), 1)
    lower = jnp.where(er < ec, 1.0, 0.0).astype(BF16)
    off_units = jnp.dot(jnp.broadcast_to(units, (8, LANES)).astype(BF16), lower,
                        preferred_element_type=F32)[0:1, :]
    base = off_units * ROW_ALIGN + lrank
    slot = jnp.zeros((tb, LANES), jnp.int32)
    for k in range(TOP_K):
        sk = jnp.sum(onehot[k] * base, axis=-1, keepdims=True).astype(jnp.int32)
        slot = jnp.where(lane == k, sk, slot)
    slot_ref[...] = slot
    slot_t_ref[...] = slot.T[0:8, :]
    row = lax.broadcasted_iota(jnp.int32, (8, LANES), 0)
    meta = jnp.where(row == 0, units, jnp.where(row == 1, off_units, carry_ref[...]))
    meta_ref[...] = meta.astype(jnp.int32)
    carry_ref[...] = carry_ref[...] + units


def _pieces_of(m, so, do, bits, visit):
    for b in bits:
        done = m & ((1 << b) - 1)

        @pl.when(((m >> b) & 1) == 1)
        def _(b=b, done=done):
            visit(b, pl.multiple_of((so + done) * ROW_ALIGN, ROW_ALIGN),
                  pl.multiple_of((do + done) * ROW_ALIGN, ROW_ALIGN), ROW_ALIGN << b)


def _piece_lists(units, off_units, dst_units):
    bits = jnp.arange(RUN_BITS, dtype=jnp.int32)[None, :, None]
    m = units[:, None, :]
    has = ((m >> bits) & 1) == 1
    done = m & ((1 << bits) - 1)
    pos = jnp.cumsum(has, axis=-1) - 1
    at = has[..., None] & (pos[..., None] == jnp.arange(N_EXPERTS, dtype=jnp.int32))
    pick = lambda v: jnp.sum(jnp.where(at, v[..., None], 0), axis=2).reshape(-1).astype(jnp.int32)
    cnt = jnp.sum(has, axis=-1).reshape(-1).astype(jnp.int32)
    return cnt, pick(off_units[:, None, :] + done), pick(dst_units[:, None, :] + done)


def _run_pieces(cnt_ref, srow_ref, grow_ref, c, visit):
    for b in range(RUN_BITS):
        base = (c * RUN_BITS + b) * N_EXPERTS

        def body(j, carry, b=b, base=base):
            visit(b, j, pl.multiple_of(srow_ref[base + j] * ROW_ALIGN, ROW_ALIGN),
                  pl.multiple_of(grow_ref[base + j] * ROW_ALIGN, ROW_ALIGN), ROW_ALIGN << b)
            return carry

        lax.fori_loop(0, cnt_ref[c * RUN_BITS + b], body, 0)


def _dispatch_kernel(cnt_ref, srow_ref, grow_ref, tail_ref, tail_dst_ref, slot_t_ref, h_ref,
                     xs_ref, stage_ref, zero_ref, sems, *, nblk):
    c = pl.program_id(0)
    par = c % 2
    tb = h_ref.shape[0]
    for r0 in range(0, STAGE_ROWS, STAGE_CHUNK):
        srow = r0 + lax.broadcasted_iota(jnp.int32, (STAGE_CHUNK, tb), 0)
        perm = jnp.zeros((STAGE_CHUNK, tb), F32)
        for k in range(TOP_K):
            perm = jnp.where(srow == slot_t_ref[k:k + 1, :], 1.0, perm)
        perm = perm.astype(BF16)
        stage_ref[par, r0:r0 + STAGE_CHUNK, :] = jnp.dot(
            perm, h_ref[...], preferred_element_type=F32).astype(BF16)

    def piece(p):
        return lambda b, j, s_row, g_row, rows: pltpu.make_async_copy(
            stage_ref.at[p, pl.ds(s_row, rows), :], xs_ref.at[pl.ds(g_row, rows), :],
            sems.at[p, j, b])

    _run_pieces(cnt_ref, srow_ref, grow_ref, c, lambda *a: piece(par)(*a).start())

    @pl.when(c > 0)
    def _():
        _run_pieces(cnt_ref, srow_ref, grow_ref, c - 1, lambda *a: piece(1 - par)(*a).wait())

    @pl.when(c == nblk - 1)
    def _():
        _run_pieces(cnt_ref, srow_ref, grow_ref, c, lambda *a: piece(par)(*a).wait())
        zero_ref[...] = jnp.zeros(zero_ref.shape, BF16)

        def tail(e, b, z_row, g_row, rows):
            del z_row
            return pltpu.make_async_copy(zero_ref.at[pl.ds(0, rows), :],
                                         xs_ref.at[pl.ds(g_row, rows), :], sems.at[0, e, b])

        def tails(act):
            for e in range(N_EXPERTS):
                _pieces_of(tail_ref[e], 0, tail_dst_ref[e], range(TAIL_BITS),
                           lambda b, z, g, rows, e=e: act(tail(e, b, z, g, rows)))

        tails(lambda cp: cp.start())
        tails(lambda cp: cp.wait())

        zrows = zero_ref.shape[0]
        used = (tail_dst_ref[N_EXPERTS - 1] + tail_ref[N_EXPERTS - 1]) * ROW_ALIGN
        n_rest = (xs_ref.shape[0] - used) // zrows

        def rest(i):
            row0 = pl.multiple_of(used + i * zrows, ROW_ALIGN)
            return pltpu.make_async_copy(
                zero_ref, xs_ref.at[pl.ds(row0, zrows), :],
                sems.at[i // (N_EXPERTS * RUN_BITS), (i // RUN_BITS) % N_EXPERTS, i % RUN_BITS])

        lax.fori_loop(0, n_rest, lambda i, _: (rest(i).start(), 0)[1], 0)
        lax.fori_loop(0, n_rest, lambda i, _: (rest(i).wait(), 0)[1], 0)


def _dispatch(units, off, dst, tail, tail_dst, slot_t, h1b, n_rows):
    T = h1b.shape[0]
    nblk = T // ROUTE_TB
    grid_spec = pltpu.PrefetchScalarGridSpec(
        num_scalar_prefetch=5,
        grid=(nblk,),
        in_specs=[pl.BlockSpec((8, ROUTE_TB), lambda c, *_: (0, c)),
                  pl.BlockSpec((ROUTE_TB, D_MODEL), lambda c, *_: (c, 0))],
        out_specs=pl.BlockSpec(memory_space=pl.ANY),
        scratch_shapes=[pltpu.VMEM((2, STAGE_ROWS, D_MODEL), BF16),
                        pltpu.VMEM((ROW_ALIGN << (TAIL_BITS - 1), D_MODEL), BF16),
                        pltpu.SemaphoreType.DMA((2, N_EXPERTS, RUN_BITS))],
    )
    return pl.pallas_call(
        functools.partial(_dispatch_kernel, nblk=nblk),
        grid_spec=grid_spec,
        out_shape=jax.ShapeDtypeStruct((n_rows, D_MODEL), BF16),
        compiler_params=pltpu.CompilerParams(
            dimension_semantics=("arbitrary",), vmem_limit_bytes=VMEM_LIMIT),
    )(units, off, dst, tail, tail_dst, slot_t, h1b)


def _ffn_kernel(tile_e_ref, n_tiles_ref, xs_ref, wgu_ref, bgu_ref, wd_ref, bd_ref, y_ref,
                wgu_b, wd_b, *, n_chunk):
    j = pl.program_id(0)
    active = j < n_tiles_ref[0]
    new_expert = (j == 0) | (tile_e_ref[j] != tile_e_ref[jnp.maximum(j - 1, 0)])

    def tile(refresh):
        def weights(dst, src, rows, cols):
            if refresh:
                dst[rows, cols] = src[0, rows, cols].astype(BF16)
            return dst[rows, cols]

        xs = xs_ref[...]
        cw = D_FF // n_chunk
        full = slice(None)
        acc = jnp.zeros(y_ref.shape, F32)
        for c in range(n_chunk):
            cols = slice(c * cw, (c + 1) * cw)
            ucols = slice(D_FF + c * cw, D_FF + (c + 1) * cw)
            gate = jnp.dot(xs, weights(wgu_b, wgu_ref, full, cols), preferred_element_type=F32)
            gate = gate + bgu_ref[0, :, cols]
            up = jnp.dot(xs, weights(wgu_b, wgu_ref, full, ucols), preferred_element_type=F32)
            up = up + bgu_ref[0, :, ucols]
            gate = jnp.minimum(gate, SWIGLU_LIMIT)
            up = jnp.clip(up, -SWIGLU_LIMIT, SWIGLU_LIMIT)
            act = (up + 1.0) * gate * jax.nn.sigmoid(SWIGLU_ALPHA * gate)
            acc = acc + jnp.dot(act.astype(BF16), weights(wd_b, wd_ref, cols, full),
                                preferred_element_type=F32)
        y_ref[...] = (acc + bd_ref[0]).astype(BF16)

    pl.when(active & new_expert)(lambda: tile(True))
    pl.when(active & jnp.logical_not(new_expert))(lambda: tile(False))

    @pl.when(jnp.logical_not(active))
    def _():
        y_ref[...] = jnp.zeros(y_ref.shape, BF16)


def _ffn(tile_e, n_tiles, xs, wgu, bgu, wd, bd):
    n_rows = xs.shape[0]
    last = lambda j, nt: jnp.minimum(j, nt[0] - 1)
    row = lambda j, te, nt: (last(j, nt), 0)
    exp3 = lambda j, te, nt: (te[last(j, nt)], 0, 0)
    grid_spec = pltpu.PrefetchScalarGridSpec(
        num_scalar_prefetch=2,
        grid=(n_rows // FFN_TM,),
        in_specs=[
            pl.BlockSpec((FFN_TM, D_MODEL), row),
            pl.BlockSpec((1, D_MODEL, 2 * D_FF), exp3),
            pl.BlockSpec((1, 1, 2 * D_FF), exp3),
            pl.BlockSpec((1, D_FF, D_MODEL), exp3),
            pl.BlockSpec((1, 1, D_MODEL), exp3),
        ],
        out_specs=pl.BlockSpec((FFN_TM, D_MODEL), lambda j, te, nt: (j, 0)),
        scratch_shapes=[pltpu.VMEM((D_MODEL, 2 * D_FF), BF16), pltpu.VMEM((D_FF, D_MODEL), BF16)],
    )
    return pl.pallas_call(
        functools.partial(_ffn_kernel, n_chunk=2),
        grid_spec=grid_spec,
        out_shape=jax.ShapeDtypeStruct((n_rows, D_MODEL), BF16),
        compiler_params=pltpu.CompilerParams(
            dimension_semantics=("arbitrary",), vmem_limit_bytes=FFN_VMEM_LIMIT),
    )(tile_e, n_tiles, xs, wgu, bgu, wd, bd)


def _combine_kernel(cnt_ref, srow_ref, grow_ref, slot_ref, gates_ref, h1_ref, ys_ref,
                    g_ref, b_ref, o_ref, stage_ref, sems, *, nblk):
    c = pl.program_id(0)
    par = c % 2
    tb = h1_ref.shape[0]

    def piece(p):
        return lambda b, j, s_row, g_row, rows: pltpu.make_async_copy(
            ys_ref.at[pl.ds(g_row, rows), :], stage_ref.at[p, pl.ds(s_row, rows), :],
            sems.at[p, j, b])

    @pl.when(c == 0)
    def _():
        stage_ref[...] = jnp.zeros(stage_ref.shape, BF16)
        _run_pieces(cnt_ref, srow_ref, grow_ref, c, lambda *a: piece(0)(*a).start())

    _run_pieces(cnt_ref, srow_ref, grow_ref, c, lambda *a: piece(par)(*a).wait())

    @pl.when(c + 1 < nblk)
    def _():
        _run_pieces(cnt_ref, srow_ref, grow_ref, c + 1, lambda *a: piece(1 - par)(*a).start())

    slot = slot_ref[...]
    gates = gates_ref[...]
    ffn = jnp.zeros((tb, D_MODEL), F32)
    for r0 in range(0, STAGE_ROWS, COMBINE_CHUNK):
        scol = r0 + lax.broadcasted_iota(jnp.int32, (tb, COMBINE_CHUNK), 1)
        w = jnp.zeros((tb, COMBINE_CHUNK), F32)
        for k in range(TOP_K):
            w = jnp.where(scol == slot[:, k:k + 1], gates[:, k:k + 1], w)
        ffn = ffn + jnp.dot(w.astype(BF16), stage_ref[par, r0:r0 + COMBINE_CHUNK, :],
                            preferred_element_type=F32)
    o_ref[...] = _layer_norm(ALPHA_RES * h1_ref[...] + ffn, g_ref[...], b_ref[...])


def _combine(units, off, dst, slot, gates, h1, ys, g, b):
    T = h1.shape[0]
    nblk = T // ROUTE_TB
    blk = lambda c, *_: (c, 0)
    fixed = lambda c, *_: (0, 0)
    grid_spec = pltpu.PrefetchScalarGridSpec(
        num_scalar_prefetch=3,
        grid=(nblk,),
        in_specs=[pl.BlockSpec((ROUTE_TB, LANES), blk),
                  pl.BlockSpec((ROUTE_TB, LANES), blk),
                  pl.BlockSpec((ROUTE_TB, D_MODEL), blk),
                  pl.BlockSpec(memory_space=pl.ANY),
                  pl.BlockSpec((1, D_MODEL), fixed),
                  pl.BlockSpec((1, D_MODEL), fixed)],
        out_specs=pl.BlockSpec((ROUTE_TB, D_MODEL), blk),
        scratch_shapes=[pltpu.VMEM((2, STAGE_ROWS, D_MODEL), BF16),
                        pltpu.SemaphoreType.DMA((2, N_EXPERTS, RUN_BITS))],
    )
    return pl.pallas_call(
        functools.partial(_combine_kernel, nblk=nblk),
        grid_spec=grid_spec,
        out_shape=jax.ShapeDtypeStruct((T, D_MODEL), F32),
        compiler_params=pltpu.CompilerParams(
            dimension_semantics=("arbitrary",), vmem_limit_bytes=VMEM_LIMIT),
    )(units, off, dst, slot, gates, h1, ys, g, b)


def kernel(x, positions, ln_in_g, ln_in_b, w_in, w_pool, pool_scale, w_out, ln1_g, ln1_b,
           w_router, b_router, w_gate_up, b_gate_up, w_down, b_down, ln2_g, ln2_b):
    B, L, D = x.shape
    T = B * L
    assert T % ROUTE_TB == 0 and D == D_MODEL
    tq = min(512, L)
    x2 = x.reshape(T, D)
    pos2 = positions.reshape(1, T)
    gin = ln_in_g.reshape(1, D)
    bin_ = ln_in_b.reshape(1, D)

    w_pad = jnp.pad(w_in[0], ((0, 0), (0, IN_WIDTH_PAD - IN_WIDTH))).astype(BF16)
    u, q, k, v, iq, ik, iw = _inproj(x2, pos2, gin, bin_, w_pad, tq)

    attn = _attn(q, k, v, iq, ik, iw, B, L, min(256, L // CAUSAL_BANDS))

    wpool_bd = jnp.zeros((POOL_WIDTH, POOL_WIDTH), F32)
    for gi in range(POOL_GROUPS):
        sl = slice(gi * POOL_GDIM, (gi + 1) * POOL_GDIM)
        wpool_bd = wpool_bd.at[sl, sl].set(w_pool[0, gi])
    wr = jnp.pad(w_router[0], ((0, 0), (0, LANES - N_EXPERTS)))
    wr_hi = wr.astype(BF16)
    wr_lo = (wr - wr_hi.astype(F32)).astype(BF16)
    br = jnp.pad(b_router[0], (0, LANES - N_EXPERTS)).reshape(1, LANES)
    h1, h1b, gates, slot, slot_t, meta = _outproj(
        x2, u, attn, gin, bin_, wpool_bd.astype(BF16), pool_scale[0].reshape(1, POOL_WIDTH),
        w_out[0].astype(BF16), ln1_g[0].reshape(1, D), ln1_b[0].reshape(1, D),
        wr_hi, wr_lo, br, L, tq)

    nblk = T // ROUTE_TB
    meta = meta.reshape(nblk, 8, LANES)[:, :, :N_EXPERTS]
    units, off_units, base_units = meta[:, 0], meta[:, 1], meta[:, 2]
    tile_units = FFN_TM // ROW_ALIGN
    total_units = base_units[-1] + units[-1]
    region_units = ((total_units + tile_units - 1) // tile_units) * tile_units
    region_end = jnp.cumsum(region_units)
    dst_units = (region_end - region_units)[None, :] + base_units
    max_rows = T * TOP_K + nblk * N_EXPERTS * (ROW_ALIGN - 1) + N_EXPERTS * (FFN_TM - 1)
    n_rows = -(-max_rows // FFN_TM) * FFN_TM
    tile_start = jnp.arange(n_rows // FFN_TM, dtype=jnp.int32) * tile_units
    tile_e = jnp.minimum(jnp.sum(tile_start[:, None] >= region_end[None, :], axis=1),
                         N_EXPERTS - 1).astype(jnp.int32)
    n_tiles = (region_end[-1:] // tile_units).astype(jnp.int32)
    cnt, srow, grow = _piece_lists(units, off_units, dst_units)
    tail_units = (region_units - total_units).astype(jnp.int32)
    tail_dst = (region_end - tail_units).astype(jnp.int32)

    xs = _dispatch(cnt, srow, grow, tail_units, tail_dst, slot_t, h1b, n_rows)
    ys = _ffn(tile_e, n_tiles, xs, w_gate_up[0], b_gate_up[0].reshape(N_EXPERTS, 1, 2 * D_FF),
              w_down[0], b_down[0].reshape(N_EXPERTS, 1, D))
    out = _combine(cnt, srow, grow, slot, gates, h1, ys,
                   ln2_g[0].reshape(1, D), ln2_b[0].reshape(1, D))
    return out.reshape(B, L, D)
```
